```python
import jax, jax.numpy as jnp
from jax import lax
import numpy as np

D_MODEL = 1024
BATCH = 1
SEQ = 16384
DEPTH = 1

GRID_W = 64
CTX_LEN = 256
HEAD_DIM = 64
N_Q_HEADS = 12
N_KV_HEADS = 4
Q_PER_KV = N_Q_HEADS // N_KV_HEADS
ATTN_WIDTH = N_Q_HEADS * HEAD_DIM
KV_WIDTH = N_KV_HEADS * HEAD_DIM
N_FOURIER_GROUPS = 4
FOURIER_GROUP_DIM = 64
FOURIER_WIDTH = N_FOURIER_GROUPS * FOURIER_GROUP_DIM
MIX_WIDTH = FOURIER_WIDTH + ATTN_WIDTH
IN_WIDTH = FOURIER_WIDTH + ATTN_WIDTH + 2 * KV_WIDTH
Q_BLOCK = 128
ROPE_BASE = 10000.0
N_EXPERTS = 16
EC_CAPACITY_FACTOR = 2
D_EXPERT = 2048
N_MOD = 6
NORM_EPS = 1e-6

kernel_name = "hymba_fnet_gqa_ec_moe_dit"


def rms_norm(x, g):
    xf = x.astype(jnp.float32)
    y = xf * lax.rsqrt(jnp.mean(xf * xf, axis=-1, keepdims=True) + NORM_EPS)
    return (y * g.astype(jnp.float32)).astype(x.dtype)


def modulate(x, g, shift, scale):
    return rms_norm(x, g) * (1 + scale) + shift


def split_projection(p):
    f = p[..., :FOURIER_WIDTH]
    q = p[..., FOURIER_WIDTH:FOURIER_WIDTH + ATTN_WIDTH]
    k = p[..., FOURIER_WIDTH + ATTN_WIDTH:FOURIER_WIDTH + ATTN_WIDTH + KV_WIDTH]
    v = p[..., FOURIER_WIDTH + ATTN_WIDTH + KV_WIDTH:]
    return f, q, k, v


def fourier_mix(f, w_four):
    B, N = f.shape[:2]
    fg = f.reshape(B, N, N_FOURIER_GROUPS, FOURIER_GROUP_DIM).astype(jnp.float32)
    y = jnp.fft.fftn(fg, axes=(1, 3), norm="ortho").real.astype(f.dtype)
    y = jnp.einsum('bngc,gcd->bngd', y, w_four)
    return y.reshape(B, N, FOURIER_WIDTH)


def rope_1d(x, pos):
    half = x.shape[-1] // 2
    freqs = ROPE_BASE ** (-jnp.arange(half, dtype=jnp.float32) / half)
    ang = pos.astype(jnp.float32)[:, None] * freqs[None, :]
    cos = jnp.cos(ang)[None, :, None, :]
    sin = jnp.sin(ang)[None, :, None, :]
    x1 = x[..., :half].astype(jnp.float32)
    x2 = x[..., half:].astype(jnp.float32)
    return jnp.concatenate([x1 * cos - x2 * sin, x1 * sin + x2 * cos], axis=-1).astype(x.dtype)


def axial_rope(x, row, col):
    h = HEAD_DIM // 2
    return jnp.concatenate([rope_1d(x[..., :h], row), rope_1d(x[..., h:], col)], axis=-1)


def heads(t, n_heads, g):
    B, N = t.shape[:2]
    return rms_norm(t.reshape(B, N, n_heads, HEAD_DIM), g)


def gqa_block(qi, k_all, v_all):
    s = jnp.einsum('bqkgd,bskd->bkgqs', qi, k_all).astype(jnp.float32) * (HEAD_DIM ** -0.5)
    p = jax.nn.softmax(s, axis=-1).astype(v_all.dtype)
    return jnp.einsum('bkgqs,bskd->bqkgd', p, v_all)


def latent_attention(q, k_lat, v_lat, k_ctx, v_ctx):
    B, N = q.shape[:2]
    k_all = jnp.concatenate([k_lat, k_ctx], axis=1)
    v_all = jnp.concatenate([v_lat, v_ctx], axis=1)
    nb = N // Q_BLOCK
    qb = q.reshape(B, nb, Q_BLOCK, N_KV_HEADS, Q_PER_KV, HEAD_DIM).swapaxes(0, 1)
    o = lax.map(lambda qi: gqa_block(qi, k_all, v_all), qb)
    return o.swapaxes(0, 1).reshape(B, N, ATTN_WIDTH)


def context_attention(q, k, v):
    B, L = q.shape[:2]
    qg = q.reshape(B, L, N_KV_HEADS, Q_PER_KV, HEAD_DIM)
    return gqa_block(qg, k, v).reshape(B, L, ATTN_WIDTH)


def expert_choice_ffn(h, w_router, w_gate, w_up, w_down):
    B, N, D = h.shape
    cap = EC_CAPACITY_FACTOR * N // N_EXPERTS
    aff = jax.nn.softmax(jnp.einsum('bnd,de->bne', h, w_router).astype(jnp.float32), axis=-1)
    gates, idx = lax.top_k(aff.swapaxes(1, 2), cap)
    bidx = jnp.arange(B)[:, None, None]
    xe = h[bidx, idx]
    hid = jax.nn.silu(jnp.einsum('becd,edf->becf', xe, w_gate)) * jnp.einsum('becd,edf->becf', xe, w_up)
    ye = jnp.einsum('becf,efd->becd', hid, w_down) * gates[..., None].astype(h.dtype)
    return jnp.zeros_like(h).at[bidx, idx].add(ye)


def setup_inputs(seed: int = 0) -> dict:
    key = jax.random.key(seed)
    ks = jax.random.split(key, 20)
    f32 = jnp.float32
    nrm = lambda k, shp, s: jax.random.normal(k, shp, f32) * s
    return {
        "x": nrm(ks[0], (BATCH, SEQ, D_MODEL), 1.0),
        "c": nrm(ks[1], (BATCH, D_MODEL), 1.0),
        "ctx": nrm(ks[2], (BATCH, CTX_LEN, D_MODEL), 1.0),
        "c_ctx": nrm(ks[3], (D_MODEL,), 1.0),
        "w_mod": nrm(ks[4], (DEPTH, D_MODEL, N_MOD * D_MODEL), 0.5 * D_MODEL ** -0.5),
        "b_mod": nrm(ks[5], (DEPTH, N_MOD * D_MODEL), 0.02),
        "norm_mix_g": 1.0 + nrm(ks[6], (DEPTH, D_MODEL), 0.1),
        "norm_ffn_g": 1.0 + nrm(ks[7], (DEPTH, D_MODEL), 0.1),
        "w_in": nrm(ks[8], (DEPTH, D_MODEL, IN_WIDTH), D_MODEL ** -0.5),
        "w_four": nrm(ks[9], (DEPTH, N_FOURIER_GROUPS, FOURIER_GROUP_DIM, FOURIER_GROUP_DIM), FOURIER_GROUP_DIM ** -0.5),
        "q_norm_g": 1.0 + nrm(ks[10], (DEPTH, HEAD_DIM), 0.1),
        "k_norm_g": 1.0 + nrm(ks[11], (DEPTH, HEAD_DIM), 0.1),
        "w_out": nrm(ks[12], (DEPTH, MIX_WIDTH, D_MODEL), MIX_WIDTH ** -0.5),
        "w_router": nrm(ks[13], (DEPTH, D_MODEL, N_EXPERTS), D_MODEL ** -0.5),
        "w_gate": nrm(ks[14], (DEPTH, N_EXPERTS, D_MODEL, D_EXPERT), D_MODEL ** -0.5),
        "w_up": nrm(ks[15], (DEPTH, N_EXPERTS, D_MODEL, D_EXPERT), D_MODEL ** -0.5),
        "w_down": nrm(ks[16], (DEPTH, N_EXPERTS, D_EXPERT, D_MODEL), D_EXPERT ** -0.5),
        "final_norm_g": 1.0 + nrm(ks[17], (D_MODEL,), 0.1),
    }


def reference(x, c, ctx, c_ctx, w_mod, b_mod, norm_mix_g, norm_ffn_g, w_in, w_four,
              q_norm_g, k_norm_g, w_out, w_router, w_gate, w_up, w_down, final_norm_g):
    B, N, _ = x.shape
    rows = N // GRID_W
    row = jnp.repeat(jnp.arange(rows, dtype=jnp.int32), GRID_W)
    col = jnp.tile(jnp.arange(GRID_W, dtype=jnp.int32), rows)

    for l in range(DEPTH):
        last = l == DEPTH - 1
        mod_x = (jax.nn.silu(c) @ w_mod[l] + b_mod[l])[:, None, :]
        mod_c = (jax.nn.silu(c_ctx) @ w_mod[l] + b_mod[l])[None, None, :]
        sh1, sc1, g1, sh2, sc2, g2 = jnp.split(mod_x, N_MOD, axis=-1)
        csh1, csc1, cg1, csh2, csc2, cg2 = jnp.split(mod_c, N_MOD, axis=-1)

        hx = modulate(x, norm_mix_g[l], sh1, sc1)
        hc = modulate(ctx, norm_mix_g[l], csh1, csc1)
        fx, qx, kx, vx = split_projection(hx @ w_in[l])
        fc, qc, kc, vc = split_projection(hc @ w_in[l])

        qx = axial_rope(heads(qx, N_Q_HEADS, q_norm_g[l]), row, col)
        kx = axial_rope(heads(kx, N_KV_HEADS, k_norm_g[l]), row, col)
        vx = vx.reshape(B, N, N_KV_HEADS, HEAD_DIM)
        kc = heads(kc, N_KV_HEADS, k_norm_g[l])
        vc = vc.reshape(B, ctx.shape[1], N_KV_HEADS, HEAD_DIM)

        attn_x = latent_attention(qx, kx, vx, kc, vc)
        four_x = fourier_mix(fx, w_four[l])
        x = x + g1 * (jnp.concatenate([four_x, attn_x], axis=-1) @ w_out[l])

        x = x + g2 * expert_choice_ffn(modulate(x, norm_ffn_g[l], sh2, sc2),
                                       w_router[l], w_gate[l], w_up[l], w_down[l])

        if not last:
            qc = heads(qc, N_Q_HEADS, q_norm_g[l])
            attn_c = context_attention(qc, kc, vc)
            four_c = fourier_mix(fc, w_four[l])
            ctx = ctx + cg1 * (jnp.concatenate([four_c, attn_c], axis=-1) @ w_out[l])
            ctx = ctx + cg2 * expert_choice_ffn(modulate(ctx, norm_ffn_g[l], csh2, csc2),
                                                w_router[l], w_gate[l], w_up[l], w_down[l])

    return rms_norm(x, final_norm_g)
```

```python
import functools
import math

import numpy as np
import jax
import jax.numpy as jnp
from jax import lax
from jax.experimental import pallas as pl
from jax.experimental.pallas import tpu as pltpu

D_MODEL = 1024
GRID_W = 64
HEAD_DIM = 64
N_Q_HEADS = 12
N_KV_HEADS = 4
Q_PER_KV = N_Q_HEADS // N_KV_HEADS
ATTN_WIDTH = N_Q_HEADS * HEAD_DIM
KV_WIDTH = N_KV_HEADS * HEAD_DIM
N_FOURIER_GROUPS = 4
FOURIER_GROUP_DIM = 64
FOURIER_WIDTH = N_FOURIER_GROUPS * FOURIER_GROUP_DIM
IN_WIDTH = FOURIER_WIDTH + ATTN_WIDTH + 2 * KV_WIDTH
ROPE_BASE = 10000.0
N_EXPERTS = 16
EC_CAPACITY_FACTOR = 2
D_EXPERT = 2048
N_MOD = 6
NORM_EPS = 1e-6

LANES = 128
BF16_SUBLANES = 16
GROUP_PAD = 2 * LANES
VMEM_LIMIT = 56 * 1024 * 1024

F32 = jnp.float32
BF16 = jnp.bfloat16


def _params(sem):
    return pltpu.CompilerParams(dimension_semantics=sem, vmem_limit_bytes=VMEM_LIMIT)


def _mod_kernel(c_ref, w_ref, b_ref, o_ref):
    s = jax.nn.silu(c_ref[...])
    o_ref[...] = jnp.dot(s, w_ref[...], precision=lax.Precision.HIGHEST,
                         preferred_element_type=F32) + b_ref[...]


def _mod(cc, w_mod, b_mod):
    tn = 1024
    n = w_mod.shape[1]
    return pl.pallas_call(
        _mod_kernel,
        grid=(n // tn,),
        in_specs=[pl.BlockSpec((8, D_MODEL), lambda j: (0, 0)),
                  pl.BlockSpec((D_MODEL, tn), lambda j: (0, j)),
                  pl.BlockSpec((1, tn), lambda j: (0, j))],
        out_specs=pl.BlockSpec((8, tn), lambda j: (0, j)),
        out_shape=jax.ShapeDtypeStruct((8, n), F32),
        compiler_params=_params(("arbitrary",)),
        name="mod",
    )(cc, w_mod, b_mod)


def _rms_modulate(x, g, shift, scale):
    ms = jnp.mean(x * x, axis=-1, keepdims=True)
    y = x * lax.rsqrt(ms + NORM_EPS) * g
    return y * (1.0 + scale) + shift


def _head_norm(t, bmat, gain):
    tt = t * t
    hi = tt.astype(BF16)
    lo = (tt - hi.astype(F32)).astype(BF16)
    ms = (jnp.dot(hi, bmat, preferred_element_type=F32)
          + jnp.dot(lo, bmat, preferred_element_type=F32))
    return t * lax.rsqrt(ms + NORM_EPS) * gain


def _rope(t, cos, sin, lo_mask):
    outs = []
    for c in range(t.shape[1] // LANES):
        xc = t[:, c * LANES:(c + 1) * LANES]
        up = pltpu.roll(xc, LANES - 16, axis=1)
        dn = pltpu.roll(xc, 16, axis=1)
        rot = jnp.where(lo_mask, up, dn)
        outs.append(xc * cos + rot * sin)
    return jnp.concatenate(outs, axis=1)


def _inproj_kernel(x_ref, mod_ref, g_ref, w_ref, cos_ref, sin_ref, bq_ref, bk_ref,
                   gq_ref, gk_ref, w0_ref, *out_refs, mod_row, with_q):
    if with_q:
        zr_ref, zi_ref, q_ref, k_ref, v_ref = out_refs
    else:
        k_ref, v_ref = out_refs
    tm = x_ref.shape[0]
    shift = mod_ref[mod_row:mod_row + 1, 0:D_MODEL]
    scale = mod_ref[mod_row:mod_row + 1, D_MODEL:2 * D_MODEL]
    h = _rms_modulate(x_ref[...], g_ref[...], shift, scale).astype(BF16)
    cos = cos_ref[...]
    sin = sin_ref[...]
    lane = lax.broadcasted_iota(jnp.int32, (tm, LANES), 1)
    lo_mask = (lane % 32) < 16
    k0 = FOURIER_WIDTH + ATTN_WIDTH
    if with_q:
        p = jnp.dot(h, w_ref[...], preferred_element_type=F32)
        f = p[:, :FOURIER_WIDTH].astype(BF16)
        z = jnp.dot(f, w0_ref[...], preferred_element_type=F32)
        zr_ref[...] = z[:, :FOURIER_WIDTH].astype(BF16)
        zi_ref[...] = z[:, FOURIER_WIDTH:].astype(BF16)
        q = _head_norm(p[:, FOURIER_WIDTH:k0], bq_ref[...], gq_ref[...])
        q = _rope(q, cos, sin, lo_mask)
        q_ref[...] = q.T.astype(BF16).reshape(N_Q_HEADS, HEAD_DIM, tm)
        pk = p[:, k0:k0 + KV_WIDTH]
        pv = p[:, k0 + KV_WIDTH:]
    else:
        pkv = jnp.dot(h, w_ref[:, k0:], preferred_element_type=F32)
        pk = pkv[:, :KV_WIDTH]
        pv = pkv[:, KV_WIDTH:]
    k = _head_norm(pk, bk_ref[...], gk_ref[...])
    k = _rope(k, cos, sin, lo_mask)
    for g in range(N_KV_HEADS):
        k_ref[g] = k[:, g * HEAD_DIM:(g + 1) * HEAD_DIM].astype(BF16)
    v_ref[...] = pv.T.astype(BF16).reshape(N_KV_HEADS, HEAD_DIM, tm)


def _inproj(x, mod, g, w_in, cos, sin, bq, bk, gq, gk, w0, *, mod_row, with_q, tm):
    n = x.shape[0]
    const = lambda shape: pl.BlockSpec(shape, lambda i: (0,) * len(shape))
    in_specs = [pl.BlockSpec((tm, D_MODEL), lambda i: (i, 0)),
                const(mod.shape), const(g.shape), const(w_in.shape),
                pl.BlockSpec((tm, LANES), lambda i: (i, 0)),
                pl.BlockSpec((tm, LANES), lambda i: (i, 0)),
                const(bq.shape), const(bk.shape), const(gq.shape), const(gk.shape),
                const(w0.shape)]
    kv_specs = [pl.BlockSpec((N_KV_HEADS, tm, HEAD_DIM), lambda i: (0, i, 0)),
                pl.BlockSpec((N_KV_HEADS, HEAD_DIM, tm), lambda i: (0, 0, i))]
    kv_shapes = [jax.ShapeDtypeStruct((N_KV_HEADS, n, HEAD_DIM), BF16),
                 jax.ShapeDtypeStruct((N_KV_HEADS, HEAD_DIM, n), BF16)]
    if with_q:
        out_specs = [pl.BlockSpec((tm, FOURIER_WIDTH), lambda i: (i, 0)),
                     pl.BlockSpec((tm, FOURIER_WIDTH), lambda i: (i, 0)),
                     pl.BlockSpec((N_Q_HEADS, HEAD_DIM, tm), lambda i: (0, 0, i))] + kv_specs
        out_shapes = [jax.ShapeDtypeStruct((n, FOURIER_WIDTH), BF16),
                      jax.ShapeDtypeStruct((n, FOURIER_WIDTH), BF16),
                      jax.ShapeDtypeStruct((N_Q_HEADS, HEAD_DIM, n), BF16)] + kv_shapes
    else:
        out_specs, out_shapes = kv_specs, kv_shapes
    return pl.pallas_call(
        functools.partial(_inproj_kernel, mod_row=mod_row, with_q=with_q),
        grid=(n // tm,),
        in_specs=in_specs,
        out_specs=out_specs,
        out_shape=out_shapes,
        compiler_params=_params(("arbitrary",)),
        name="inproj_x" if with_q else "inproj_ctx",
    )(x, mod, g, w_in, cos, sin, bq, bk, gq, gk, w0)


def _attn_kernel(q_ref, k_ref, v_ref, kc_ref, vc_ref, o_ref, *, tk):
    tq = q_ref.shape[2]
    n = k_ref.shape[1]
    qs = jnp.concatenate([q_ref[j] for j in range(Q_PER_KV)], axis=1)

    def step(kb, vb, carry):
        m, l, acc = carry
        s = jnp.dot(kb, qs, preferred_element_type=F32)
        m_new = jnp.maximum(m, jnp.max(s, axis=0, keepdims=True))
        alpha = jnp.exp2(m - m_new)
        p = jnp.exp2(s - m_new)
        l = alpha * l + jnp.sum(p, axis=0, keepdims=True)
        acc = alpha * acc + jnp.dot(vb, p.astype(BF16), preferred_element_type=F32)
        return m_new, l, acc

    def body(c, carry):
        start = pl.multiple_of(c * tk, tk)
        return step(k_ref[0, pl.ds(start, tk), :], v_ref[0, :, pl.ds(start, tk)], carry)

    w = Q_PER_KV * tq
    init = (jnp.full((1, w), -1e30, F32), jnp.zeros((1, w), F32), jnp.zeros((HEAD_DIM, w), F32))
    carry = lax.fori_loop(0, n // tk, body, init)
    m, l, acc = step(kc_ref[0], vc_ref[0], carry)
    o = acc / l
    ot = jnp.concatenate([o, jnp.zeros_like(o)], axis=0).T
    h0, h1, h2 = (ot[j * tq:(j + 1) * tq] for j in range(Q_PER_KV))
    o_ref[0] = jnp.concatenate([h0 + pltpu.roll(h1, HEAD_DIM, axis=1), h2], axis=1).astype(BF16)


def _attention(q, k, v, kc, vc, *, tq, tk):
    n = k.shape[1]
    lc = kc.shape[1]
    gw = GROUP_PAD
    return pl.pallas_call(
        functools.partial(_attn_kernel, tk=tk),
        grid=(N_KV_HEADS, n // tq),
        in_specs=[pl.BlockSpec((Q_PER_KV, HEAD_DIM, tq), lambda g, i: (g, 0, i)),
                  pl.BlockSpec((1, n, HEAD_DIM), lambda g, i: (g, 0, 0)),
                  pl.BlockSpec((1, HEAD_DIM, n), lambda g, i: (g, 0, 0)),
                  pl.BlockSpec((1, lc, HEAD_DIM), lambda g, i: (g, 0, 0)),
                  pl.BlockSpec((1, HEAD_DIM, lc), lambda g, i: (g, 0, 0))],
        out_specs=pl.BlockSpec((1, tq, gw), lambda g, i: (g, i, 0)),
        out_shape=jax.ShapeDtypeStruct((N_KV_HEADS, n, gw), BF16),
        compiler_params=_params(("arbitrary", "arbitrary")),
        name="attention",
    )(q, k, v, kc, vc)


def _dft_a_kernel(zr_ref, zi_ref, w_ref, ar_ref, ai_ref):
    na = zr_ref.shape[0]
    zz = jnp.concatenate([zr_ref[...], zi_ref[...]], axis=0)
    a = jnp.dot(w_ref[...], zz, preferred_element_type=F32)
    ar_ref[...] = a[:na].astype(BF16)
    ai_ref[...] = a[na:].astype(BF16)


def _dft_a(zr2, zi2, wa, *, cb):
    na, cols = zr2.shape
    blk = pl.BlockSpec((na, cb), lambda j: (0, j))
    return pl.pallas_call(
        _dft_a_kernel,
        grid=(cols // cb,),
        in_specs=[blk, blk, pl.BlockSpec(wa.shape, lambda j: (0, 0))],
        out_specs=[blk, blk],
        out_shape=[jax.ShapeDtypeStruct((na, cols), BF16)] * 2,
        compiler_params=_params(("arbitrary",)),
        name="dft_a",
    )(zr2, zi2, wa)


def _dft_c_kernel(ar_ref, ai_ref, m_ref, wf_ref, o_ref):
    kb = ar_ref.shape[0]
    outs = []
    for j in range(kb):
        ab = jnp.concatenate([ar_ref[j], ai_ref[j]], axis=0)
        y = jnp.dot(m_ref[j], ab, preferred_element_type=F32)
        outs.append(jnp.dot(y.astype(BF16), wf_ref[...], preferred_element_type=F32))
    o_ref[...] = jnp.concatenate(outs, axis=1).astype(BF16)


def _dft_c(ar3, ai3, mm, wf, *, kb):
    na, nb, fw = ar3.shape
    blk = pl.BlockSpec((kb, nb, fw), lambda i: (i, 0, 0))
    return pl.pallas_call(
        _dft_c_kernel,
        grid=(na // kb,),
        in_specs=[blk, blk,
                  pl.BlockSpec((kb, nb, 2 * nb), lambda i: (i, 0, 0)),
                  pl.BlockSpec(wf.shape, lambda i: (0, 0))],
        out_specs=pl.BlockSpec((nb, kb * fw), lambda i: (0, i)),
        out_shape=jax.ShapeDtypeStruct((nb, na * fw), BF16),
        compiler_params=_params(("arbitrary",)),
        name="dft_c",
    )(ar3, ai3, mm, wf)


def _outproj_kernel(four_ref, attn_ref, x_ref, mod_ref, g_ref, wf_ref, wa_ref, wr_ref,
                    x1_ref, h_ref, aff_ref):
    mix = jnp.dot(four_ref[...], wf_ref[...], preferred_element_type=F32)
    for g in range(N_KV_HEADS):
        mix = mix + jnp.dot(attn_ref[g], wa_ref[g], preferred_element_type=F32)
    gate1 = mod_ref[0:1, 2 * D_MODEL:3 * D_MODEL]
    shift2 = mod_ref[0:1, 3 * D_MODEL:4 * D_MODEL]
    scale2 = mod_ref[0:1, 4 * D_MODEL:5 * D_MODEL]
    x1 = x_ref[...] + gate1 * mix
    x1_ref[...] = x1
    h = _rms_modulate(x1, g_ref[...], shift2, scale2).astype(BF16)
    h_ref[...] = h
    logits = lax.dot_general(wr_ref[...], h, (((1,), (1,)), ((), ())),
                             preferred_element_type=F32)
    mx = jnp.max(logits, axis=0, keepdims=True)
    e = jnp.exp(logits - mx)
    aff_ref[...] = e / jnp.sum(e, axis=0, keepdims=True)


def _outproj(four, attn, x, mod, g, wo_f, wo_a, wr_t, *, tm):
    n = x.shape[0]
    gw = GROUP_PAD
    const = lambda shape: pl.BlockSpec(shape, lambda i: (0,) * len(shape))
    return pl.pallas_call(
        _outproj_kernel,
        grid=(n // tm,),
        in_specs=[pl.BlockSpec((tm, FOURIER_WIDTH), lambda i: (i, 0)),
                  pl.BlockSpec((N_KV_HEADS, tm, gw), lambda i: (0, i, 0)),
                  pl.BlockSpec((tm, D_MODEL), lambda i: (i, 0)),
                  const(mod.shape), const(g.shape), const(wo_f.shape), const(wo_a.shape),
                  const(wr_t.shape)],
        out_specs=[pl.BlockSpec((tm, D_MODEL), lambda i: (i, 0)),
                   pl.BlockSpec((tm, D_MODEL), lambda i: (i, 0)),
                   pl.BlockSpec((N_EXPERTS, tm), lambda i: (0, i))],
        out_shape=[jax.ShapeDtypeStruct((n, D_MODEL), F32),
                   jax.ShapeDtypeStruct((n, D_MODEL), BF16),
                   jax.ShapeDtypeStruct((N_EXPERTS, n), F32)],
        compiler_params=_params(("arbitrary",)),
        name="outproj",
    )(four, attn, x, mod, g, wo_f, wo_a, wr_t)


def _count(m):
    return jnp.sum(jnp.sum(m, axis=2, keepdims=True), axis=1, keepdims=True)


def _topk_kernel(aff_ref, ut_ref, ones_ref, slt_ref, slot_ref, off_ref, *, cap):
    e_n, r_n, _ = aff_ref.shape
    aff = aff_ref[...]
    capf = jnp.float32(cap)

    def count_ge(th):
        return _count(jnp.where(aff >= th, 1.0, 0.0))

    def search(i, t):
        cand = t | jnp.left_shift(jnp.int32(1), 30 - i)
        c = count_ge(lax.bitcast_convert_type(cand, F32))
        return jnp.where(c >= capf, cand, t)

    tbits = lax.fori_loop(0, 31, search, jnp.zeros((e_n, 1, 1), jnp.int32))
    thr = lax.bitcast_convert_type(tbits, F32)
    ulp = lax.bitcast_convert_type(tbits + 1, F32) - thr

    def refine(_, carry):
        t, step = carry
        step = step * 0.5
        cand = t + step
        return jnp.where(count_ge(cand) >= capf, cand, t), step

    thr, _ = lax.fori_loop(0, 29, refine, (thr, ulp))
    gt = jnp.where(aff > thr, 1.0, 0.0)
    eq = jnp.where(aff == thr, 1.0, 0.0)
    need = capf - _count(gt)

    def excl_prefix(m):
        m2 = m.reshape(e_n * r_n, LANES).astype(BF16)
        within = jnp.dot(m2, ut_ref[...], preferred_element_type=F32).reshape(e_n, r_n, LANES)
        tot = jnp.dot(m2, ones_ref[...], preferred_element_type=F32).reshape(e_n, r_n, LANES)
        rows = jnp.stack([jnp.dot(slt_ref[...], tot[e].astype(BF16), preferred_element_type=F32)
                          for e in range(e_n)], axis=0)
        return within + rows, rows

    pe, _ = excl_prefix(eq)
    sel = jnp.maximum(gt, eq * jnp.where(pe < need, 1.0, 0.0))
    pos, rows = excl_prefix(sel)
    slot_ref[...] = jnp.where(sel > 0.5, pos, -1.0).astype(jnp.int32)
    off_ref[...] = rows.astype(jnp.int32)


def _topk(aff3, ut, ones, slt, *, cap):
    shp = aff3.shape
    full = lambda a: pl.BlockSpec(a.shape, lambda i: (0,) * a.ndim)
    return pl.pallas_call(
        functools.partial(_topk_kernel, cap=cap),
        grid=(1,),
        in_specs=[full(aff3), full(ut), full(ones), full(slt)],
        out_specs=[pl.BlockSpec(shp, lambda i: (0, 0, 0))] * 2,
        out_shape=[jax.ShapeDtypeStruct(shp, jnp.int32)] * 2,
        compiler_params=_params(("arbitrary",)),
        name="topk",
    )(aff3, ut, ones, slt)


def _window_start(off):
    return pl.multiple_of((off // BF16_SUBLANES) * BF16_SUBLANES, BF16_SUBLANES)


def _gather_kernel(off_ref, slot_ref, h_ref, xe_ref, acc_ref, *, win):
    e = pl.program_id(0)
    s = pl.program_id(1)
    tb = h_ref.shape[0]
    cap = xe_ref.shape[1]
    nsub = tb // LANES

    @pl.when(s == 0)
    def _():
        acc_ref[...] = jnp.zeros_like(acc_ref)

    rows = lax.broadcasted_iota(jnp.int32, (win, LANES), 0)
    for j in range(nsub):
        ws = _window_start(off_ref[e, s * nsub + j])
        rel = slot_ref[0, :, j * LANES:(j + 1) * LANES] - ws
        onehot = jnp.where(rows == rel, 1.0, 0.0).astype(BF16)
        acc_ref[pl.ds(ws, win), :] += jnp.dot(onehot, h_ref[j * LANES:(j + 1) * LANES, :],
                                              preferred_element_type=F32)

    @pl.when(s == pl.num_programs(1) - 1)
    def _():
        xe_ref[0] = acc_ref[0:cap, :].astype(BF16)


def _gather(off, slot3, h, *, cap, tb):
    n = h.shape[0]
    win = LANES + BF16_SUBLANES
    grid_spec = pltpu.PrefetchScalarGridSpec(
        num_scalar_prefetch=1,
        grid=(N_EXPERTS, n // tb),
        in_specs=[pl.BlockSpec((1, 1, tb), lambda e, s, off: (e, 0, s)),
                  pl.BlockSpec((tb, D_MODEL), lambda e, s, off: (s, 0))],
        out_specs=pl.BlockSpec((1, cap, D_MODEL), lambda e, s, off: (e, 0, 0)),
        scratch_shapes=[pltpu.VMEM((cap + win, D_MODEL), F32)])
    return pl.pallas_call(
        functools.partial(_gather_kernel, win=win),
        grid_spec=grid_spec,
        out_shape=jax.ShapeDtypeStruct((N_EXPERTS, cap, D_MODEL), BF16),
        compiler_params=_params(("arbitrary", "arbitrary")),
        name="gather",
    )(off, slot3, h)


def _ffn_kernel(xe_ref, wg_ref, wu_ref, wd_ref, ye_ref, acc_ref, *, mc):
    f = pl.program_id(1)
    cap = xe_ref.shape[1]
    wg = wg_ref[0].astype(BF16)
    wu = wu_ref[0].astype(BF16)
    wd = wd_ref[0].astype(BF16)
    for m in range(cap // mc):
        xm = xe_ref[0, m * mc:(m + 1) * mc, :]
        gate = jnp.dot(xm, wg, preferred_element_type=F32)
        up = jnp.dot(xm, wu, preferred_element_type=F32)
        hid = (jax.nn.silu(gate) * up).astype(BF16)
        y = jnp.dot(hid, wd, preferred_element_type=F32)

        @pl.when(f == 0)
        def _():
            acc_ref[m * mc:(m + 1) * mc, :] = y

        @pl.when(f != 0)
        def _():
            acc_ref[m * mc:(m + 1) * mc, :] += y

    @pl.when(f == pl.num_programs(1) - 1)
    def _():
        ye_ref[0] = acc_ref[...].astype(BF16)


def _ffn(xe, w_gate, w_up, w_down, *, tf, mc):
    e_n, cap, _ = xe.shape
    return pl.pallas_call(
        functools.partial(_ffn_kernel, mc=mc),
        grid=(e_n, D_EXPERT // tf),
        in_specs=[pl.BlockSpec((1, cap, D_MODEL), lambda e, f: (e, 0, 0)),
                  pl.BlockSpec((1, D_MODEL, tf), lambda e, f: (e, 0, f)),
                  pl.BlockSpec((1, D_MODEL, tf), lambda e, f: (e, 0, f)),
                  pl.BlockSpec((1, tf, D_MODEL), lambda e, f: (e, f, 0))],
        out_specs=pl.BlockSpec((1, cap, D_MODEL), lambda e, f: (e, 0, 0)),
        out_shape=jax.ShapeDtypeStruct((e_n, cap, D_MODEL), BF16),
        scratch_shapes=[pltpu.VMEM((cap, D_MODEL), F32)],
        compiler_params=_params(("arbitrary", "arbitrary")),
        name="ffn",
    )(xe, w_gate, w_up, w_down)


def _combine_kernel(off_ref, x1_ref, slot_ref, aff_ref, mod_ref, g_ref, ye_ref, o_ref,
                    buf_ref, sem_ref, *, win):
    t = pl.program_id(0)
    tt = x1_ref.shape[0]
    cap = ye_ref.shape[1]
    nsub = tt // LANES
    big = tt + BF16_SUBLANES

    def big_start(e):
        return pl.multiple_of(jnp.minimum(_window_start(off_ref[e, t * nsub]), cap - big),
                              BF16_SUBLANES)

    def copy(e, slot):
        return pltpu.make_async_copy(ye_ref.at[e, pl.ds(big_start(e), big), :],
                                     buf_ref.at[slot], sem_ref.at[slot])

    copy(0, 0).start()
    cols = lax.broadcasted_iota(jnp.int32, (LANES, win), 1)
    accs = [jnp.zeros((LANES, D_MODEL), F32) for _ in range(nsub)]
    for e in range(N_EXPERTS):
        slot = e % 2
        if e + 1 < N_EXPERTS:
            copy(e + 1, 1 - slot).start()
        copy(e, slot).wait()
        wb = big_start(e)
        for j in range(nsub):
            ws = jnp.minimum(_window_start(off_ref[e, t * nsub + j]), wb + (big - win))
            rel = slot_ref[j * LANES:(j + 1) * LANES, e:e + 1] - ws
            onehot = jnp.where(cols == rel, 1.0, 0.0).astype(BF16)
            r0 = pl.multiple_of(ws - wb, BF16_SUBLANES)
            y = jnp.dot(onehot, buf_ref[slot, pl.ds(r0, win), :], preferred_element_type=F32)
            accs[j] = accs[j] + aff_ref[j * LANES:(j + 1) * LANES, e:e + 1] * y
    gate2 = mod_ref[0:1, 5 * D_MODEL:6 * D_MODEL]
    x2 = x1_ref[...] + gate2 * jnp.concatenate(accs, axis=0)
    ms = jnp.mean(x2 * x2, axis=-1, keepdims=True)
    o_ref[...] = x2 * lax.rsqrt(ms + NORM_EPS) * g_ref[...]


def _combine(off, x1, slot_t, aff_t, mod, g, ye, *, tt):
    n = x1.shape[0]
    win = min(2 * LANES, tt + BF16_SUBLANES)
    grid_spec = pltpu.PrefetchScalarGridSpec(
        num_scalar_prefetch=1,
        grid=(n // tt,),
        in_specs=[pl.BlockSpec((tt, D_MODEL), lambda t, off: (t, 0)),
                  pl.BlockSpec((tt, N_EXPERTS), lambda t, off: (t, 0)),
                  pl.BlockSpec((tt, N_EXPERTS), lambda t, off: (t, 0)),
                  pl.BlockSpec(mod.shape, lambda t, off: (0, 0)),
                  pl.BlockSpec(g.shape, lambda t, off: (0, 0)),
                  pl.BlockSpec(memory_space=pl.ANY)],
        out_specs=pl.BlockSpec((tt, D_MODEL), lambda t, off: (t, 0)),
        scratch_shapes=[pltpu.VMEM((2, tt + BF16_SUBLANES, D_MODEL), BF16),
                        pltpu.SemaphoreType.DMA((2,))])
    return pl.pallas_call(
        functools.partial(_combine_kernel, win=win),
        grid_spec=grid_spec,
        out_shape=jax.ShapeDtypeStruct((n, D_MODEL), F32),
        compiler_params=_params(("arbitrary",)),
        name="combine",
    )(off, x1, slot_t, aff_t, mod, g, ye)


def _rope_tables(n):
    rows = n // GRID_W
    row = jnp.repeat(jnp.arange(rows, dtype=jnp.int32), GRID_W)
    col = jnp.tile(jnp.arange(GRID_W, dtype=jnp.int32), rows)
    half = HEAD_DIM // 4
    freqs = ROPE_BASE ** (-jnp.arange(half, dtype=F32) / half)
    ang_r = row.astype(F32)[:, None] * freqs[None, :]
    ang_c = col.astype(F32)[:, None] * freqs[None, :]
    cos_h = jnp.concatenate([jnp.cos(ang_r)] * 2 + [jnp.cos(ang_c)] * 2, axis=1)
    sin_h = jnp.concatenate([-jnp.sin(ang_r), jnp.sin(ang_r),
                             -jnp.sin(ang_c), jnp.sin(ang_c)], axis=1)
    return jnp.concatenate([cos_h] * 2, axis=1), jnp.concatenate([sin_h] * 2, axis=1)


def _block_diag(blocks):
    k = len(blocks)
    r, c = blocks[0].shape
    out = jnp.zeros((k * r, k * c), blocks[0].dtype)
    for i, b in enumerate(blocks):
        out = out.at[i * r:(i + 1) * r, i * c:(i + 1) * c].set(b)
    return out


def _dft_tables(n):
    nb = LANES
    na = n // nb
    gd = FOURIER_GROUP_DIM
    scale = 1.0 / math.sqrt(n * gd)

    def cos_sin(i, j, period):
        ang = ((i * j) % period).astype(F32) * (2.0 * math.pi / period)
        return jnp.cos(ang), jnp.sin(ang)

    ar_g = jnp.arange(gd, dtype=jnp.int32)
    c64, s64 = cos_sin(ar_g[:, None], ar_g[None, :], gd)
    w0 = jnp.concatenate([_block_diag([c64 * scale] * N_FOURIER_GROUPS),
                          _block_diag([-s64 * scale] * N_FOURIER_GROUPS)], axis=1)
    ar_a = jnp.arange(na, dtype=jnp.int32)
    ca, sa = cos_sin(ar_a[:, None], ar_a[None, :], na)
    wa = jnp.concatenate([jnp.concatenate([ca, sa], axis=1),
                          jnp.concatenate([-sa, ca], axis=1)], axis=0)
    ar_b = jnp.arange(nb, dtype=jnp.int32)
    ct, st = cos_sin(ar_a[:, None], ar_b[None, :], n)
    cb, sb = cos_sin(ar_b[:, None], ar_b[None, :], nb)
    mc = ct[:, None, :] * cb[None, :, :] - st[:, None, :] * sb[None, :, :]
    ms = st[:, None, :] * cb[None, :, :] + ct[:, None, :] * sb[None, :, :]
    mm = jnp.concatenate([mc, ms], axis=2)
    return w0.astype(BF16), wa.astype(BF16), mm.astype(BF16)


def _prefix_tables(r_n):
    i = np.arange(LANES)
    ut = (i[:, None] < i[None, :]).astype(np.float32)
    ones = np.ones((LANES, LANES), np.float32)
    r = np.arange(r_n)
    slt = (r[None, :] < r[:, None]).astype(np.float32)
    as_bf16 = lambda a: jnp.asarray(a).astype(BF16)
    return as_bf16(ut), as_bf16(ones), as_bf16(slt)


def kernel(x, c, ctx, c_ctx, w_mod, b_mod, norm_mix_g, norm_ffn_g, w_in, w_four, q_norm_g, k_norm_g,
           w_out, w_router, w_gate, w_up, w_down, final_norm_g):
    batch, n, _ = x.shape
    assert batch == 1 and w_mod.shape[0] == 1
    lc = ctx.shape[1]
    cap = EC_CAPACITY_FACTOR * n // N_EXPERTS
    x2d = x[0]
    ctx2d = ctx[0]

    cc = jnp.zeros((8, D_MODEL), F32).at[0].set(c[0]).at[1].set(c_ctx)
    mod = _mod(cc, w_mod[0], b_mod[0][None, :])

    cos, sin = _rope_tables(n)
    cos_c = jnp.ones((lc, LANES), F32)
    sin_c = jnp.zeros((lc, LANES), F32)
    inv_hd = jnp.full((HEAD_DIM, HEAD_DIM), 1.0 / HEAD_DIM, BF16)
    bq = _block_diag([inv_hd] * N_Q_HEADS)
    bk = _block_diag([inv_hd] * N_KV_HEADS)
    q_scale = (HEAD_DIM ** -0.5) * math.log2(math.e)
    gq = jnp.tile(q_norm_g[0], N_Q_HEADS)[None, :] * q_scale
    gk = jnp.tile(k_norm_g[0], N_KV_HEADS)[None, :]
    w0, wa, mm = _dft_tables(n)
    w_in_b = w_in[0].astype(BF16)
    g_mix = norm_mix_g[0][None, :]
    tm = min(512, n)
    zr, zi, q_t, k_l, v_t = _inproj(x2d, mod, g_mix, w_in_b, cos, sin, bq, bk, gq, gk, w0,
                                    mod_row=0, with_q=True, tm=tm)
    k_c, v_c = _inproj(ctx2d, mod, g_mix, w_in_b, cos_c, sin_c, bq, bk, gq, gk, w0,
                       mod_row=1, with_q=False, tm=lc)

    attn = _attention(q_t, k_l, v_t, k_c, v_c, tq=min(256, n), tk=min(512, n))

    nb = LANES
    na = n // nb
    ar, ai = _dft_a(zr.reshape(na, nb * FOURIER_WIDTH), zi.reshape(na, nb * FOURIER_WIDTH), wa,
                    cb=min(4096, nb * FOURIER_WIDTH))
    wf = _block_diag([w_four[0, g] for g in range(N_FOURIER_GROUPS)]).astype(BF16)
    four = _dft_c(ar.reshape(na, nb, FOURIER_WIDTH), ai.reshape(na, nb, FOURIER_WIDTH), mm, wf,
                  kb=8).reshape(n, FOURIER_WIDTH)

    w_out_b = w_out[0].astype(BF16)
    wo_f = w_out_b[:FOURIER_WIDTH]
    wo_a = w_out_b[FOURIER_WIDTH:].reshape(N_KV_HEADS, Q_PER_KV * HEAD_DIM, D_MODEL)
    wo_a = jnp.pad(wo_a, ((0, 0), (0, GROUP_PAD - Q_PER_KV * HEAD_DIM), (0, 0)))
    wr_t = w_router[0].T.astype(BF16)
    x1, h2, aff = _outproj(four, attn, x2d, mod, norm_ffn_g[0][None, :], wo_f, wo_a, wr_t, tm=tm)

    r_n = n // LANES
    ut, ones, slt = _prefix_tables(r_n)
    slot3, off3 = _topk(aff.reshape(N_EXPERTS, r_n, LANES), ut, ones, slt, cap=cap)
    off = off3[:, :, 0]
    slot = slot3.reshape(N_EXPERTS, n)

    xe = _gather(off, slot.reshape(N_EXPERTS, 1, n), h2, cap=cap, tb=min(1024, n))
    ye = _ffn(xe, w_gate[0], w_up[0], w_down[0], tf=512, mc=min(512, cap))
    tt = min(512, cap - BF16_SUBLANES)
    tt = (tt // LANES) * LANES
    out = _combine(off, x1, slot.T, aff.T, mod, final_norm_g[None, :], ye, tt=tt)
    return out[None]
```

```python
import functools
import math

import numpy as np
import jax
import jax.numpy as jnp
from jax import lax
from jax.experimental import pallas as pl
from jax.experimental.pallas import tpu as pltpu

D_MODEL = 1024
GRID_W = 64
HEAD_DIM = 64
N_Q_HEADS = 12
N_KV_HEADS = 4
Q_PER_KV = N_Q_HEADS // N_KV_HEADS
ATTN_WIDTH = N_Q_HEADS * HEAD_DIM
KV_WIDTH = N_KV_HEADS * HEAD_DIM
N_FOURIER_GROUPS = 4
FOURIER_GROUP_DIM = 64
FOURIER_WIDTH = N_FOURIER_GROUPS * FOURIER_GROUP_DIM
IN_WIDTH = FOURIER_WIDTH + ATTN_WIDTH + 2 * KV_WIDTH
ROPE_BASE = 10000.0
N_EXPERTS = 16
EC_CAPACITY_FACTOR = 2
D_EXPERT = 2048
N_MOD = 6
NORM_EPS = 1e-6

LANES = 128
BF16_SUBLANES = 16
GROUP_PAD = 2 * LANES
V_ROWS = HEAD_DIM + BF16_SUBLANES
VMEM_LIMIT = 56 * 1024 * 1024

F32 = jnp.float32
BF16 = jnp.bfloat16


def _params(sem):
    return pltpu.CompilerParams(dimension_semantics=sem, vmem_limit_bytes=VMEM_LIMIT)


def _mod_kernel(c_ref, w_ref, b_ref, o_ref):
    s = jax.nn.silu(c_ref[...])
    o_ref[...] = jnp.dot(s, w_ref[...], precision=lax.Precision.HIGHEST,
                         preferred_element_type=F32) + b_ref[...]


def _mod(cc, w_mod, b_mod):
    tn = 1024
    n = w_mod.shape[1]
    return pl.pallas_call(
        _mod_kernel,
        grid=(n // tn,),
        in_specs=[pl.BlockSpec((8, D_MODEL), lambda j: (0, 0)),
                  pl.BlockSpec((D_MODEL, tn), lambda j: (0, j)),
                  pl.BlockSpec((1, tn), lambda j: (0, j))],
        out_specs=pl.BlockSpec((8, tn), lambda j: (0, j)),
        out_shape=jax.ShapeDtypeStruct((8, n), F32),
        compiler_params=_params(("arbitrary",)),
        name="mod",
    )(cc, w_mod, b_mod)


def _rms_modulate(x, g, shift, scale):
    ms = jnp.mean(x * x, axis=-1, keepdims=True)
    y = x * lax.rsqrt(ms + NORM_EPS) * g
    return y * (1.0 + scale) + shift


def _head_norm(t, bmat, gain):
    tt = t * t
    hi = tt.astype(BF16)
    lo = (tt - hi.astype(F32)).astype(BF16)
    ms = (jnp.dot(hi, bmat, preferred_element_type=F32)
          + jnp.dot(lo, bmat, preferred_element_type=F32))
    return t * lax.rsqrt(ms + NORM_EPS) * gain


def _rope(t, cos, sin, lo_mask):
    outs = []
    for c in range(t.shape[1] // LANES):
        xc = t[:, c * LANES:(c + 1) * LANES]
        up = pltpu.roll(xc, LANES - 16, axis=1)
        dn = pltpu.roll(xc, 16, axis=1)
        rot = jnp.where(lo_mask, up, dn)
        outs.append(xc * cos + rot * sin)
    return jnp.concatenate(outs, axis=1)


def _inproj_kernel(x_ref, mod_ref, g_ref, w_ref, cos_ref, sin_ref, bq_ref, bk_ref,
                   gq_ref, gk_ref, w0_ref, *out_refs, mod_row, with_q):
    if with_q:
        zr_ref, zi_ref, q_ref, k_ref, v_ref = out_refs
    else:
        k_ref, v_ref = out_refs
    tm = x_ref.shape[0]
    shift = mod_ref[mod_row:mod_row + 1, 0:D_MODEL]
    scale = mod_ref[mod_row:mod_row + 1, D_MODEL:2 * D_MODEL]
    h = _rms_modulate(x_ref[...], g_ref[...], shift, scale).astype(BF16)
    cos = cos_ref[...]
    sin = sin_ref[...]
    lane = lax.broadcasted_iota(jnp.int32, (tm, LANES), 1)
    lo_mask = (lane % 32) < 16
    k0 = FOURIER_WIDTH + ATTN_WIDTH
    if with_q:
        p = jnp.dot(h, w_ref[...], preferred_element_type=F32)
        f = p[:, :FOURIER_WIDTH].astype(BF16)
        z = jnp.dot(f, w0_ref[...], preferred_element_type=F32)
        zr_ref[...] = z[:, :FOURIER_WIDTH].astype(BF16)
        zi_ref[...] = z[:, FOURIER_WIDTH:].astype(BF16)
        q = _head_norm(p[:, FOURIER_WIDTH:k0], bq_ref[...], gq_ref[...])
        q = _rope(q, cos, sin, lo_mask)
        q_ref[...] = q.T.astype(BF16).reshape(N_Q_HEADS, HEAD_DIM, tm)
        pk = p[:, k0:k0 + KV_WIDTH]
        pv = p[:, k0 + KV_WIDTH:]
    else:
        pkv = jnp.dot(h, w_ref[:, k0:], preferred_element_type=F32)
        pk = pkv[:, :KV_WIDTH]
        pv = pkv[:, KV_WIDTH:]
    k = _head_norm(pk, bk_ref[...], gk_ref[...])
    k = _rope(k, cos, sin, lo_mask)
    for g in range(N_KV_HEADS):
        k_ref[g] = k[:, g * HEAD_DIM:(g + 1) * HEAD_DIM].astype(BF16)
    v_ref[:, 0:HEAD_DIM, :] = pv.T.astype(BF16).reshape(N_KV_HEADS, HEAD_DIM, tm)
    v_ref[:, HEAD_DIM:V_ROWS, :] = jnp.ones((N_KV_HEADS, V_ROWS - HEAD_DIM, tm), BF16)


def _inproj(x, mod, g, w_in, cos, sin, bq, bk, gq, gk, w0, *, mod_row, with_q, tm):
    n = x.shape[0]
    const = lambda shape: pl.BlockSpec(shape, lambda i: (0,) * len(shape))
    in_specs = [pl.BlockSpec((tm, D_MODEL), lambda i: (i, 0)),
                const(mod.shape), const(g.shape), const(w_in.shape),
                pl.BlockSpec((tm, LANES), lambda i: (i, 0)),
                pl.BlockSpec((tm, LANES), lambda i: (i, 0)),
                const(bq.shape), const(bk.shape), const(gq.shape), const(gk.shape),
                const(w0.shape)]
    kv_specs = [pl.BlockSpec((N_KV_HEADS, tm, HEAD_DIM), lambda i: (0, i, 0)),
                pl.BlockSpec((N_KV_HEADS, V_ROWS, tm), lambda i: (0, 0, i))]
    kv_shapes = [jax.ShapeDtypeStruct((N_KV_HEADS, n, HEAD_DIM), BF16),
                 jax.ShapeDtypeStruct((N_KV_HEADS, V_ROWS, n), BF16)]
    if with_q:
        out_specs = [pl.BlockSpec((tm, FOURIER_WIDTH), lambda i: (i, 0)),
                     pl.BlockSpec((tm, FOURIER_WIDTH), lambda i: (i, 0)),
                     pl.BlockSpec((N_Q_HEADS, HEAD_DIM, tm), lambda i: (0, 0, i))] + kv_specs
        out_shapes = [jax.ShapeDtypeStruct((n, FOURIER_WIDTH), BF16),
                      jax.ShapeDtypeStruct((n, FOURIER_WIDTH), BF16),
                      jax.ShapeDtypeStruct((N_Q_HEADS, HEAD_DIM, n), BF16)] + kv_shapes
    else:
        out_specs, out_shapes = kv_specs, kv_shapes
    return pl.pallas_call(
        functools.partial(_inproj_kernel, mod_row=mod_row, with_q=with_q),
        grid=(n // tm,),
        in_specs=in_specs,
        out_specs=out_specs,
        out_shape=out_shapes,
        compiler_params=_params(("arbitrary",)),
        name="inproj_x" if with_q else "inproj_ctx",
    )(x, mod, g, w_in, cos, sin, bq, bk, gq, gk, w0)


def _attn_kernel(q_ref, k_ref, v_ref, kc_ref, vc_ref, o_ref, s_ref, acc_ref, *, tk):
    tq = q_ref.shape[2]
    n = k_ref.shape[1]
    lc = kc_ref.shape[1]
    nch = n // tk
    qs = jnp.concatenate([q_ref[j] for j in range(Q_PER_KV)], axis=1)

    def scores_latent(c, slot):
        start = pl.multiple_of(c * tk, tk)
        s_ref[slot] = jnp.dot(k_ref[0, pl.ds(start, tk), :], qs, preferred_element_type=F32)

    def v_latent(c):
        return v_ref[0, :, pl.ds(pl.multiple_of(c * tk, tk), tk)]

    def consume(slot, rows, vb, m):
        s = s_ref[slot, 0:rows, :]
        m_new = jnp.maximum(m, jnp.max(s, axis=0, keepdims=True))
        alpha = jnp.exp2(m - m_new)
        p = jnp.exp2(s - m_new).astype(BF16)
        acc_ref[...] = alpha * acc_ref[...] + jnp.dot(vb, p, preferred_element_type=F32)
        return m_new

    def pair(c0, m, prefetch):
        scores_latent(c0 + 1, 1)
        m = consume(0, tk, v_latent(c0), m)
        prefetch()
        return consume(1, tk, v_latent(c0 + 1), m)

    def body(i, m):
        return pair(2 * i, m, lambda: scores_latent(2 * i + 2, 0))

    def scores_context():
        s_ref[0, 0:lc, :] = jnp.dot(kc_ref[0], qs, preferred_element_type=F32)

    w = Q_PER_KV * tq
    acc_ref[...] = jnp.zeros_like(acc_ref)
    scores_latent(0, 0)
    m = lax.fori_loop(0, nch // 2 - 1, body, jnp.full((1, w), -1e30, F32))
    m = pair(nch - 2, m, scores_context)
    consume(0, lc, vc_ref[0], m)
    acc = acc_ref[...]
    o = acc[0:HEAD_DIM] / acc[HEAD_DIM:HEAD_DIM + 1]
    ot = jnp.concatenate([o, jnp.zeros_like(o)], axis=0).T
    h0, h1, h2 = (ot[j * tq:(j + 1) * tq] for j in range(Q_PER_KV))
    o_ref[0] = jnp.concatenate([h0 + pltpu.roll(h1, HEAD_DIM, axis=1), h2], axis=1).astype(BF16)


def _attention(q, k, v, kc, vc, *, tq, tk):
    n = k.shape[1]
    lc = kc.shape[1]
    gw = GROUP_PAD
    return pl.pallas_call(
        functools.partial(_attn_kernel, tk=tk),
        grid=(N_KV_HEADS, n // tq),
        in_specs=[pl.BlockSpec((Q_PER_KV, HEAD_DIM, tq), lambda g, i: (g, 0, i)),
                  pl.BlockSpec((1, n, HEAD_DIM), lambda g, i: (g, 0, 0)),
                  pl.BlockSpec((1, V_ROWS, n), lambda g, i: (g, 0, 0)),
                  pl.BlockSpec((1, lc, HEAD_DIM), lambda g, i: (g, 0, 0)),
                  pl.BlockSpec((1, V_ROWS, lc), lambda g, i: (g, 0, 0))],
        out_specs=pl.BlockSpec((1, tq, gw), lambda g, i: (g, i, 0)),
        out_shape=jax.ShapeDtypeStruct((N_KV_HEADS, n, gw), BF16),
        scratch_shapes=[pltpu.VMEM((2, tk, Q_PER_KV * tq), F32),
                        pltpu.VMEM((V_ROWS, Q_PER_KV * tq), F32)],
        compiler_params=_params(("arbitrary", "arbitrary")),
        name="attention",
    )(q, k, v, kc, vc)


def _dft_a_kernel(zr_ref, zi_ref, w_ref, ar_ref, ai_ref):
    na = zr_ref.shape[0]
    zz = jnp.concatenate([zr_ref[...], zi_ref[...]], axis=0)
    a = jnp.dot(w_ref[...], zz, preferred_element_type=F32)
    ar_ref[...] = a[:na].astype(BF16)
    ai_ref[...] = a[na:].astype(BF16)


def _dft_a(zr2, zi2, wa, *, cb):
    na, cols = zr2.shape
    blk = pl.BlockSpec((na, cb), lambda j: (0, j))
    return pl.pallas_call(
        _dft_a_kernel,
        grid=(cols // cb,),
        in_specs=[blk, blk, pl.BlockSpec(wa.shape, lambda j: (0, 0))],
        out_specs=[blk, blk],
        out_shape=[jax.ShapeDtypeStruct((na, cols), BF16)] * 2,
        compiler_params=_params(("arbitrary",)),
        name="dft_a",
    )(zr2, zi2, wa)


def _dft_c_kernel(ar_ref, ai_ref, m_ref, wf_ref, o_ref):
    kb = ar_ref.shape[0]
    outs = []
    for j in range(kb):
        ab = jnp.concatenate([ar_ref[j], ai_ref[j]], axis=0)
        y = jnp.dot(m_ref[j], ab, preferred_element_type=F32)
        outs.append(jnp.dot(y.astype(BF16), wf_ref[...], preferred_element_type=F32))
    o_ref[...] = jnp.concatenate(outs, axis=1).astype(BF16)


def _dft_c(ar3, ai3, mm, wf, *, kb):
    na, nb, fw = ar3.shape
    blk = pl.BlockSpec((kb, nb, fw), lambda i: (i, 0, 0))
    return pl.pallas_call(
        _dft_c_kernel,
        grid=(na // kb,),
        in_specs=[blk, blk,
                  pl.BlockSpec((kb, nb, 2 * nb), lambda i: (i, 0, 0)),
                  pl.BlockSpec(wf.shape, lambda i: (0, 0))],
        out_specs=pl.BlockSpec((nb, kb * fw), lambda i: (0, i)),
        out_shape=jax.ShapeDtypeStruct((nb, na * fw), BF16),
        compiler_params=_params(("arbitrary",)),
        name="dft_c",
    )(ar3, ai3, mm, wf)


def _outproj_kernel(four_ref, attn_ref, x_ref, mod_ref, g_ref, wf_ref, wa_ref, wr_ref,
                    x1_ref, h_ref, aff_ref):
    mix = jnp.dot(four_ref[...], wf_ref[...], preferred_element_type=F32)
    for g in range(N_KV_HEADS):
        mix = mix + jnp.dot(attn_ref[g], wa_ref[g], preferred_element_type=F32)
    gate1 = mod_ref[0:1, 2 * D_MODEL:3 * D_MODEL]
    shift2 = mod_ref[0:1, 3 * D_MODEL:4 * D_MODEL]
    scale2 = mod_ref[0:1, 4 * D_MODEL:5 * D_MODEL]
    x1 = x_ref[...] + gate1 * mix
    x1_ref[...] = x1
    h = _rms_modulate(x1, g_ref[...], shift2, scale2).astype(BF16)
    h_ref[...] = h
    logits = lax.dot_general(wr_ref[...], h, (((1,), (1,)), ((), ())),
                             preferred_element_type=F32)
    mx = jnp.max(logits, axis=0, keepdims=True)
    e = jnp.exp(logits - mx)
    aff_ref[...] = e / jnp.sum(e, axis=0, keepdims=True)


def _outproj(four, attn, x, mod, g, wo_f, wo_a, wr_t, *, tm):
    n = x.shape[0]
    gw = GROUP_PAD
    const = lambda shape: pl.BlockSpec(shape, lambda i: (0,) * len(shape))
    return pl.pallas_call(
        _outproj_kernel,
        grid=(n // tm,),
        in_specs=[pl.BlockSpec((tm, FOURIER_WIDTH), lambda i: (i, 0)),
                  pl.BlockSpec((N_KV_HEADS, tm, gw), lambda i: (0, i, 0)),
                  pl.BlockSpec((tm, D_MODEL), lambda i: (i, 0)),
                  const(mod.shape), const(g.shape), const(wo_f.shape), const(wo_a.shape),
                  const(wr_t.shape)],
        out_specs=[pl.BlockSpec((tm, D_MODEL), lambda i: (i, 0)),
                   pl.BlockSpec((tm, D_MODEL), lambda i: (i, 0)),
                   pl.BlockSpec((N_EXPERTS, tm), lambda i: (0, i))],
        out_shape=[jax.ShapeDtypeStruct((n, D_MODEL), F32),
                   jax.ShapeDtypeStruct((n, D_MODEL), BF16),
                   jax.ShapeDtypeStruct((N_EXPERTS, n), F32)],
        compiler_params=_params(("arbitrary",)),
        name="outproj",
    )(four, attn, x, mod, g, wo_f, wo_a, wr_t)


def _count(m):
    return jnp.sum(jnp.sum(m, axis=2, keepdims=True), axis=1, keepdims=True)


def _topk_kernel(aff_ref, ut_ref, ones_ref, slt_ref, slot_ref, off_ref, *, cap):
    e_n, r_n, _ = aff_ref.shape
    aff = aff_ref[...]
    capf = jnp.float32(cap)

    def count_ge(th):
        return _count(jnp.where(aff >= th, 1.0, 0.0))

    def search(i, t):
        cand = t | jnp.left_shift(jnp.int32(1), 30 - i)
        c = count_ge(lax.bitcast_convert_type(cand, F32))
        return jnp.where(c >= capf, cand, t)

    tbits = lax.fori_loop(0, 31, search, jnp.zeros((e_n, 1, 1), jnp.int32))
    thr = lax.bitcast_convert_type(tbits, F32)
    ulp = lax.bitcast_convert_type(tbits + 1, F32) - thr

    def refine(_, carry):
        t, step = carry
        step = step * 0.5
        cand = t + step
        return jnp.where(count_ge(cand) >= capf, cand, t), step

    thr, _ = lax.fori_loop(0, 29, refine, (thr, ulp))
    gt = jnp.where(aff > thr, 1.0, 0.0)
    eq = jnp.where(aff == thr, 1.0, 0.0)
    need = capf - _count(gt)

    def excl_prefix(m):
        m2 = m.reshape(e_n * r_n, LANES).astype(BF16)
        within = jnp.dot(m2, ut_ref[...], preferred_element_type=F32).reshape(e_n, r_n, LANES)
        tot = jnp.dot(m2, ones_ref[...], preferred_element_type=F32).reshape(e_n, r_n, LANES)
        rows = jnp.stack([jnp.dot(slt_ref[...], tot[e].astype(BF16), preferred_element_type=F32)
                          for e in range(e_n)], axis=0)
        return within + rows, rows

    pe, _ = excl_prefix(eq)
    sel = jnp.maximum(gt, eq * jnp.where(pe < need, 1.0, 0.0))
    pos, rows = excl_prefix(sel)
    slot_ref[...] = jnp.where(sel > 0.5, pos, -1.0).astype(jnp.int32)
    off_ref[...] = rows.astype(jnp.int32)


def _topk(aff3, ut, ones, slt, *, cap):
    shp = aff3.shape
    full = lambda a: pl.BlockSpec(a.shape, lambda i: (0,) * a.ndim)
    return pl.pallas_call(
        functools.partial(_topk_kernel, cap=cap),
        grid=(1,),
        in_specs=[full(aff3), full(ut), full(ones), full(slt)],
        out_specs=[pl.BlockSpec(shp, lambda i: (0, 0, 0))] * 2,
        out_shape=[jax.ShapeDtypeStruct(shp, jnp.int32)] * 2,
        compiler_params=_params(("arbitrary",)),
        name="topk",
    )(aff3, ut, ones, slt)


def _window_start(off):
    return pl.multiple_of((off // BF16_SUBLANES) * BF16_SUBLANES, BF16_SUBLANES)


def _gather_kernel(off_ref, slot_ref, h_ref, xe_ref, acc_ref, *, win):
    e = pl.program_id(0)
    s = pl.program_id(1)
    tb = h_ref.shape[0]
    cap = xe_ref.shape[1]
    nsub = tb // LANES

    @pl.when(s == 0)
    def _():
        acc_ref[...] = jnp.zeros_like(acc_ref)

    rows = lax.broadcasted_iota(jnp.int32, (win, LANES), 0)
    for j in range(nsub):
        ws = _window_start(off_ref[e, s * nsub + j])
        rel = slot_ref[0, :, j * LANES:(j + 1) * LANES] - ws
        onehot = jnp.where(rows == rel, 1.0, 0.0).astype(BF16)
        acc_ref[pl.ds(ws, win), :] += jnp.dot(onehot, h_ref[j * LANES:(j + 1) * LANES, :],
                                              preferred_element_type=F32)

    @pl.when(s == pl.num_programs(1) - 1)
    def _():
        xe_ref[0] = acc_ref[0:cap, :].astype(BF16)


def _gather(off, slot3, h, *, cap, tb):
    n = h.shape[0]
    win = LANES + BF16_SUBLANES
    grid_spec = pltpu.PrefetchScalarGridSpec(
        num_scalar_prefetch=1,
        grid=(N_EXPERTS, n // tb),
        in_specs=[pl.BlockSpec((1, 1, tb), lambda e, s, off: (e, 0, s)),
                  pl.BlockSpec((tb, D_MODEL), lambda e, s, off: (s, 0))],
        out_specs=pl.BlockSpec((1, cap, D_MODEL), lambda e, s, off: (e, 0, 0)),
        scratch_shapes=[pltpu.VMEM((cap + win, D_MODEL), F32)])
    return pl.pallas_call(
        functools.partial(_gather_kernel, win=win),
        grid_spec=grid_spec,
        out_shape=jax.ShapeDtypeStruct((N_EXPERTS, cap, D_MODEL), BF16),
        compiler_params=_params(("arbitrary", "arbitrary")),
        name="gather",
    )(off, slot3, h)


def _ffn_kernel(xe_ref, wg_ref, wu_ref, wd_ref, ye_ref, acc_ref, *, mc):
    f = pl.program_id(1)
    cap = xe_ref.shape[1]
    wg = wg_ref[0].astype(BF16)
    wu = wu_ref[0].astype(BF16)
    wd = wd_ref[0].astype(BF16)
    for m in range(cap // mc):
        xm = xe_ref[0, m * mc:(m + 1) * mc, :]
        gate = jnp.dot(xm, wg, preferred_element_type=F32)
        up = jnp.dot(xm, wu, preferred_element_type=F32)
        hid = (jax.nn.silu(gate) * up).astype(BF16)
        y = jnp.dot(hid, wd, preferred_element_type=F32)

        @pl.when(f == 0)
        def _():
            acc_ref[m * mc:(m + 1) * mc, :] = y

        @pl.when(f != 0)
        def _():
            acc_ref[m * mc:(m + 1) * mc, :] += y

    @pl.when(f == pl.num_programs(1) - 1)
    def _():
        ye_ref[0] = acc_ref[...].astype(BF16)


def _ffn(xe, w_gate, w_up, w_down, *, tf, mc):
    e_n, cap, _ = xe.shape
    return pl.pallas_call(
        functools.partial(_ffn_kernel, mc=mc),
        grid=(e_n, D_EXPERT // tf),
        in_specs=[pl.BlockSpec((1, cap, D_MODEL), lambda e, f: (e, 0, 0)),
                  pl.BlockSpec((1, D_MODEL, tf), lambda e, f: (e, 0, f)),
                  pl.BlockSpec((1, D_MODEL, tf), lambda e, f: (e, 0, f)),
                  pl.BlockSpec((1, tf, D_MODEL), lambda e, f: (e, f, 0))],
        out_specs=pl.BlockSpec((1, cap, D_MODEL), lambda e, f: (e, 0, 0)),
        out_shape=jax.ShapeDtypeStruct((e_n, cap, D_MODEL), BF16),
        scratch_shapes=[pltpu.VMEM((cap, D_MODEL), F32)],
        compiler_params=_params(("arbitrary", "arbitrary")),
        name="ffn",
    )(xe, w_gate, w_up, w_down)


def _combine_kernel(off_ref, x1_ref, slot_ref, aff_ref, mod_ref, g_ref, ye_ref, o_ref,
                    buf_ref, sem_ref, *, win):
    t = pl.program_id(0)
    tt = x1_ref.shape[0]
    cap = ye_ref.shape[1]
    nsub = tt // LANES
    big = tt + BF16_SUBLANES

    def big_start(e):
        return pl.multiple_of(jnp.minimum(_window_start(off_ref[e, t * nsub]), cap - big),
                              BF16_SUBLANES)

    def copy(e, slot):
        return pltpu.make_async_copy(ye_ref.at[e, pl.ds(big_start(e), big), :],
                                     buf_ref.at[slot], sem_ref.at[slot])

    copy(0, 0).start()
    cols = lax.broadcasted_iota(jnp.int32, (LANES, win), 1)
    accs = [jnp.zeros((LANES, D_MODEL), F32) for _ in range(nsub)]
    for e in range(N_EXPERTS):
        slot = e % 2
        if e + 1 < N_EXPERTS:
            copy(e + 1, 1 - slot).start()
        copy(e, slot).wait()
        wb = big_start(e)
        for j in range(nsub):
            ws = jnp.minimum(_window_start(off_ref[e, t * nsub + j]), wb + (big - win))
            rel = slot_ref[j * LANES:(j + 1) * LANES, e:e + 1] - ws
            onehot = jnp.where(cols == rel, 1.0, 0.0).astype(BF16)
            r0 = pl.multiple_of(ws - wb, BF16_SUBLANES)
            y = jnp.dot(onehot, buf_ref[slot, pl.ds(r0, win), :], preferred_element_type=F32)
            accs[j] = accs[j] + aff_ref[j * LANES:(j + 1) * LANES, e:e + 1] * y
    gate2 = mod_ref[0:1, 5 * D_MODEL:6 * D_MODEL]
    x2 = x1_ref[...] + gate2 * jnp.concatenate(accs, axis=0)
    ms = jnp.mean(x2 * x2, axis=-1, keepdims=True)
    o_ref[...] = x2 * lax.rsqrt(ms + NORM_EPS) * g_ref[...]


def _combine(off, x1, slot_t, aff_t, mod, g, ye, *, tt):
    n = x1.shape[0]
    win = min(2 * LANES, tt + BF16_SUBLANES)
    grid_spec = pltpu.PrefetchScalarGridSpec(
        num_scalar_prefetch=1,
        grid=(n // tt,),
        in_specs=[pl.BlockSpec((tt, D_MODEL), lambda t, off: (t, 0)),
                  pl.BlockSpec((tt, N_EXPERTS), lambda t, off: (t, 0)),
                  pl.BlockSpec((tt, N_EXPERTS), lambda t, off: (t, 0)),
                  pl.BlockSpec(mod.shape, lambda t, off: (0, 0)),
                  pl.BlockSpec(g.shape, lambda t, off: (0, 0)),
                  pl.BlockSpec(memory_space=pl.ANY)],
        out_specs=pl.BlockSpec((tt, D_MODEL), lambda t, off: (t, 0)),
        scratch_shapes=[pltpu.VMEM((2, tt + BF16_SUBLANES, D_MODEL), BF16),
                        pltpu.SemaphoreType.DMA((2,))])
    return pl.pallas_call(
        functools.partial(_combine_kernel, win=win),
        grid_spec=grid_spec,
        out_shape=jax.ShapeDtypeStruct((n, D_MODEL), F32),
        compiler_params=_params(("arbitrary",)),
        name="combine",
    )(off, x1, slot_t, aff_t, mod, g, ye)


def _rope_tables(n):
    rows = n // GRID_W
    row = jnp.repeat(jnp.arange(rows, dtype=jnp.int32), GRID_W)
    col = jnp.tile(jnp.arange(GRID_W, dtype=jnp.int32), rows)
    half = HEAD_DIM // 4
    freqs = ROPE_BASE ** (-jnp.arange(half, dtype=F32) / half)
    ang_r = row.astype(F32)[:, None] * freqs[None, :]
    ang_c = col.astype(F32)[:, None] * freqs[None, :]
    cos_h = jnp.concatenate([jnp.cos(ang_r)] * 2 + [jnp.cos(ang_c)] * 2, axis=1)
    sin_h = jnp.concatenate([-jnp.sin(ang_r), jnp.sin(ang_r),
                             -jnp.sin(ang_c), jnp.sin(ang_c)], axis=1)
    return jnp.concatenate([cos_h] * 2, axis=1), jnp.concatenate([sin_h] * 2, axis=1)


def _block_diag(blocks):
    k = len(blocks)
    r, c = blocks[0].shape
    out = jnp.zeros((k * r, k * c), blocks[0].dtype)
    for i, b in enumerate(blocks):
        out = out.at[i * r:(i + 1) * r, i * c:(i + 1) * c].set(b)
    return out


def _dft_tables(n):
    nb = LANES
    na = n // nb
    gd = FOURIER_GROUP_DIM
    scale = 1.0 / math.sqrt(n * gd)

    def cos_sin(i, j, period):
        ang = ((i * j) % period).astype(F32) * (2.0 * math.pi / period)
        return jnp.cos(ang), jnp.sin(ang)

    ar_g = jnp.arange(gd, dtype=jnp.int32)
    c64, s64 = cos_sin(ar_g[:, None], ar_g[None, :], gd)
    w0 = jnp.concatenate([_block_diag([c64 * scale] * N_FOURIER_GROUPS),
                          _block_diag([-s64 * scale] * N_FOURIER_GROUPS)], axis=1)
    ar_a = jnp.arange(na, dtype=jnp.int32)
    ca, sa = cos_sin(ar_a[:, None], ar_a[None, :], na)
    wa = jnp.concatenate([jnp.concatenate([ca, sa], axis=1),
                          jnp.concatenate([-sa, ca], axis=1)], axis=0)
    ar_b = jnp.arange(nb, dtype=jnp.int32)
    ct, st = cos_sin(ar_a[:, None], ar_b[None, :], n)
    cb, sb = cos_sin(ar_b[:, None], ar_b[None, :], nb)
    mc = ct[:, None, :] * cb[None, :, :] - st[:, None, :] * sb[None, :, :]
    ms = st[:, None, :] * cb[None, :, :] + ct[:, None, :] * sb[None, :, :]
    mm = jnp.concatenate([mc, ms], axis=2)
    return w0.astype(BF16), wa.astype(BF16), mm.astype(BF16)


def _prefix_tables(r_n):
    i = np.arange(LANES)
    ut = (i[:, None] < i[None, :]).astype(np.float32)
    ones = np.ones((LANES, LANES), np.float32)
    r = np.arange(r_n)
    slt = (r[None, :] < r[:, None]).astype(np.float32)
    as_bf16 = lambda a: jnp.asarray(a).astype(BF16)
    return as_bf16(ut), as_bf16(ones), as_bf16(slt)


def kernel(x, c, ctx, c_ctx, w_mod, b_mod, norm_mix_g, norm_ffn_g, w_in, w_four, q_norm_g, k_norm_g,
           w_out, w_router, w_gate, w_up, w_down, final_norm_g):
    batch, n, _ = x.shape
    assert batch == 1 and w_mod.shape[0] == 1
    lc = ctx.shape[1]
    cap = EC_CAPACITY_FACTOR * n // N_EXPERTS
    x2d = x[0]
    ctx2d = ctx[0]

    cc = jnp.zeros((8, D_MODEL), F32).at[0].set(c[0]).at[1].set(c_ctx)
    mod = _mod(cc, w_mod[0], b_mod[0][None, :])

    cos, sin = _rope_tables(n)
    cos_c = jnp.ones((lc, LANES), F32)
    sin_c = jnp.zeros((lc, LANES), F32)
    inv_hd = jnp.full((HEAD_DIM, HEAD_DIM), 1.0 / HEAD_DIM, BF16)
    bq = _block_diag([inv_hd] * N_Q_HEADS)
    bk = _block_diag([inv_hd] * N_KV_HEADS)
    q_scale = (HEAD_DIM ** -0.5) * math.log2(math.e)
    gq = jnp.tile(q_norm_g[0], N_Q_HEADS)[None, :] * q_scale
    gk = jnp.tile(k_norm_g[0], N_KV_HEADS)[None, :]
    w0, wa, mm = _dft_tables(n)
    w_in_b = w_in[0].astype(BF16)
    g_mix = norm_mix_g[0][None, :]
    tm = min(512, n)
    zr, zi, q_t, k_l, v_t = _inproj(x2d, mod, g_mix, w_in_b, cos, sin, bq, bk, gq, gk, w0,
                                    mod_row=0, with_q=True, tm=tm)
    k_c, v_c = _inproj(ctx2d, mod, g_mix, w_in_b, cos_c, sin_c, bq, bk, gq, gk, w0,
                       mod_row=1, with_q=False, tm=lc)

    attn = _attention(q_t, k_l, v_t, k_c, v_c, tq=min(256, n), tk=min(512, n))

    nb = LANES
    na = n // nb
    ar, ai = _dft_a(zr.reshape(na, nb * FOURIER_WIDTH), zi.reshape(na, nb * FOURIER_WIDTH), wa,
                    cb=min(4096, nb * FOURIER_WIDTH))
    wf = _block_diag([w_four[0, g] for g in range(N_FOURIER_GROUPS)]).astype(BF16)
    four = _dft_c(ar.reshape(na, nb, FOURIER_WIDTH), ai.reshape(na, nb, FOURIER_WIDTH), mm, wf,
                  kb=8).reshape(n, FOURIER_WIDTH)

    w_out_b = w_out[0].astype(BF16)
    wo_f = w_out_b[:FOURIER_WIDTH]
    wo_a = w_out_b[FOURIER_WIDTH:].reshape(N_KV_HEADS, Q_PER_KV * HEAD_DIM, D_MODEL)
    wo_a = jnp.pad(wo_a, ((0, 0), (0, GROUP_PAD - Q_PER_KV * HEAD_DIM), (0, 0)))
    wr_t = w_router[0].T.astype(BF16)
    x1, h2, aff = _outproj(four, attn, x2d, mod, norm_ffn_g[0][None, :], wo_f, wo_a, wr_t, tm=tm)

    r_n = n // LANES
    ut, ones, slt = _prefix_tables(r_n)
    slot3, off3 = _topk(aff.reshape(N_EXPERTS, r_n, LANES), ut, ones, slt, cap=cap)
    off = off3[:, :, 0]
    slot = slot3.reshape(N_EXPERTS, n)

    xe = _gather(off, slot.reshape(N_EXPERTS, 1, n), h2, cap=cap, tb=min(1024, n))
    ye = _ffn(xe, w_gate[0], w_up[0], w_down[0], tf=512, mc=min(512, cap))
    tt = min(512, cap - BF16_SUBLANES)
    tt = (tt // LANES) * LANES
    out = _combine(off, x1, slot.T, aff.T, mod, final_norm_g[None, :], ye, tt=tt)
    return out[None]
```

```python
import functools
import math

import numpy as np
import jax
import jax.numpy as jnp
from jax import lax
from jax.experimental import pallas as pl
from jax.experimental.pallas import tpu as pltpu

D_MODEL = 1024
GRID_W = 64
HEAD_DIM = 64
N_Q_HEADS = 12
N_KV_HEADS = 4
Q_PER_KV = N_Q_HEADS // N_KV_HEADS
ATTN_WIDTH = N_Q_HEADS * HEAD_DIM
KV_WIDTH = N_KV_HEADS * HEAD_DIM
N_FOURIER_GROUPS = 4
FOURIER_GROUP_DIM = 64
FOURIER_WIDTH = N_FOURIER_GROUPS * FOURIER_GROUP_DIM
IN_WIDTH = FOURIER_WIDTH + ATTN_WIDTH + 2 * KV_WIDTH
ROPE_BASE = 10000.0
N_EXPERTS = 16
EC_CAPACITY_FACTOR = 2
D_EXPERT = 2048
N_MOD = 6
NORM_EPS = 1e-6

LANES = 128
BF16_SUBLANES = 16
GROUP_PAD = 2 * LANES
V_ROWS = HEAD_DIM + BF16_SUBLANES
GATHER_WIN = LANES + BF16_SUBLANES
GATHER_FAST_MAX = LANES
COMBINE_FAST_MAX = LANES - BF16_SUBLANES
GATHER_EXPERTS = 4
VMEM_LIMIT = 56 * 1024 * 1024

F32 = jnp.float32
BF16 = jnp.bfloat16


def _params(sem):
    return pltpu.CompilerParams(dimension_semantics=sem, vmem_limit_bytes=VMEM_LIMIT)


def _mod_kernel(c_ref, w_ref, b_ref, o_ref):
    s = jax.nn.silu(c_ref[...])
    o_ref[...] = jnp.dot(s, w_ref[...], precision=lax.Precision.HIGHEST,
                         preferred_element_type=F32) + b_ref[...]


def _mod(cc, w_mod, b_mod):
    tn = 1024
    n = w_mod.shape[1]
    return pl.pallas_call(
        _mod_kernel,
        grid=(n // tn,),
        in_specs=[pl.BlockSpec((8, D_MODEL), lambda j: (0, 0)),
                  pl.BlockSpec((D_MODEL, tn), lambda j: (0, j)),
                  pl.BlockSpec((1, tn), lambda j: (0, j))],
        out_specs=pl.BlockSpec((8, tn), lambda j: (0, j)),
        out_shape=jax.ShapeDtypeStruct((8, n), F32),
        compiler_params=_params(("arbitrary",)),
        name="mod",
    )(cc, w_mod, b_mod)


def _rms_modulate(x, g, shift, scale):
    ms = jnp.mean(x * x, axis=-1, keepdims=True)
    y = x * lax.rsqrt(ms + NORM_EPS) * g
    return y * (1.0 + scale) + shift


def _head_norm(t, bmat, gain):
    tt = t * t
    hi = tt.astype(BF16)
    lo = (tt - hi.astype(F32)).astype(BF16)
    ms = (jnp.dot(hi, bmat, preferred_element_type=F32)
          + jnp.dot(lo, bmat, preferred_element_type=F32))
    return t * lax.rsqrt(ms + NORM_EPS) * gain


def _rope(t, cos, sin, lo_mask):
    outs = []
    for c in range(t.shape[1] // LANES):
        xc = t[:, c * LANES:(c + 1) * LANES]
        up = pltpu.roll(xc, LANES - 16, axis=1)
        dn = pltpu.roll(xc, 16, axis=1)
        rot = jnp.where(lo_mask, up, dn)
        outs.append(xc * cos + rot * sin)
    return jnp.concatenate(outs, axis=1)


def _inproj_kernel(x_ref, mod_ref, g_ref, w_ref, rcos_ref, rsin_ref, ccos_ref, csin_ref, bq_ref, bk_ref,
                   gq_ref, gk_ref, w0_ref, *out_refs, mod_row, with_q):
    if with_q:
        zr_ref, zi_ref, q_ref, k_ref, v_ref = out_refs
    else:
        k_ref, v_ref = out_refs
    tm = x_ref.shape[0]
    shift = mod_ref[mod_row:mod_row + 1, 0:D_MODEL]
    scale = mod_ref[mod_row:mod_row + 1, D_MODEL:2 * D_MODEL]
    h = _rms_modulate(x_ref[...], g_ref[...], shift, scale).astype(BF16)
    cos = jnp.concatenate([rcos_ref[r:r + 1, :] + ccos_ref[...] for r in range(tm // GRID_W)], axis=0)
    sin = jnp.concatenate([rsin_ref[r:r + 1, :] + csin_ref[...] for r in range(tm // GRID_W)], axis=0)
    lane = lax.broadcasted_iota(jnp.int32, (tm, LANES), 1)
    lo_mask = (lane % 32) < 16
    k0 = FOURIER_WIDTH + ATTN_WIDTH
    if with_q:
        p = jnp.dot(h, w_ref[...], preferred_element_type=F32)
        f = p[:, :FOURIER_WIDTH].astype(BF16)
        z = jnp.dot(f, w0_ref[...], preferred_element_type=F32)
        zr_ref[...] = z[:, :FOURIER_WIDTH].astype(BF16)
        zi_ref[...] = z[:, FOURIER_WIDTH:].astype(BF16)
        q = _head_norm(p[:, FOURIER_WIDTH:k0], bq_ref[...], gq_ref[...])
        q = _rope(q, cos, sin, lo_mask)
        q_ref[...] = q.T.astype(BF16).reshape(N_Q_HEADS, HEAD_DIM, tm)
        pk = p[:, k0:k0 + KV_WIDTH]
        pv = p[:, k0 + KV_WIDTH:]
    else:
        pkv = jnp.dot(h, w_ref[:, k0:], preferred_element_type=F32)
        pk = pkv[:, :KV_WIDTH]
        pv = pkv[:, KV_WIDTH:]
    k = _head_norm(pk, bk_ref[...], gk_ref[...])
    k = _rope(k, cos, sin, lo_mask)
    for g in range(N_KV_HEADS):
        k_ref[g] = k[:, g * HEAD_DIM:(g + 1) * HEAD_DIM].astype(BF16)
    v_ref[:, 0:HEAD_DIM, :] = pv.T.astype(BF16).reshape(N_KV_HEADS, HEAD_DIM, tm)
    v_ref[:, HEAD_DIM:V_ROWS, :] = jnp.ones((N_KV_HEADS, V_ROWS - HEAD_DIM, tm), BF16)


def _inproj(x, mod, g, w_in, rope, bq, bk, gq, gk, w0, *, mod_row, with_q, tm):
    n = x.shape[0]
    rcos, rsin, ccos, csin = rope
    const = lambda shape: pl.BlockSpec(shape, lambda i: (0,) * len(shape))
    in_specs = [pl.BlockSpec((tm, D_MODEL), lambda i: (i, 0)),
                const(mod.shape), const(g.shape), const(w_in.shape),
                pl.BlockSpec((tm // GRID_W, LANES), lambda i: (i, 0)),
                pl.BlockSpec((tm // GRID_W, LANES), lambda i: (i, 0)),
                const(ccos.shape), const(csin.shape),
                const(bq.shape), const(bk.shape), const(gq.shape), const(gk.shape),
                const(w0.shape)]
    kv_specs = [pl.BlockSpec((N_KV_HEADS, tm, HEAD_DIM), lambda i: (0, i, 0)),
                pl.BlockSpec((N_KV_HEADS, V_ROWS, tm), lambda i: (0, 0, i))]
    kv_shapes = [jax.ShapeDtypeStruct((N_KV_HEADS, n, HEAD_DIM), BF16),
                 jax.ShapeDtypeStruct((N_KV_HEADS, V_ROWS, n), BF16)]
    if with_q:
        out_specs = [pl.BlockSpec((tm, FOURIER_WIDTH), lambda i: (i, 0)),
                     pl.BlockSpec((tm, FOURIER_WIDTH), lambda i: (i, 0)),
                     pl.BlockSpec((N_Q_HEADS, HEAD_DIM, tm), lambda i: (0, 0, i))] + kv_specs
        out_shapes = [jax.ShapeDtypeStruct((n, FOURIER_WIDTH), BF16),
                      jax.ShapeDtypeStruct((n, FOURIER_WIDTH), BF16),
                      jax.ShapeDtypeStruct((N_Q_HEADS, HEAD_DIM, n), BF16)] + kv_shapes
    else:
        out_specs, out_shapes = kv_specs, kv_shapes
    return pl.pallas_call(
        functools.partial(_inproj_kernel, mod_row=mod_row, with_q=with_q),
        grid=(n // tm,),
        in_specs=in_specs,
        out_specs=out_specs,
        out_shape=out_shapes,
        compiler_params=_params(("arbitrary",)),
        name="inproj_x" if with_q else "inproj_ctx",
    )(x, mod, g, w_in, rcos, rsin, ccos, csin, bq, bk, gq, gk, w0)


def _attn_kernel(q_ref, k_ref, v_ref, kc_ref, vc_ref, o_ref, s_ref, acc_ref, *, tk):
    tq = q_ref.shape[2]
    n = k_ref.shape[1]
    lc = kc_ref.shape[1]
    nch = n // tk
    qs = jnp.concatenate([q_ref[j] for j in range(Q_PER_KV)], axis=1)

    def scores_latent(c, slot):
        start = pl.multiple_of(c * tk, tk)
        s_ref[slot] = jnp.dot(k_ref[0, pl.ds(start, tk), :], qs, preferred_element_type=F32)

    def v_latent(c):
        return v_ref[0, :, pl.ds(pl.multiple_of(c * tk, tk), tk)]

    def consume(slot, rows, vb, m):
        s = s_ref[slot, 0:rows, :]
        m_new = jnp.maximum(m, jnp.max(s, axis=0, keepdims=True))
        alpha = jnp.exp2(m - m_new)
        p = jnp.exp2(s - m_new).astype(BF16)
        acc_ref[...] = alpha * acc_ref[...] + jnp.dot(vb, p, preferred_element_type=F32)
        return m_new

    def pair(c0, m, prefetch):
        scores_latent(c0 + 1, 1)
        m = consume(0, tk, v_latent(c0), m)
        prefetch()
        return consume(1, tk, v_latent(c0 + 1), m)

    def body(i, m):
        return pair(2 * i, m, lambda: scores_latent(2 * i + 2, 0))

    def scores_context():
        s_ref[0, 0:lc, :] = jnp.dot(kc_ref[0], qs, preferred_element_type=F32)

    w = Q_PER_KV * tq
    acc_ref[...] = jnp.zeros_like(acc_ref)
    scores_latent(0, 0)
    m = lax.fori_loop(0, nch // 2 - 1, body, jnp.full((1, w), -1e30, F32))
    m = pair(nch - 2, m, scores_context)
    consume(0, lc, vc_ref[0], m)
    acc = acc_ref[...]
    o = acc[0:HEAD_DIM] / acc[HEAD_DIM:HEAD_DIM + 1]
    ot = jnp.concatenate([o, jnp.zeros_like(o)], axis=0).T
    h0, h1, h2 = (ot[j * tq:(j + 1) * tq] for j in range(Q_PER_KV))
    o_ref[0] = jnp.concatenate([h0 + pltpu.roll(h1, HEAD_DIM, axis=1), h2], axis=1).astype(BF16)


def _attention(q, k, v, kc, vc, *, tq, tk):
    n = k.shape[1]
    lc = kc.shape[1]
    gw = GROUP_PAD
    return pl.pallas_call(
        functools.partial(_attn_kernel, tk=tk),
        grid=(N_KV_HEADS, n // tq),
        in_specs=[pl.BlockSpec((Q_PER_KV, HEAD_DIM, tq), lambda g, i: (g, 0, i)),
                  pl.BlockSpec((1, n, HEAD_DIM), lambda g, i: (g, 0, 0)),
                  pl.BlockSpec((1, V_ROWS, n), lambda g, i: (g, 0, 0)),
                  pl.BlockSpec((1, lc, HEAD_DIM), lambda g, i: (g, 0, 0)),
                  pl.BlockSpec((1, V_ROWS, lc), lambda g, i: (g, 0, 0))],
        out_specs=pl.BlockSpec((1, tq, gw), lambda g, i: (g, i, 0)),
        out_shape=jax.ShapeDtypeStruct((N_KV_HEADS, n, gw), BF16),
        scratch_shapes=[pltpu.VMEM((2, tk, Q_PER_KV * tq), F32),
                        pltpu.VMEM((V_ROWS, Q_PER_KV * tq), F32)],
        compiler_params=_params(("arbitrary", "arbitrary")),
        name="attention",
    )(q, k, v, kc, vc)


def _dft_a_kernel(zr_ref, zi_ref, w_ref, tc_ref, ts_ref, br_ref, bi_ref):
    na = zr_ref.shape[0]
    fw = FOURIER_WIDTH
    zz = jnp.concatenate([zr_ref[...], zi_ref[...]], axis=0)
    a = jnp.dot(w_ref[...], zz, preferred_element_type=F32)
    tc = tc_ref[0]
    ts = ts_ref[0]
    for j in range(tc.shape[1]):
        ar = a[:na, j * fw:(j + 1) * fw]
        ai = a[na:, j * fw:(j + 1) * fw]
        c = tc[:, j:j + 1]
        s = ts[:, j:j + 1]
        br_ref[:, j * fw:(j + 1) * fw] = (ar * c + ai * s).astype(BF16)
        bi_ref[:, j * fw:(j + 1) * fw] = (ai * c - ar * s).astype(BF16)


def _dft_a(zr2, zi2, wa, tc3, ts3, *, cb):
    na, cols = zr2.shape
    blk = pl.BlockSpec((na, cb), lambda j: (0, j))
    tw = pl.BlockSpec((1,) + tc3.shape[1:], lambda j: (j, 0, 0))
    return pl.pallas_call(
        _dft_a_kernel,
        grid=(cols // cb,),
        in_specs=[blk, blk, pl.BlockSpec(wa.shape, lambda j: (0, 0)), tw, tw],
        out_specs=[blk, blk],
        out_shape=[jax.ShapeDtypeStruct((na, cols), BF16)] * 2,
        compiler_params=_params(("arbitrary",)),
        name="dft_a",
    )(zr2, zi2, wa, tc3, ts3)


def _dft_c_kernel(br_ref, bi_ref, m_ref, wf_ref, o_ref):
    kb = br_ref.shape[0]
    outs = []
    for j in range(kb):
        ab = jnp.concatenate([br_ref[j], bi_ref[j]], axis=0)
        y = jnp.dot(m_ref[...], ab, preferred_element_type=F32)
        outs.append(jnp.dot(y.astype(BF16), wf_ref[...], preferred_element_type=F32))
    o_ref[...] = jnp.concatenate(outs, axis=1).astype(BF16)


def _dft_c(br3, bi3, mm, wf, *, kb):
    na, nb, fw = br3.shape
    blk = pl.BlockSpec((kb, nb, fw), lambda i: (i, 0, 0))
    return pl.pallas_call(
        _dft_c_kernel,
        grid=(na // kb,),
        in_specs=[blk, blk,
                  pl.BlockSpec(mm.shape, lambda i: (0, 0)),
                  pl.BlockSpec(wf.shape, lambda i: (0, 0))],
        out_specs=pl.BlockSpec((nb, kb * fw), lambda i: (0, i)),
        out_shape=jax.ShapeDtypeStruct((nb, na * fw), BF16),
        compiler_params=_params(("arbitrary",)),
        name="dft_c",
    )(br3, bi3, mm, wf)


def _outproj_kernel(four_ref, attn_ref, x_ref, mod_ref, g_ref, wf_ref, wa_ref, wr_ref,
                    x1_ref, h_ref, aff_ref):
    mix = jnp.dot(four_ref[...], wf_ref[...], preferred_element_type=F32)
    for g in range(N_KV_HEADS):
        mix = mix + jnp.dot(attn_ref[g], wa_ref[g], preferred_element_type=F32)
    gate1 = mod_ref[0:1, 2 * D_MODEL:3 * D_MODEL]
    shift2 = mod_ref[0:1, 3 * D_MODEL:4 * D_MODEL]
    scale2 = mod_ref[0:1, 4 * D_MODEL:5 * D_MODEL]
    x1 = x_ref[...] + gate1 * mix
    x1_ref[...] = x1
    h = _rms_modulate(x1, g_ref[...], shift2, scale2).astype(BF16)
    h_ref[...] = h
    logits = lax.dot_general(wr_ref[...], h, (((1,), (1,)), ((), ())),
                             preferred_element_type=F32)
    mx = jnp.max(logits, axis=0, keepdims=True)
    e = jnp.exp(logits - mx)
    aff_ref[...] = e / jnp.sum(e, axis=0, keepdims=True)


def _outproj(four, attn, x, mod, g, wo_f, wo_a, wr_t, *, tm):
    n = x.shape[0]
    gw = GROUP_PAD
    const = lambda shape: pl.BlockSpec(shape, lambda i: (0,) * len(shape))
    return pl.pallas_call(
        _outproj_kernel,
        grid=(n // tm,),
        in_specs=[pl.BlockSpec((tm, FOURIER_WIDTH), lambda i: (i, 0)),
                  pl.BlockSpec((N_KV_HEADS, tm, gw), lambda i: (0, i, 0)),
                  pl.BlockSpec((tm, D_MODEL), lambda i: (i, 0)),
                  const(mod.shape), const(g.shape), const(wo_f.shape), const(wo_a.shape),
                  const(wr_t.shape)],
        out_specs=[pl.BlockSpec((tm, D_MODEL), lambda i: (i, 0)),
                   pl.BlockSpec((tm, D_MODEL), lambda i: (i, 0)),
                   pl.BlockSpec((N_EXPERTS, tm), lambda i: (0, i))],
        out_shape=[jax.ShapeDtypeStruct((n, D_MODEL), F32),
                   jax.ShapeDtypeStruct((n, D_MODEL), BF16),
                   jax.ShapeDtypeStruct((N_EXPERTS, n), F32)],
        compiler_params=_params(("arbitrary",)),
        name="outproj",
    )(four, attn, x, mod, g, wo_f, wo_a, wr_t)


def _count(m):
    return jnp.sum(jnp.sum(m, axis=2, keepdims=True), axis=1, keepdims=True)


def _topk_kernel(aff_ref, ut_ref, ones_ref, slt_ref, slot_ref, off_ref, *, cap):
    e_n, r_n, _ = aff_ref.shape
    aff = aff_ref[...]
    capf = jnp.float32(cap)

    def count_ge(th):
        return _count(jnp.where(aff >= th, 1.0, 0.0))

    def search(i, t):
        cand = t | jnp.left_shift(jnp.int32(1), 30 - i)
        c = count_ge(lax.bitcast_convert_type(cand, F32))
        return jnp.where(c >= capf, cand, t)

    tbits = lax.fori_loop(0, 31, search, jnp.zeros((e_n, 1, 1), jnp.int32))
    thr = lax.bitcast_convert_type(tbits, F32)
    ulp = lax.bitcast_convert_type(tbits + 1, F32) - thr

    def refine(_, carry):
        t, step = carry
        step = step * 0.5
        cand = t + step
        return jnp.where(count_ge(cand) >= capf, cand, t), step

    thr, _ = lax.fori_loop(0, 29, refine, (thr, ulp))
    gt = jnp.where(aff > thr, 1.0, 0.0)
    eq = jnp.where(aff == thr, 1.0, 0.0)
    need = capf - _count(gt)

    def excl_prefix(m):
        m2 = m.reshape(e_n * r_n, LANES).astype(BF16)
        within = jnp.dot(m2, ut_ref[...], preferred_element_type=F32).reshape(e_n, r_n, LANES)
        tot = jnp.dot(m2, ones_ref[...], preferred_element_type=F32).reshape(e_n, r_n, LANES)
        rows = jnp.stack([jnp.dot(slt_ref[...], tot[e].astype(BF16), preferred_element_type=F32)
                          for e in range(e_n)], axis=0)
        return within + rows, rows

    pe, _ = excl_prefix(eq)
    sel = jnp.maximum(gt, eq * jnp.where(pe < need, 1.0, 0.0))
    pos, rows = excl_prefix(sel)
    slot_ref[...] = jnp.where(sel > 0.5, pos, -1.0).astype(jnp.int32)
    off_ref[...] = rows.astype(jnp.int32)


def _topk(aff3, ut, ones, slt, *, cap):
    shp = aff3.shape
    full = lambda a: pl.BlockSpec(a.shape, lambda i: (0,) * a.ndim)
    return pl.pallas_call(
        functools.partial(_topk_kernel, cap=cap),
        grid=(1,),
        in_specs=[full(aff3), full(ut), full(ones), full(slt)],
        out_specs=[pl.BlockSpec(shp, lambda i: (0, 0, 0))] * 2,
        out_shape=[jax.ShapeDtypeStruct(shp, jnp.int32)] * 2,
        compiler_params=_params(("arbitrary",)),
        name="topk",
    )(aff3, ut, ones, slt)


def _window_start(off):
    return pl.multiple_of((off // BF16_SUBLANES) * BF16_SUBLANES, BF16_SUBLANES)


def _tile_max_count(off_ref, experts, t, nsub):
    cnts = [off_ref[e, (t + 1) * nsub] - off_ref[e, t * nsub] for e in experts]
    return functools.reduce(jnp.maximum, cnts)


def _gather_kernel(off_ref, slot_ref, aff_ref, h_ref, xe_ref, gate_ref, *, fast_max):
    p = pl.program_id(0)
    t = pl.program_id(1)
    eg, _, tt = slot_ref.shape
    nsub = tt // LANES
    experts = [p * eg + j for j in range(eg)]

    @pl.when(t == 0)
    def _():
        xe_ref[...] = jnp.zeros_like(xe_ref)
        gate_ref[...] = jnp.zeros_like(gate_ref)

    def scatter_rows(j, ws, hit, res, aff_row):
        xe_ref[j, pl.ds(ws, GATHER_WIN), :] += res.astype(BF16)
        g = jnp.sum(jnp.where(hit, aff_row, 0.0), axis=1, keepdims=True)
        gate_ref[j, pl.ds(ws, GATHER_WIN), :] += jnp.broadcast_to(g, (GATHER_WIN, LANES))

    fast = _tile_max_count(off_ref, experts, t, nsub) <= fast_max

    @pl.when(fast)
    def _():
        rows = lax.broadcasted_iota(jnp.int32, (GATHER_WIN, tt), 0)
        starts = [_window_start(off_ref[e, t * nsub]) for e in experts]
        hits = [rows == (slot_ref[j] - starts[j]) for j in range(eg)]
        onehot = jnp.concatenate([jnp.where(hit, 1.0, 0.0).astype(BF16) for hit in hits], axis=0)
        res = jnp.dot(onehot, h_ref[...], preferred_element_type=F32)
        for j in range(eg):
            scatter_rows(j, starts[j], hits[j], res[j * GATHER_WIN:(j + 1) * GATHER_WIN], aff_ref[j])

    @pl.when(jnp.logical_not(fast))
    def _():
        rows = lax.broadcasted_iota(jnp.int32, (GATHER_WIN, LANES), 0)
        for j in range(eg):
            for b in range(nsub):
                ws = _window_start(off_ref[experts[j], t * nsub + b])
                lanes = slice(b * LANES, (b + 1) * LANES)
                hit = rows == (slot_ref[j, :, lanes] - ws)
                res = jnp.dot(jnp.where(hit, 1.0, 0.0).astype(BF16), h_ref[lanes, :],
                              preferred_element_type=F32)
                scatter_rows(j, ws, hit, res, aff_ref[j, :, lanes])


def _gather(off, slot3, aff3, h, *, cap, tt, eg):
    n = h.shape[0]
    capp = cap + GATHER_WIN
    grid_spec = pltpu.PrefetchScalarGridSpec(
        num_scalar_prefetch=1,
        grid=(N_EXPERTS // eg, n // tt),
        in_specs=[pl.BlockSpec((eg, 1, tt), lambda p, t, off: (p, 0, t)),
                  pl.BlockSpec((eg, 1, tt), lambda p, t, off: (p, 0, t)),
                  pl.BlockSpec((tt, D_MODEL), lambda p, t, off: (t, 0))],
        out_specs=[pl.BlockSpec((eg, capp, D_MODEL), lambda p, t, off: (p, 0, 0)),
                   pl.BlockSpec((eg, capp, LANES), lambda p, t, off: (p, 0, 0))])
    return pl.pallas_call(
        functools.partial(_gather_kernel, fast_max=GATHER_FAST_MAX),
        grid_spec=grid_spec,
        out_shape=[jax.ShapeDtypeStruct((N_EXPERTS, capp, D_MODEL), BF16),
                   jax.ShapeDtypeStruct((N_EXPERTS, capp, LANES), F32)],
        compiler_params=_params(("arbitrary", "arbitrary")),
        name="gather",
    )(off, slot3, aff3, h)


def _ffn_kernel(xe_ref, gate_ref, wg_ref, wu_ref, wd_ref, ye_ref, acc_ref, *, mc):
    f = pl.program_id(1)
    cap = xe_ref.shape[1]
    wg = wg_ref[0].astype(BF16)
    wu = wu_ref[0].astype(BF16)
    wd = wd_ref[0].astype(BF16)
    for m in range(cap // mc):
        xm = xe_ref[0, m * mc:(m + 1) * mc, :]
        gate = jnp.dot(xm, wg, preferred_element_type=F32)
        up = jnp.dot(xm, wu, preferred_element_type=F32)
        hid = (jax.nn.silu(gate) * up).astype(BF16)
        y = jnp.dot(hid, wd, preferred_element_type=F32)

        @pl.when(f == 0)
        def _():
            acc_ref[m * mc:(m + 1) * mc, :] = y

        @pl.when(f != 0)
        def _():
            acc_ref[m * mc:(m + 1) * mc, :] += y

    @pl.when(f == pl.num_programs(1) - 1)
    def _():
        for m in range(cap // mc):
            rows = slice(m * mc, (m + 1) * mc)
            gate = jnp.concatenate([gate_ref[0, rows, :]] * (D_MODEL // LANES), axis=1)
            ye_ref[0, rows, :] = (acc_ref[rows, :] * gate).astype(BF16)


def _ffn(xe, gate, w_gate, w_up, w_down, *, cap, tf, mc):
    e_n = xe.shape[0]
    return pl.pallas_call(
        functools.partial(_ffn_kernel, mc=mc),
        grid=(e_n, D_EXPERT // tf),
        in_specs=[pl.BlockSpec((1, cap, D_MODEL), lambda e, f: (e, 0, 0)),
                  pl.BlockSpec((1, cap, LANES), lambda e, f: (e, 0, 0)),
                  pl.BlockSpec((1, D_MODEL, tf), lambda e, f: (e, 0, f)),
                  pl.BlockSpec((1, D_MODEL, tf), lambda e, f: (e, 0, f)),
                  pl.BlockSpec((1, tf, D_MODEL), lambda e, f: (e, f, 0))],
        out_specs=pl.BlockSpec((1, cap, D_MODEL), lambda e, f: (e, 0, 0)),
        out_shape=jax.ShapeDtypeStruct((e_n, cap, D_MODEL), BF16),
        scratch_shapes=[pltpu.VMEM((cap, D_MODEL), F32)],
        compiler_params=_params(("arbitrary", "arbitrary")),
        name="ffn",
    )(xe, gate, w_gate, w_up, w_down)


def _combine_kernel(off_ref, x1_ref, slot_ref, mod_ref, g_ref, ye_ref, o_ref,
                    wbuf_ref, buf_ref, wsem_ref, sem_ref, *, win, fast_max):
    t = pl.program_id(0)
    tt = x1_ref.shape[0]
    cap = ye_ref.shape[1]
    nsub = tt // LANES
    big = tt + BF16_SUBLANES
    experts = list(range(N_EXPERTS))

    def finish(moe):
        gate2 = mod_ref[0:1, 5 * D_MODEL:6 * D_MODEL]
        x2 = x1_ref[...] + gate2 * moe
        ms = jnp.mean(x2 * x2, axis=-1, keepdims=True)
        o_ref[...] = x2 * lax.rsqrt(ms + NORM_EPS) * g_ref[...]

    def clamped_start(e, rows):
        return pl.multiple_of(jnp.minimum(_window_start(off_ref[e, t * nsub]), cap - rows),
                              BF16_SUBLANES)

    fast = _tile_max_count(off_ref, experts, t, nsub) <= fast_max

    @pl.when(fast)
    def _():
        starts = [clamped_start(e, LANES) for e in experts]
        copies = [pltpu.make_async_copy(ye_ref.at[e, pl.ds(starts[e], LANES), :],
                                        wbuf_ref.at[pl.ds(e * LANES, LANES), :], wsem_ref.at[0])
                  for e in experts]
        for cp in copies:
            cp.start()
        cols = lax.broadcasted_iota(jnp.int32, (tt, LANES), 1)
        onehot = jnp.concatenate(
            [jnp.where(cols == (slot_ref[:, e:e + 1] - starts[e]), 1.0, 0.0).astype(BF16)
             for e in experts], axis=1)
        for cp in copies:
            cp.wait()
        finish(jnp.dot(onehot, wbuf_ref[...], preferred_element_type=F32))

    @pl.when(jnp.logical_not(fast))
    def _():
        def copy(e, slot):
            return pltpu.make_async_copy(ye_ref.at[e, pl.ds(clamped_start(e, big), big), :],
                                         buf_ref.at[slot], sem_ref.at[slot])

        copy(0, 0).start()
        cols = lax.broadcasted_iota(jnp.int32, (LANES, win), 1)
        accs = [jnp.zeros((LANES, D_MODEL), F32) for _ in range(nsub)]
        for e in experts:
            slot = e % 2
            if e + 1 < N_EXPERTS:
                copy(e + 1, 1 - slot).start()
            copy(e, slot).wait()
            wb = clamped_start(e, big)
            for j in range(nsub):
                ws = jnp.minimum(_window_start(off_ref[e, t * nsub + j]), wb + (big - win))
                rel = slot_ref[j * LANES:(j + 1) * LANES, e:e + 1] - ws
                onehot = jnp.where(cols == rel, 1.0, 0.0).astype(BF16)
                r0 = pl.multiple_of(ws - wb, BF16_SUBLANES)
                accs[j] = accs[j] + jnp.dot(onehot, buf_ref[slot, pl.ds(r0, win), :],
                                            preferred_element_type=F32)
        finish(jnp.concatenate(accs, axis=0))


def _combine(off, x1, slot_t, mod, g, ye, *, tt):
    n = x1.shape[0]
    win = min(2 * LANES, tt + BF16_SUBLANES)
    grid_spec = pltpu.PrefetchScalarGridSpec(
        num_scalar_prefetch=1,
        grid=(n // tt,),
        in_specs=[pl.BlockSpec((tt, D_MODEL), lambda t, off: (t, 0)),
                  pl.BlockSpec((tt, N_EXPERTS), lambda t, off: (t, 0)),
                  pl.BlockSpec(mod.shape, lambda t, off: (0, 0)),
                  pl.BlockSpec(g.shape, lambda t, off: (0, 0)),
                  pl.BlockSpec(memory_space=pl.ANY)],
        out_specs=pl.BlockSpec((tt, D_MODEL), lambda t, off: (t, 0)),
        scratch_shapes=[pltpu.VMEM((N_EXPERTS * LANES, D_MODEL), BF16),
                        pltpu.VMEM((2, tt + BF16_SUBLANES, D_MODEL), BF16),
                        pltpu.SemaphoreType.DMA((1,)),
                        pltpu.SemaphoreType.DMA((2,))])
    return pl.pallas_call(
        functools.partial(_combine_kernel, win=win, fast_max=COMBINE_FAST_MAX),
        grid_spec=grid_spec,
        out_shape=jax.ShapeDtypeStruct((n, D_MODEL), F32),
        compiler_params=_params(("arbitrary",)),
        name="combine",
    )(off, x1, slot_t, mod, g, ye)


def _rope_tables(n):
    rows = n // GRID_W
    half = HEAD_DIM // 4
    freqs = ROPE_BASE ** (-jnp.arange(half, dtype=F32) / half)
    ang_r = jnp.arange(rows, dtype=jnp.int32).astype(F32)[:, None] * freqs[None, :]
    ang_c = jnp.arange(GRID_W, dtype=jnp.int32).astype(F32)[:, None] * freqs[None, :]
    zr = jnp.zeros((rows, 2 * half), F32)
    zc = jnp.zeros((GRID_W, 2 * half), F32)
    head = lambda a, b: jnp.concatenate([a, b] * 2, axis=1)
    rcos = head(jnp.concatenate([jnp.cos(ang_r)] * 2, axis=1), zr)
    rsin = head(jnp.concatenate([-jnp.sin(ang_r), jnp.sin(ang_r)], axis=1), zr)
    ccos = head(zc, jnp.concatenate([jnp.cos(ang_c)] * 2, axis=1))
    csin = head(zc, jnp.concatenate([-jnp.sin(ang_c), jnp.sin(ang_c)], axis=1))
    return rcos, rsin, ccos, csin


def _block_diag(blocks):
    k = len(blocks)
    r, c = blocks[0].shape
    out = jnp.zeros((k * r, k * c), blocks[0].dtype)
    for i, b in enumerate(blocks):
        out = out.at[i * r:(i + 1) * r, i * c:(i + 1) * c].set(b)
    return out


def _dft_tables(n, cb_n2):
    nb = LANES
    na = n // nb
    gd = FOURIER_GROUP_DIM
    scale = 1.0 / math.sqrt(n * gd)

    def cos_sin(i, j, period):
        ang = ((i * j) % period).astype(F32) * (2.0 * math.pi / period)
        return jnp.cos(ang), jnp.sin(ang)

    ar_g = jnp.arange(gd, dtype=jnp.int32)
    c64, s64 = cos_sin(ar_g[:, None], ar_g[None, :], gd)
    w0 = jnp.concatenate([_block_diag([c64 * scale] * N_FOURIER_GROUPS),
                          _block_diag([-s64 * scale] * N_FOURIER_GROUPS)], axis=1)
    ar_a = jnp.arange(na, dtype=jnp.int32)
    ca, sa = cos_sin(ar_a[:, None], ar_a[None, :], na)
    wa = jnp.concatenate([jnp.concatenate([ca, sa], axis=1),
                          jnp.concatenate([-sa, ca], axis=1)], axis=0)
    ar_b = jnp.arange(nb, dtype=jnp.int32)
    ct, st = cos_sin(ar_a[:, None], ar_b[None, :], n)
    group = lambda t: t.reshape(na, nb // cb_n2, cb_n2).transpose(1, 0, 2)
    cc, sc = cos_sin(ar_b[:, None], ar_b[None, :], nb)
    mm = jnp.concatenate([cc, sc], axis=1)
    return w0.astype(BF16), wa.astype(BF16), group(ct), group(st), mm.astype(BF16)


def _prefix_tables(r_n):
    i = np.arange(LANES)
    ut = (i[:, None] < i[None, :]).astype(np.float32)
    ones = np.ones((LANES, LANES), np.float32)
    r = np.arange(r_n)
    slt = (r[None, :] < r[:, None]).astype(np.float32)
    as_bf16 = lambda a: jnp.asarray(a).astype(BF16)
    return as_bf16(ut), as_bf16(ones), as_bf16(slt)


def kernel(x, c, ctx, c_ctx, w_mod, b_mod, norm_mix_g, norm_ffn_g, w_in, w_four, q_norm_g, k_norm_g,
           w_out, w_router, w_gate, w_up, w_down, final_norm_g):
    batch, n, _ = x.shape
    assert batch == 1 and w_mod.shape[0] == 1
    lc = ctx.shape[1]
    cap = EC_CAPACITY_FACTOR * n // N_EXPERTS
    x2d = x[0]
    ctx2d = ctx[0]

    cc = jnp.zeros((8, D_MODEL), F32).at[0].set(c[0]).at[1].set(c_ctx)
    mod = _mod(cc, w_mod[0], b_mod[0][None, :])

    rope_x = _rope_tables(n)
    zero_c = jnp.zeros((GRID_W, LANES), F32)
    rope_c = (jnp.ones((lc // GRID_W, LANES), F32), jnp.zeros((lc // GRID_W, LANES), F32), zero_c, zero_c)
    inv_hd = jnp.full((HEAD_DIM, HEAD_DIM), 1.0 / HEAD_DIM, BF16)
    bq = _block_diag([inv_hd] * N_Q_HEADS)
    bk = _block_diag([inv_hd] * N_KV_HEADS)
    q_scale = (HEAD_DIM ** -0.5) * math.log2(math.e)
    gq = jnp.tile(q_norm_g[0], N_Q_HEADS)[None, :] * q_scale
    gk = jnp.tile(k_norm_g[0], N_KV_HEADS)[None, :]
    nb = LANES
    na = n // nb
    dft_cb = min(4096, nb * FOURIER_WIDTH)
    w0, wa, tw_c, tw_s, mm = _dft_tables(n, dft_cb // FOURIER_WIDTH)
    w_in_b = w_in[0].astype(BF16)
    g_mix = norm_mix_g[0][None, :]
    tm = min(512, n)
    zr, zi, q_t, k_l, v_t = _inproj(x2d, mod, g_mix, w_in_b, rope_x, bq, bk, gq, gk, w0,
                                    mod_row=0, with_q=True, tm=tm)
    k_c, v_c = _inproj(ctx2d, mod, g_mix, w_in_b, rope_c, bq, bk, gq, gk, w0,
                       mod_row=1, with_q=False, tm=lc)

    attn = _attention(q_t, k_l, v_t, k_c, v_c, tq=min(256, n), tk=min(1024, n))

    br, bi = _dft_a(zr.reshape(na, nb * FOURIER_WIDTH), zi.reshape(na, nb * FOURIER_WIDTH), wa,
                    tw_c, tw_s, cb=dft_cb)
    wf = _block_diag([w_four[0, g] for g in range(N_FOURIER_GROUPS)]).astype(BF16)
    four = _dft_c(br.reshape(na, nb, FOURIER_WIDTH), bi.reshape(na, nb, FOURIER_WIDTH), mm, wf,
                  kb=8).reshape(n, FOURIER_WIDTH)

    w_out_b = w_out[0].astype(BF16)
    wo_f = w_out_b[:FOURIER_WIDTH]
    wo_a = w_out_b[FOURIER_WIDTH:].reshape(N_KV_HEADS, Q_PER_KV * HEAD_DIM, D_MODEL)
    wo_a = jnp.pad(wo_a, ((0, 0), (0, GROUP_PAD - Q_PER_KV * HEAD_DIM), (0, 0)))
    wr_t = w_router[0].T.astype(BF16)
    x1, h2, aff = _outproj(four, attn, x2d, mod, norm_ffn_g[0][None, :], wo_f, wo_a, wr_t, tm=tm)

    r_n = n // LANES
    ut, ones, slt = _prefix_tables(r_n)
    slot3, off3 = _topk(aff.reshape(N_EXPERTS, r_n, LANES), ut, ones, slt, cap=cap)
    off = jnp.concatenate([off3[:, :, 0], jnp.full((N_EXPERTS, 1), cap, jnp.int32)], axis=1)
    slot = slot3.reshape(N_EXPERTS, n)

    tt = (min(512, cap - BF16_SUBLANES) // LANES) * LANES
    xe, gate = _gather(off, slot.reshape(N_EXPERTS, 1, n), aff.reshape(N_EXPERTS, 1, n), h2,
                       cap=cap, tt=tt, eg=GATHER_EXPERTS)
    ye = _ffn(xe, gate, w_gate[0], w_up[0], w_down[0], cap=cap, tf=512, mc=min(512, cap))
    out = _combine(off, x1, slot.T, mod, final_norm_g[None, :], ye, tt=tt)
    return out[None]
```

```python
import functools
import math

import numpy as np
import jax
import jax.numpy as jnp
from jax import lax
from jax.experimental import pallas as pl
from jax.experimental.pallas import tpu as pltpu

D_MODEL = 1024
GRID_W = 64
HEAD_DIM = 64
N_Q_HEADS = 12
N_KV_HEADS = 4
Q_PER_KV = N_Q_HEADS // N_KV_HEADS
ATTN_WIDTH = N_Q_HEADS * HEAD_DIM
KV_WIDTH = N_KV_HEADS * HEAD_DIM
N_FOURIER_GROUPS = 4
FOURIER_GROUP_DIM = 64
FOURIER_WIDTH = N_FOURIER_GROUPS * FOURIER_GROUP_DIM
IN_WIDTH = FOURIER_WIDTH + ATTN_WIDTH + 2 * KV_WIDTH
ROPE_BASE = 10000.0
N_EXPERTS = 16
EC_CAPACITY_FACTOR = 2
D_EXPERT = 2048
N_MOD = 6
NORM_EPS = 1e-6

LANES = 128
BF16_SUBLANES = 16
GROUP_PAD = 2 * LANES
V_ROWS = HEAD_DIM + BF16_SUBLANES
GATHER_WIN = LANES + BF16_SUBLANES
GATHER_FAST_MAX = LANES
COMBINE_FAST_MAX = LANES - BF16_SUBLANES
GATHER_EXPERTS = 4
VMEM_LIMIT = 56 * 1024 * 1024

F32 = jnp.float32
BF16 = jnp.bfloat16


def _params(sem):
    return pltpu.CompilerParams(dimension_semantics=sem, vmem_limit_bytes=VMEM_LIMIT)


def _mod_kernel(c_ref, w_ref, b_ref, o_ref):
    s = jax.nn.silu(c_ref[...])
    o_ref[...] = jnp.dot(s, w_ref[...], precision=lax.Precision.HIGHEST,
                         preferred_element_type=F32) + b_ref[...]


def _mod(cc, w_mod, b_mod):
    tn = 1024
    n = w_mod.shape[1]
    return pl.pallas_call(
        _mod_kernel,
        grid=(n // tn,),
        in_specs=[pl.BlockSpec((8, D_MODEL), lambda j: (0, 0)),
                  pl.BlockSpec((D_MODEL, tn), lambda j: (0, j)),
                  pl.BlockSpec((1, tn), lambda j: (0, j))],
        out_specs=pl.BlockSpec((8, tn), lambda j: (0, j)),
        out_shape=jax.ShapeDtypeStruct((8, n), F32),
        compiler_params=_params(("arbitrary",)),
        name="mod",
    )(cc, w_mod, b_mod)


def _rms_modulate(x, g, shift, scale):
    ms = jnp.mean(x * x, axis=-1, keepdims=True)
    y = x * lax.rsqrt(ms + NORM_EPS) * g
    return y * (1.0 + scale) + shift


def _head_norm(t, bmat, gain):
    ms = jnp.dot((t * t).astype(BF16), bmat, preferred_element_type=F32)
    return t * lax.rsqrt(ms + NORM_EPS) * gain


def _rope(t, cos, sin, lo_mask):
    outs = []
    for c in range(t.shape[1] // LANES):
        xc = t[:, c * LANES:(c + 1) * LANES]
        up = pltpu.roll(xc, LANES - 16, axis=1)
        dn = pltpu.roll(xc, 16, axis=1)
        rot = jnp.where(lo_mask, up, dn)
        outs.append(xc * cos + rot * sin)
    return jnp.concatenate(outs, axis=1)


def _inproj_kernel(x_ref, mod_ref, g_ref, w_ref, rcos_ref, rsin_ref, ccos_ref, csin_ref, bq_ref, bk_ref,
                   gq_ref, gk_ref, w0_ref, *out_refs, mod_row, with_q):
    if with_q:
        zr_ref, zi_ref, q_ref, k_ref, v_ref = out_refs
    else:
        k_ref, v_ref = out_refs
    tm = x_ref.shape[0]
    shift = mod_ref[mod_row:mod_row + 1, 0:D_MODEL]
    scale = mod_ref[mod_row:mod_row + 1, D_MODEL:2 * D_MODEL]
    h = _rms_modulate(x_ref[...], g_ref[...], shift, scale).astype(BF16)
    cos = jnp.concatenate([rcos_ref[r:r + 1, :] + ccos_ref[...] for r in range(tm // GRID_W)], axis=0)
    sin = jnp.concatenate([rsin_ref[r:r + 1, :] + csin_ref[...] for r in range(tm // GRID_W)], axis=0)
    lane = lax.broadcasted_iota(jnp.int32, (tm, LANES), 1)
    lo_mask = (lane % 32) < 16
    k0 = FOURIER_WIDTH + ATTN_WIDTH
    if with_q:
        p = jnp.dot(h, w_ref[...], preferred_element_type=F32)
        f = p[:, :FOURIER_WIDTH].astype(BF16)
        z = jnp.dot(f, w0_ref[...], preferred_element_type=F32)
        zr_ref[...] = z[:, :FOURIER_WIDTH].astype(BF16)
        zi_ref[...] = z[:, FOURIER_WIDTH:].astype(BF16)
        q = _head_norm(p[:, FOURIER_WIDTH:k0], bq_ref[...], gq_ref[...])
        q = _rope(q, cos, sin, lo_mask)
        q_ref[...] = q.T.astype(BF16).reshape(N_Q_HEADS, HEAD_DIM, tm)
        pk = p[:, k0:k0 + KV_WIDTH]
        pv = p[:, k0 + KV_WIDTH:]
    else:
        pkv = jnp.dot(h, w_ref[:, k0:], preferred_element_type=F32)
        pk = pkv[:, :KV_WIDTH]
        pv = pkv[:, KV_WIDTH:]
    k = _head_norm(pk, bk_ref[...], gk_ref[...])
    k = _rope(k, cos, sin, lo_mask)
    for g in range(N_KV_HEADS):
        k_ref[g] = k[:, g * HEAD_DIM:(g + 1) * HEAD_DIM].astype(BF16)
    v_ref[:, 0:HEAD_DIM, :] = pv.T.astype(BF16).reshape(N_KV_HEADS, HEAD_DIM, tm)
    v_ref[:, HEAD_DIM:V_ROWS, :] = jnp.ones((N_KV_HEADS, V_ROWS - HEAD_DIM, tm), BF16)


def _inproj(x, mod, g, w_in, rope, bq, bk, gq, gk, w0, *, mod_row, with_q, tm):
    n = x.shape[0]
    rcos, rsin, ccos, csin = rope
    const = lambda shape: pl.BlockSpec(shape, lambda i: (0,) * len(shape))
    in_specs = [pl.BlockSpec((tm, D_MODEL), lambda i: (i, 0)),
                const(mod.shape), const(g.shape), const(w_in.shape),
                pl.BlockSpec((tm // GRID_W, LANES), lambda i: (i, 0)),
                pl.BlockSpec((tm // GRID_W, LANES), lambda i: (i, 0)),
                const(ccos.shape), const(csin.shape),
                const(bq.shape), const(bk.shape), const(gq.shape), const(gk.shape),
                const(w0.shape)]
    kv_specs = [pl.BlockSpec((N_KV_HEADS, tm, HEAD_DIM), lambda i: (0, i, 0)),
                pl.BlockSpec((N_KV_HEADS, V_ROWS, tm), lambda i: (0, 0, i))]
    kv_shapes = [jax.ShapeDtypeStruct((N_KV_HEADS, n, HEAD_DIM), BF16),
                 jax.ShapeDtypeStruct((N_KV_HEADS, V_ROWS, n), BF16)]
    if with_q:
        out_specs = [pl.BlockSpec((tm, FOURIER_WIDTH), lambda i: (i, 0)),
                     pl.BlockSpec((tm, FOURIER_WIDTH), lambda i: (i, 0)),
                     pl.BlockSpec((N_Q_HEADS, HEAD_DIM, tm), lambda i: (0, 0, i))] + kv_specs
        out_shapes = [jax.ShapeDtypeStruct((n, FOURIER_WIDTH), BF16),
                      jax.ShapeDtypeStruct((n, FOURIER_WIDTH), BF16),
                      jax.ShapeDtypeStruct((N_Q_HEADS, HEAD_DIM, n), BF16)] + kv_shapes
    else:
        out_specs, out_shapes = kv_specs, kv_shapes
    return pl.pallas_call(
        functools.partial(_inproj_kernel, mod_row=mod_row, with_q=with_q),
        grid=(n // tm,),
        in_specs=in_specs,
        out_specs=out_specs,
        out_shape=out_shapes,
        compiler_params=_params(("arbitrary",)),
        name="inproj_x" if with_q else "inproj_ctx",
    )(x, mod, g, w_in, rcos, rsin, ccos, csin, bq, bk, gq, gk, w0)


def _attn_kernel(q_ref, k_ref, v_ref, kc_ref, vc_ref, o_ref, s_ref, acc_ref, *, tk):
    tq = q_ref.shape[2]
    n = k_ref.shape[1]
    lc = kc_ref.shape[1]
    nch = n // tk
    qs = jnp.concatenate([q_ref[j] for j in range(Q_PER_KV)], axis=1)

    def scores_latent(c, slot):
        start = pl.multiple_of(c * tk, tk)
        s_ref[slot] = jnp.dot(k_ref[0, pl.ds(start, tk), :], qs, preferred_element_type=F32)

    def v_latent(c):
        return v_ref[0, :, pl.ds(pl.multiple_of(c * tk, tk), tk)]

    def consume(slot, rows, vb, m):
        s = s_ref[slot, 0:rows, :]
        m_new = jnp.maximum(m, jnp.max(s, axis=0, keepdims=True))
        alpha = jnp.exp2(m - m_new)
        p = jnp.exp2(s - m_new).astype(BF16)
        acc_ref[...] = alpha * acc_ref[...] + jnp.dot(vb, p, preferred_element_type=F32)
        return m_new

    def pair(c0, m, prefetch):
        scores_latent(c0 + 1, 1)
        m = consume(0, tk, v_latent(c0), m)
        prefetch()
        return consume(1, tk, v_latent(c0 + 1), m)

    def body(i, m):
        return pair(2 * i, m, lambda: scores_latent(2 * i + 2, 0))

    def scores_context():
        s_ref[0, 0:lc, :] = jnp.dot(kc_ref[0], qs, preferred_element_type=F32)

    w = Q_PER_KV * tq
    acc_ref[...] = jnp.zeros_like(acc_ref)
    scores_latent(0, 0)
    m = lax.fori_loop(0, nch // 2 - 1, body, jnp.full((1, w), -1e30, F32))
    m = pair(nch - 2, m, scores_context)
    consume(0, lc, vc_ref[0], m)
    acc = acc_ref[...]
    o = acc[0:HEAD_DIM] / acc[HEAD_DIM:HEAD_DIM + 1]
    ot = jnp.concatenate([o, jnp.zeros_like(o)], axis=0).T
    h0, h1, h2 = (ot[j * tq:(j + 1) * tq] for j in range(Q_PER_KV))
    o_ref[0] = jnp.concatenate([h0 + pltpu.roll(h1, HEAD_DIM, axis=1), h2], axis=1).astype(BF16)


def _attention(q, k, v, kc, vc, *, tq, tk):
    n = k.shape[1]
    lc = kc.shape[1]
    gw = GROUP_PAD
    return pl.pallas_call(
        functools.partial(_attn_kernel, tk=tk),
        grid=(N_KV_HEADS, n // tq),
        in_specs=[pl.BlockSpec((Q_PER_KV, HEAD_DIM, tq), lambda g, i: (g, 0, i)),
                  pl.BlockSpec((1, n, HEAD_DIM), lambda g, i: (g, 0, 0)),
                  pl.BlockSpec((1, V_ROWS, n), lambda g, i: (g, 0, 0)),
                  pl.BlockSpec((1, lc, HEAD_DIM), lambda g, i: (g, 0, 0)),
                  pl.BlockSpec((1, V_ROWS, lc), lambda g, i: (g, 0, 0))],
        out_specs=pl.BlockSpec((1, tq, gw), lambda g, i: (g, i, 0)),
        out_shape=jax.ShapeDtypeStruct((N_KV_HEADS, n, gw), BF16),
        scratch_shapes=[pltpu.VMEM((2, tk, Q_PER_KV * tq), F32),
                        pltpu.VMEM((V_ROWS, Q_PER_KV * tq), F32)],
        compiler_params=_params(("arbitrary", "arbitrary")),
        name="attention",
    )(q, k, v, kc, vc)


def _dft_a_kernel(zr_ref, zi_ref, w_ref, tc_ref, ts_ref, br_ref, bi_ref):
    na = zr_ref.shape[0]
    fw = FOURIER_WIDTH
    zz = jnp.concatenate([zr_ref[...], zi_ref[...]], axis=0)
    a = jnp.dot(w_ref[...], zz, preferred_element_type=F32)
    tc = tc_ref[0]
    ts = ts_ref[0]
    for j in range(tc.shape[1]):
        ar = a[:na, j * fw:(j + 1) * fw]
        ai = a[na:, j * fw:(j + 1) * fw]
        c = tc[:, j:j + 1]
        s = ts[:, j:j + 1]
        br_ref[:, j * fw:(j + 1) * fw] = (ar * c + ai * s).astype(BF16)
        bi_ref[:, j * fw:(j + 1) * fw] = (ai * c - ar * s).astype(BF16)


def _dft_a(zr2, zi2, wa, tc3, ts3, *, cb):
    na, cols = zr2.shape
    blk = pl.BlockSpec((na, cb), lambda j: (0, j))
    tw = pl.BlockSpec((1,) + tc3.shape[1:], lambda j: (j, 0, 0))
    return pl.pallas_call(
        _dft_a_kernel,
        grid=(cols // cb,),
        in_specs=[blk, blk, pl.BlockSpec(wa.shape, lambda j: (0, 0)), tw, tw],
        out_specs=[blk, blk],
        out_shape=[jax.ShapeDtypeStruct((na, cols), BF16)] * 2,
        compiler_params=_params(("arbitrary",)),
        name="dft_a",
    )(zr2, zi2, wa, tc3, ts3)


def _dft_c_kernel(br_ref, bi_ref, m_ref, wf_ref, o_ref):
    kb = br_ref.shape[0]
    outs = []
    for j in range(kb):
        ab = jnp.concatenate([br_ref[j], bi_ref[j]], axis=0)
        y = jnp.dot(m_ref[...], ab, preferred_element_type=F32)
        outs.append(jnp.dot(y.astype(BF16), wf_ref[...], preferred_element_type=F32))
    o_ref[...] = jnp.concatenate(outs, axis=1).astype(BF16)


def _dft_c(br3, bi3, mm, wf, *, kb):
    na, nb, fw = br3.shape
    blk = pl.BlockSpec((kb, nb, fw), lambda i: (i, 0, 0))
    return pl.pallas_call(
        _dft_c_kernel,
        grid=(na // kb,),
        in_specs=[blk, blk,
                  pl.BlockSpec(mm.shape, lambda i: (0, 0)),
                  pl.BlockSpec(wf.shape, lambda i: (0, 0))],
        out_specs=pl.BlockSpec((nb, kb * fw), lambda i: (0, i)),
        out_shape=jax.ShapeDtypeStruct((nb, na * fw), BF16),
        compiler_params=_params(("arbitrary",)),
        name="dft_c",
    )(br3, bi3, mm, wf)


def _outproj_kernel(four_ref, attn_ref, x_ref, mod_ref, g_ref, wf_ref, wa_ref, wr_ref,
                    x1_ref, h_ref, aff_ref):
    mix = jnp.dot(four_ref[...], wf_ref[...], preferred_element_type=F32)
    for g in range(N_KV_HEADS):
        mix = mix + jnp.dot(attn_ref[g], wa_ref[g], preferred_element_type=F32)
    gate1 = mod_ref[0:1, 2 * D_MODEL:3 * D_MODEL]
    shift2 = mod_ref[0:1, 3 * D_MODEL:4 * D_MODEL]
    scale2 = mod_ref[0:1, 4 * D_MODEL:5 * D_MODEL]
    x1 = x_ref[...] + gate1 * mix
    x1_ref[...] = x1
    h = _rms_modulate(x1, g_ref[...], shift2, scale2).astype(BF16)
    h_ref[...] = h
    logits = lax.dot_general(wr_ref[...], h, (((1,), (1,)), ((), ())),
                             preferred_element_type=F32)
    mx = jnp.max(logits, axis=0, keepdims=True)
    e = jnp.exp(logits - mx)
    aff_ref[...] = e / jnp.sum(e, axis=0, keepdims=True)


def _outproj(four, attn, x, mod, g, wo_f, wo_a, wr_t, *, tm):
    n = x.shape[0]
    gw = GROUP_PAD
    const = lambda shape: pl.BlockSpec(shape, lambda i: (0,) * len(shape))
    return pl.pallas_call(
        _outproj_kernel,
        grid=(n // tm,),
        in_specs=[pl.BlockSpec((tm, FOURIER_WIDTH), lambda i: (i, 0)),
                  pl.BlockSpec((N_KV_HEADS, tm, gw), lambda i: (0, i, 0)),
                  pl.BlockSpec((tm, D_MODEL), lambda i: (i, 0)),
                  const(mod.shape), const(g.shape), const(wo_f.shape), const(wo_a.shape),
                  const(wr_t.shape)],
        out_specs=[pl.BlockSpec((tm, D_MODEL), lambda i: (i, 0)),
                   pl.BlockSpec((tm, D_MODEL), lambda i: (i, 0)),
                   pl.BlockSpec((N_EXPERTS, tm), lambda i: (0, i))],
        out_shape=[jax.ShapeDtypeStruct((n, D_MODEL), F32),
                   jax.ShapeDtypeStruct((n, D_MODEL), BF16),
                   jax.ShapeDtypeStruct((N_EXPERTS, n), F32)],
        compiler_params=_params(("arbitrary",)),
        name="outproj",
    )(four, attn, x, mod, g, wo_f, wo_a, wr_t)


def _count(m):
    return jnp.sum(jnp.sum(m, axis=2, keepdims=True), axis=1, keepdims=True)


def _topk_kernel(aff_ref, ut_ref, ones_ref, slt_ref, slot_ref, off_ref, *, cap):
    e_n, r_n, _ = aff_ref.shape
    aff = aff_ref[...]
    capf = jnp.float32(cap)

    def count_ge(th):
        return _count(jnp.where(aff >= th, 1.0, 0.0))

    def search(i, t):
        cand = t | jnp.left_shift(jnp.int32(1), 30 - i)
        c = count_ge(lax.bitcast_convert_type(cand, F32))
        return jnp.where(c >= capf, cand, t)

    tbits = lax.fori_loop(0, 31, search, jnp.zeros((e_n, 1, 1), jnp.int32))
    thr = lax.bitcast_convert_type(tbits, F32)
    ulp = lax.bitcast_convert_type(tbits + 1, F32) - thr

    def refine(_, carry):
        t, step = carry
        step = step * 0.5
        cand = t + step
        return jnp.where(count_ge(cand) >= capf, cand, t), step

    thr, _ = lax.fori_loop(0, 29, refine, (thr, ulp))
    gt = jnp.where(aff > thr, 1.0, 0.0)
    eq = jnp.where(aff == thr, 1.0, 0.0)
    need = capf - _count(gt)

    def excl_prefix(m):
        m2 = m.reshape(e_n * r_n, LANES).astype(BF16)
        within = jnp.dot(m2, ut_ref[...], preferred_element_type=F32).reshape(e_n, r_n, LANES)
        tot = jnp.dot(m2, ones_ref[...], preferred_element_type=F32).reshape(e_n, r_n, LANES)
        rows = jnp.stack([jnp.dot(slt_ref[...], tot[e].astype(BF16), preferred_element_type=F32)
                          for e in range(e_n)], axis=0)
        return within + rows, rows

    pe, _ = excl_prefix(eq)
    sel = jnp.maximum(gt, eq * jnp.where(pe < need, 1.0, 0.0))
    pos, rows = excl_prefix(sel)
    slot_ref[...] = jnp.where(sel > 0.5, pos, -1.0).astype(jnp.int32)
    off_ref[...] = rows.astype(jnp.int32)


def _topk(aff3, ut, ones, slt, *, cap):
    shp = aff3.shape
    full = lambda a: pl.BlockSpec(a.shape, lambda i: (0,) * a.ndim)
    return pl.pallas_call(
        functools.partial(_topk_kernel, cap=cap),
        grid=(1,),
        in_specs=[full(aff3), full(ut), full(ones), full(slt)],
        out_specs=[pl.BlockSpec(shp, lambda i: (0, 0, 0))] * 2,
        out_shape=[jax.ShapeDtypeStruct(shp, jnp.int32)] * 2,
        compiler_params=_params(("arbitrary",)),
        name="topk",
    )(aff3, ut, ones, slt)


def _window_start(off):
    return pl.multiple_of((off // BF16_SUBLANES) * BF16_SUBLANES, BF16_SUBLANES)


def _tile_max_count(off_ref, experts, t, nsub):
    cnts = [off_ref[e, (t + 1) * nsub] - off_ref[e, t * nsub] for e in experts]
    return functools.reduce(jnp.maximum, cnts)


def _gather_kernel(off_ref, slot_ref, aff_ref, h_ref, xe_ref, gate_ref, *, fast_max):
    p = pl.program_id(0)
    t = pl.program_id(1)
    eg, _, tt = slot_ref.shape
    nsub = tt // LANES
    experts = [p * eg + j for j in range(eg)]

    @pl.when(t == 0)
    def _():
        xe_ref[...] = jnp.zeros_like(xe_ref)
        gate_ref[...] = jnp.zeros_like(gate_ref)

    def scatter_rows(j, ws, hit, res, aff_row):
        xe_ref[j, pl.ds(ws, GATHER_WIN), :] += res.astype(BF16)
        g = jnp.sum(jnp.where(hit, aff_row, 0.0), axis=1, keepdims=True)
        gate_ref[j, pl.ds(ws, GATHER_WIN), :] += jnp.broadcast_to(g, (GATHER_WIN, LANES))

    fast = _tile_max_count(off_ref, experts, t, nsub) <= fast_max

    @pl.when(fast)
    def _():
        rows = lax.broadcasted_iota(jnp.int32, (GATHER_WIN, tt), 0)
        starts = [_window_start(off_ref[e, t * nsub]) for e in experts]
        hits = [rows == (slot_ref[j] - starts[j]) for j in range(eg)]
        onehot = jnp.concatenate([jnp.where(hit, 1.0, 0.0).astype(BF16) for hit in hits], axis=0)
        res = jnp.dot(onehot, h_ref[...], preferred_element_type=F32)
        for j in range(eg):
            scatter_rows(j, starts[j], hits[j], res[j * GATHER_WIN:(j + 1) * GATHER_WIN], aff_ref[j])

    @pl.when(jnp.logical_not(fast))
    def _():
        rows = lax.broadcasted_iota(jnp.int32, (GATHER_WIN, LANES), 0)
        for j in range(eg):
            for b in range(nsub):
                ws = _window_start(off_ref[experts[j], t * nsub + b])
                lanes = slice(b * LANES, (b + 1) * LANES)
                hit = rows == (slot_ref[j, :, lanes] - ws)
                res = jnp.dot(jnp.where(hit, 1.0, 0.0).astype(BF16), h_ref[lanes, :],
                              preferred_element_type=F32)
                scatter_rows(j, ws, hit, res, aff_ref[j, :, lanes])


def _gather(off, slot3, aff3, h, *, cap, tt, eg):
    n = h.shape[0]
    capp = cap + GATHER_WIN
    grid_spec = pltpu.PrefetchScalarGridSpec(
        num_scalar_prefetch=1,
        grid=(N_EXPERTS // eg, n // tt),
        in_specs=[pl.BlockSpec((eg, 1, tt), lambda p, t, off: (p, 0, t)),
                  pl.BlockSpec((eg, 1, tt), lambda p, t, off: (p, 0, t)),
                  pl.BlockSpec((tt, D_MODEL), lambda p, t, off: (t, 0))],
        out_specs=[pl.BlockSpec((eg, capp, D_MODEL), lambda p, t, off: (p, 0, 0)),
                   pl.BlockSpec((eg, capp, LANES), lambda p, t, off: (p, 0, 0))])
    return pl.pallas_call(
        functools.partial(_gather_kernel, fast_max=GATHER_FAST_MAX),
        grid_spec=grid_spec,
        out_shape=[jax.ShapeDtypeStruct((N_EXPERTS, capp, D_MODEL), BF16),
                   jax.ShapeDtypeStruct((N_EXPERTS, capp, LANES), F32)],
        compiler_params=_params(("arbitrary", "arbitrary")),
        name="gather",
    )(off, slot3, aff3, h)


def _ffn_kernel(xe_ref, gate_ref, wg_ref, wu_ref, wd_ref, ye_ref, acc_ref, *, mc):
    f = pl.program_id(1)
    cap = xe_ref.shape[1]
    wg = wg_ref[0].astype(BF16)
    wu = wu_ref[0].astype(BF16)
    wd = wd_ref[0].astype(BF16)

    @pl.when(f == 0)
    def _():
        acc_ref[...] = jnp.zeros_like(acc_ref)

    for m in range(cap // mc):
        rows = slice(m * mc, (m + 1) * mc)
        xm = xe_ref[0, rows, :]
        gate = jnp.dot(xm, wg, preferred_element_type=F32)
        up = jnp.dot(xm, wu, preferred_element_type=F32)
        hid = (jax.nn.silu(gate) * up).astype(BF16)
        acc_ref[rows, :] += jnp.dot(hid, wd, preferred_element_type=F32)

    @pl.when(f == pl.num_programs(1) - 1)
    def _():
        for m in range(cap // mc):
            rows = slice(m * mc, (m + 1) * mc)
            gate = jnp.concatenate([gate_ref[0, rows, :]] * (D_MODEL // LANES), axis=1)
            ye_ref[0, rows, :] = (acc_ref[rows, :] * gate).astype(BF16)


def _ffn(xe, gate, w_gate, w_up, w_down, *, cap, tf, mc):
    e_n = xe.shape[0]
    return pl.pallas_call(
        functools.partial(_ffn_kernel, mc=mc),
        grid=(e_n, D_EXPERT // tf),
        in_specs=[pl.BlockSpec((1, cap, D_MODEL), lambda e, f: (e, 0, 0)),
                  pl.BlockSpec((1, cap, LANES), lambda e, f: (e, 0, 0)),
                  pl.BlockSpec((1, D_MODEL, tf), lambda e, f: (e, 0, f)),
                  pl.BlockSpec((1, D_MODEL, tf), lambda e, f: (e, 0, f)),
                  pl.BlockSpec((1, tf, D_MODEL), lambda e, f: (e, f, 0))],
        out_specs=pl.BlockSpec((1, cap, D_MODEL), lambda e, f: (e, 0, 0)),
        out_shape=jax.ShapeDtypeStruct((e_n, cap, D_MODEL), BF16),
        scratch_shapes=[pltpu.VMEM((cap, D_MODEL), F32)],
        compiler_params=_params(("arbitrary", "arbitrary")),
        name="ffn",
    )(xe, gate, w_gate, w_up, w_down)


def _combine_kernel(off_ref, x1_ref, slot_ref, mod_ref, g_ref, ye_ref, o_ref,
                    wbuf_ref, buf_ref, wsem_ref, sem_ref, *, win, fast_max):
    t = pl.program_id(0)
    tt = x1_ref.shape[0]
    cap = ye_ref.shape[1]
    nsub = tt // LANES
    big = tt + BF16_SUBLANES
    experts = list(range(N_EXPERTS))

    def finish(moe):
        gate2 = mod_ref[0:1, 5 * D_MODEL:6 * D_MODEL]
        x2 = x1_ref[...] + gate2 * moe
        ms = jnp.mean(x2 * x2, axis=-1, keepdims=True)
        o_ref[...] = x2 * lax.rsqrt(ms + NORM_EPS) * g_ref[...]

    def clamped_start(e, rows, tile=t):
        return pl.multiple_of(jnp.minimum(_window_start(off_ref[e, tile * nsub]), cap - rows),
                              BF16_SUBLANES)

    def is_fast(tile):
        return _tile_max_count(off_ref, experts, tile, nsub) <= fast_max

    def window_copies(tile):
        buf = tile % 2
        return [pltpu.make_async_copy(ye_ref.at[e, pl.ds(clamped_start(e, LANES, tile), LANES), :],
                                      wbuf_ref.at[buf, pl.ds(e * LANES, LANES), :], wsem_ref.at[buf])
                for e in experts]

    def start_windows(tile):
        for cp in window_copies(tile):
            cp.start()

    fast = is_fast(t)
    pl.when(jnp.logical_and(t == 0, fast))(lambda: start_windows(t))
    nxt = jnp.minimum(t + 1, pl.num_programs(0) - 1)
    pl.when(jnp.logical_and(t + 1 < pl.num_programs(0), is_fast(nxt)))(lambda: start_windows(nxt))

    @pl.when(fast)
    def _():
        cols = lax.broadcasted_iota(jnp.int32, (tt, LANES), 1)
        onehot = jnp.concatenate(
            [jnp.where(cols == (slot_ref[:, e:e + 1] - clamped_start(e, LANES)), 1.0, 0.0).astype(BF16)
             for e in experts], axis=1)
        for cp in window_copies(t):
            cp.wait()
        finish(jnp.dot(onehot, wbuf_ref[t % 2], preferred_element_type=F32))

    @pl.when(jnp.logical_not(fast))
    def _():
        def copy(e, slot):
            return pltpu.make_async_copy(ye_ref.at[e, pl.ds(clamped_start(e, big), big), :],
                                         buf_ref.at[slot], sem_ref.at[slot])

        copy(0, 0).start()
        cols = lax.broadcasted_iota(jnp.int32, (LANES, win), 1)
        accs = [jnp.zeros((LANES, D_MODEL), F32) for _ in range(nsub)]
        for e in experts:
            slot = e % 2
            if e + 1 < N_EXPERTS:
                copy(e + 1, 1 - slot).start()
            copy(e, slot).wait()
            wb = clamped_start(e, big)
            for j in range(nsub):
                ws = jnp.minimum(_window_start(off_ref[e, t * nsub + j]), wb + (big - win))
                rel = slot_ref[j * LANES:(j + 1) * LANES, e:e + 1] - ws
                onehot = jnp.where(cols == rel, 1.0, 0.0).astype(BF16)
                r0 = pl.multiple_of(ws - wb, BF16_SUBLANES)
                accs[j] = accs[j] + jnp.dot(onehot, buf_ref[slot, pl.ds(r0, win), :],
                                            preferred_element_type=F32)
        finish(jnp.concatenate(accs, axis=0))


def _combine(off, x1, slot_t, mod, g, ye, *, tt):
    n = x1.shape[0]
    win = min(2 * LANES, tt + BF16_SUBLANES)
    grid_spec = pltpu.PrefetchScalarGridSpec(
        num_scalar_prefetch=1,
        grid=(n // tt,),
        in_specs=[pl.BlockSpec((tt, D_MODEL), lambda t, off: (t, 0)),
                  pl.BlockSpec((tt, N_EXPERTS), lambda t, off: (t, 0)),
                  pl.BlockSpec(mod.shape, lambda t, off: (0, 0)),
                  pl.BlockSpec(g.shape, lambda t, off: (0, 0)),
                  pl.BlockSpec(memory_space=pl.ANY)],
        out_specs=pl.BlockSpec((tt, D_MODEL), lambda t, off: (t, 0)),
        scratch_shapes=[pltpu.VMEM((2, N_EXPERTS * LANES, D_MODEL), BF16),
                        pltpu.VMEM((2, tt + BF16_SUBLANES, D_MODEL), BF16),
                        pltpu.SemaphoreType.DMA((2,)),
                        pltpu.SemaphoreType.DMA((2,))])
    return pl.pallas_call(
        functools.partial(_combine_kernel, win=win, fast_max=COMBINE_FAST_MAX),
        grid_spec=grid_spec,
        out_shape=jax.ShapeDtypeStruct((n, D_MODEL), F32),
        compiler_params=_params(("arbitrary",)),
        name="combine",
    )(off, x1, slot_t, mod, g, ye)


def _rope_tables(n):
    rows = n // GRID_W
    half = HEAD_DIM // 4
    freqs = ROPE_BASE ** (-jnp.arange(half, dtype=F32) / half)
    ang_r = jnp.arange(rows, dtype=jnp.int32).astype(F32)[:, None] * freqs[None, :]
    ang_c = jnp.arange(GRID_W, dtype=jnp.int32).astype(F32)[:, None] * freqs[None, :]
    zr = jnp.zeros((rows, 2 * half), F32)
    zc = jnp.zeros((GRID_W, 2 * half), F32)
    head = lambda a, b: jnp.concatenate([a, b] * 2, axis=1)
    rcos = head(jnp.concatenate([jnp.cos(ang_r)] * 2, axis=1), zr)
    rsin = head(jnp.concatenate([-jnp.sin(ang_r), jnp.sin(ang_r)], axis=1), zr)
    ccos = head(zc, jnp.concatenate([jnp.cos(ang_c)] * 2, axis=1))
    csin = head(zc, jnp.concatenate([-jnp.sin(ang_c), jnp.sin(ang_c)], axis=1))
    return rcos, rsin, ccos, csin


def _block_diag(blocks):
    k = len(blocks)
    r, c = blocks[0].shape
    out = jnp.zeros((k * r, k * c), blocks[0].dtype)
    for i, b in enumerate(blocks):
        out = out.at[i * r:(i + 1) * r, i * c:(i + 1) * c].set(b)
    return out


def _dft_tables(n, cb_n2):
    nb = LANES
    na = n // nb
    gd = FOURIER_GROUP_DIM
    scale = 1.0 / math.sqrt(n * gd)

    def cos_sin(i, j, period):
        ang = ((i * j) % period).astype(F32) * (2.0 * math.pi / period)
        return jnp.cos(ang), jnp.sin(ang)

    ar_g = jnp.arange(gd, dtype=jnp.int32)
    c64, s64 = cos_sin(ar_g[:, None], ar_g[None, :], gd)
    w0 = jnp.concatenate([_block_diag([c64 * scale] * N_FOURIER_GROUPS),
                          _block_diag([-s64 * scale] * N_FOURIER_GROUPS)], axis=1)
    ar_a = jnp.arange(na, dtype=jnp.int32)
    ca, sa = cos_sin(ar_a[:, None], ar_a[None, :], na)
    wa = jnp.concatenate([jnp.concatenate([ca, sa], axis=1),
                          jnp.concatenate([-sa, ca], axis=1)], axis=0)
    ar_b = jnp.arange(nb, dtype=jnp.int32)
    ct, st = cos_sin(ar_a[:, None], ar_b[None, :], n)
    group = lambda t: t.reshape(na, nb // cb_n2, cb_n2).transpose(1, 0, 2)
    cc, sc = cos_sin(ar_b[:, None], ar_b[None, :], nb)
    mm = jnp.concatenate([cc, sc], axis=1)
    return w0.astype(BF16), wa.astype(BF16), group(ct), group(st), mm.astype(BF16)


def _prefix_tables(r_n):
    i = np.arange(LANES)
    ut = (i[:, None] < i[None, :]).astype(np.float32)
    ones = np.ones((LANES, LANES), np.float32)
    r = np.arange(r_n)
    slt = (r[None, :] < r[:, None]).astype(np.float32)
    as_bf16 = lambda a: jnp.asarray(a).astype(BF16)
    return as_bf16(ut), as_bf16(ones), as_bf16(slt)


def kernel(x, c, ctx, c_ctx, w_mod, b_mod, norm_mix_g, norm_ffn_g, w_in, w_four, q_norm_g, k_norm_g,
           w_out, w_router, w_gate, w_up, w_down, final_norm_g):
    batch, n, _ = x.shape
    assert batch == 1 and w_mod.shape[0] == 1
    lc = ctx.shape[1]
    cap = EC_CAPACITY_FACTOR * n // N_EXPERTS
    x2d = x[0]
    ctx2d = ctx[0]

    cc = jnp.zeros((8, D_MODEL), F32).at[0].set(c[0]).at[1].set(c_ctx)
    mod = _mod(cc, w_mod[0], b_mod[0][None, :])

    rope_x = _rope_tables(n)
    zero_c = jnp.zeros((GRID_W, LANES), F32)
    rope_c = (jnp.ones((lc // GRID_W, LANES), F32), jnp.zeros((lc // GRID_W, LANES), F32), zero_c, zero_c)
    inv_hd = jnp.full((HEAD_DIM, HEAD_DIM), 1.0 / HEAD_DIM, BF16)
    bq = _block_diag([inv_hd] * N_Q_HEADS)
    bk = _block_diag([inv_hd] * N_KV_HEADS)
    q_scale = (HEAD_DIM ** -0.5) * math.log2(math.e)
    gq = jnp.tile(q_norm_g[0], N_Q_HEADS)[None, :] * q_scale
    gk = jnp.tile(k_norm_g[0], N_KV_HEADS)[None, :]
    nb = LANES
    na = n // nb
    dft_cb = min(4096, nb * FOURIER_WIDTH)
    w0, wa, tw_c, tw_s, mm = _dft_tables(n, dft_cb // FOURIER_WIDTH)
    w_in_b = w_in[0].astype(BF16)
    g_mix = norm_mix_g[0][None, :]
    tm = min(512, n)
    zr, zi, q_t, k_l, v_t = _inproj(x2d, mod, g_mix, w_in_b, rope_x, bq, bk, gq, gk, w0,
                                    mod_row=0, with_q=True, tm=tm)
    k_c, v_c = _inproj(ctx2d, mod, g_mix, w_in_b, rope_c, bq, bk, gq, gk, w0,
                       mod_row=1, with_q=False, tm=lc)

    attn = _attention(q_t, k_l, v_t, k_c, v_c, tq=min(256, n), tk=min(1024, n))

    br, bi = _dft_a(zr.reshape(na, nb * FOURIER_WIDTH), zi.reshape(na, nb * FOURIER_WIDTH), wa,
                    tw_c, tw_s, cb=dft_cb)
    wf = _block_diag([w_four[0, g] for g in range(N_FOURIER_GROUPS)]).astype(BF16)
    four = _dft_c(br.reshape(na, nb, FOURIER_WIDTH), bi.reshape(na, nb, FOURIER_WIDTH), mm, wf,
                  kb=8).reshape(n, FOURIER_WIDTH)

    w_out_b = w_out[0].astype(BF16)
    wo_f = w_out_b[:FOURIER_WIDTH]
    wo_a = w_out_b[FOURIER_WIDTH:].reshape(N_KV_HEADS, Q_PER_KV * HEAD_DIM, D_MODEL)
    wo_a = jnp.pad(wo_a, ((0, 0), (0, GROUP_PAD - Q_PER_KV * HEAD_DIM), (0, 0)))
    wr_t = w_router[0].T.astype(BF16)
    x1, h2, aff = _outproj(four, attn, x2d, mod, norm_ffn_g[0][None, :], wo_f, wo_a, wr_t, tm=tm)

    r_n = n // LANES
    ut, ones, slt = _prefix_tables(r_n)
    slot3, off3 = _topk(aff.reshape(N_EXPERTS, r_n, LANES), ut, ones, slt, cap=cap)
    off = jnp.concatenate([off3[:, :, 0], jnp.full((N_EXPERTS, 1), cap, jnp.int32)], axis=1)
    slot = slot3.reshape(N_EXPERTS, n)

    tt = (min(512, cap - BF16_SUBLANES) // LANES) * LANES
    xe, gate = _gather(off, slot.reshape(N_EXPERTS, 1, n), aff.reshape(N_EXPERTS, 1, n), h2,
                       cap=cap, tt=tt, eg=GATHER_EXPERTS)
    ye = _ffn(xe, gate, w_gate[0], w_up[0], w_down[0], cap=cap, tf=512, mc=min(1024, cap))
    out = _combine(off, x1, slot.T, mod, final_norm_g[None, :], ye, tt=tt)
    return out[None]
```

```python
import functools
import math

import numpy as np
import jax
import jax.numpy as jnp
from jax import lax
from jax.experimental import pallas as pl
from jax.experimental.pallas import tpu as pltpu

D_MODEL = 1024
GRID_W = 64
HEAD_DIM = 64
N_Q_HEADS = 12
N_KV_HEADS = 4
Q_PER_KV = N_Q_HEADS // N_KV_HEADS
ATTN_WIDTH = N_Q_HEADS * HEAD_DIM
KV_WIDTH = N_KV_HEADS * HEAD_DIM
N_FOURIER_GROUPS = 4
FOURIER_GROUP_DIM = 64
FOURIER_WIDTH = N_FOURIER_GROUPS * FOURIER_GROUP_DIM
IN_WIDTH = FOURIER_WIDTH + ATTN_WIDTH + 2 * KV_WIDTH
ROPE_BASE = 10000.0
N_EXPERTS = 16
EC_CAPACITY_FACTOR = 2
D_EXPERT = 2048
N_MOD = 6
NORM_EPS = 1e-6

LANES = 128
BF16_SUBLANES = 16
GROUP_PAD = 2 * LANES
V_ROWS = HEAD_DIM + BF16_SUBLANES
GATHER_WIN = LANES + BF16_SUBLANES
GATHER_FAST_MAX = LANES
COMBINE_FAST_MAX = LANES - BF16_SUBLANES
GATHER_EXPERTS = 4
VMEM_LIMIT = 56 * 1024 * 1024

F32 = jnp.float32
BF16 = jnp.bfloat16


def _params(sem):
    return pltpu.CompilerParams(dimension_semantics=sem, vmem_limit_bytes=VMEM_LIMIT)


def _mod_kernel(c_ref, w_ref, b_ref, o_ref):
    s = jax.nn.silu(c_ref[...])
    o_ref[...] = jnp.dot(s, w_ref[...], precision=lax.Precision.HIGHEST,
                         preferred_element_type=F32) + b_ref[...]


def _mod(cc, w_mod, b_mod):
    tn = 1024
    n = w_mod.shape[1]
    return pl.pallas_call(
        _mod_kernel,
        grid=(n // tn,),
        in_specs=[pl.BlockSpec((8, D_MODEL), lambda j: (0, 0)),
                  pl.BlockSpec((D_MODEL, tn), lambda j: (0, j)),
                  pl.BlockSpec((1, tn), lambda j: (0, j))],
        out_specs=pl.BlockSpec((8, tn), lambda j: (0, j)),
        out_shape=jax.ShapeDtypeStruct((8, n), F32),
        compiler_params=_params(("arbitrary",)),
        name="mod",
    )(cc, w_mod, b_mod)


def _rms_modulate(x, g, shift, scale):
    ms = jnp.mean(x * x, axis=-1, keepdims=True)
    y = x * lax.rsqrt(ms + NORM_EPS) * g
    return y * (1.0 + scale) + shift


def _head_norm(t, bmat, gain):
    ms = jnp.dot((t * t).astype(BF16), bmat, preferred_element_type=F32)
    return t * lax.rsqrt(ms + NORM_EPS) * gain


def _rope(t, cos, sin, lo_mask):
    outs = []
    for c in range(t.shape[1] // LANES):
        xc = t[:, c * LANES:(c + 1) * LANES]
        up = pltpu.roll(xc, LANES - 16, axis=1)
        dn = pltpu.roll(xc, 16, axis=1)
        rot = jnp.where(lo_mask, up, dn)
        outs.append(xc * cos + rot * sin)
    return jnp.concatenate(outs, axis=1)


def _inproj_kernel(x_ref, mod_ref, g_ref, w_ref, rcos_ref, rsin_ref, ccos_ref, csin_ref, bq_ref, bk_ref,
                   gq_ref, gk_ref, w0_ref, *out_refs, mod_row, with_q):
    if with_q:
        zr_ref, zi_ref, q_ref, k_ref, v_ref = out_refs
    else:
        k_ref, v_ref = out_refs
    tm = x_ref.shape[0]
    shift = mod_ref[mod_row:mod_row + 1, 0:D_MODEL]
    scale = mod_ref[mod_row:mod_row + 1, D_MODEL:2 * D_MODEL]
    h = _rms_modulate(x_ref[...], g_ref[...], shift, scale).astype(BF16)
    cos = jnp.concatenate([rcos_ref[r:r + 1, :] + ccos_ref[...] for r in range(tm // GRID_W)], axis=0)
    sin = jnp.concatenate([rsin_ref[r:r + 1, :] + csin_ref[...] for r in range(tm // GRID_W)], axis=0)
    lane = lax.broadcasted_iota(jnp.int32, (tm, LANES), 1)
    lo_mask = (lane % 32) < 16
    k0 = FOURIER_WIDTH + ATTN_WIDTH
    if with_q:
        p = jnp.dot(h, w_ref[...], preferred_element_type=F32)
        f = p[:, :FOURIER_WIDTH].astype(BF16)
        z = jnp.dot(f, w0_ref[...], preferred_element_type=F32)
        zr_ref[...] = z[:, :FOURIER_WIDTH].astype(BF16)
        zi_ref[...] = z[:, FOURIER_WIDTH:].astype(BF16)
        q = _head_norm(p[:, FOURIER_WIDTH:k0], bq_ref[...], gq_ref[...])
        q = _rope(q, cos, sin, lo_mask)
        q_ref[...] = q.T.astype(BF16).reshape(N_Q_HEADS, HEAD_DIM, tm)
        pk = p[:, k0:k0 + KV_WIDTH]
        pv = p[:, k0 + KV_WIDTH:]
    else:
        pkv = jnp.dot(h, w_ref[:, k0:], preferred_element_type=F32)
        pk = pkv[:, :KV_WIDTH]
        pv = pkv[:, KV_WIDTH:]
    k = _head_norm(pk, bk_ref[...], gk_ref[...])
    k = _rope(k, cos, sin, lo_mask)
    for g in range(N_KV_HEADS):
        k_ref[g] = k[:, g * HEAD_DIM:(g + 1) * HEAD_DIM].astype(BF16)
    v_ref[:, 0:HEAD_DIM, :] = pv.T.astype(BF16).reshape(N_KV_HEADS, HEAD_DIM, tm)
    v_ref[:, HEAD_DIM:V_ROWS, :] = jnp.ones((N_KV_HEADS, V_ROWS - HEAD_DIM, tm), BF16)


def _inproj(x, mod, g, w_in, rope, bq, bk, gq, gk, w0, *, mod_row, with_q, tm):
    n = x.shape[0]
    rcos, rsin, ccos, csin = rope
    const = lambda shape: pl.BlockSpec(shape, lambda i: (0,) * len(shape))
    in_specs = [pl.BlockSpec((tm, D_MODEL), lambda i: (i, 0)),
                const(mod.shape), const(g.shape), const(w_in.shape),
                pl.BlockSpec((tm // GRID_W, LANES), lambda i: (i, 0)),
                pl.BlockSpec((tm // GRID_W, LANES), lambda i: (i, 0)),
                const(ccos.shape), const(csin.shape),
                const(bq.shape), const(bk.shape), const(gq.shape), const(gk.shape),
                const(w0.shape)]
    kv_specs = [pl.BlockSpec((N_KV_HEADS, tm, HEAD_DIM), lambda i: (0, i, 0)),
                pl.BlockSpec((N_KV_HEADS, V_ROWS, tm), lambda i: (0, 0, i))]
    kv_shapes = [jax.ShapeDtypeStruct((N_KV_HEADS, n, HEAD_DIM), BF16),
                 jax.ShapeDtypeStruct((N_KV_HEADS, V_ROWS, n), BF16)]
    if with_q:
        out_specs = [pl.BlockSpec((tm, FOURIER_WIDTH), lambda i: (i, 0)),
                     pl.BlockSpec((tm, FOURIER_WIDTH), lambda i: (i, 0)),
                     pl.BlockSpec((N_Q_HEADS, HEAD_DIM, tm), lambda i: (0, 0, i))] + kv_specs
        out_shapes = [jax.ShapeDtypeStruct((n, FOURIER_WIDTH), BF16),
                      jax.ShapeDtypeStruct((n, FOURIER_WIDTH), BF16),
                      jax.ShapeDtypeStruct((N_Q_HEADS, HEAD_DIM, n), BF16)] + kv_shapes
    else:
        out_specs, out_shapes = kv_specs, kv_shapes
    return pl.pallas_call(
        functools.partial(_inproj_kernel, mod_row=mod_row, with_q=with_q),
        grid=(n // tm,),
        in_specs=in_specs,
        out_specs=out_specs,
        out_shape=out_shapes,
        compiler_params=_params(("arbitrary",)),
        name="inproj_x" if with_q else "inproj_ctx",
    )(x, mod, g, w_in, rcos, rsin, ccos, csin, bq, bk, gq, gk, w0)


def _attn_kernel(q_ref, k_ref, v_ref, kc_ref, vc_ref, o_ref, s_ref, acc_ref, *, tk):
    tq = q_ref.shape[2]
    n = k_ref.shape[1]
    lc = kc_ref.shape[1]
    nch = n // tk
    qs = jnp.concatenate([q_ref[j] for j in range(Q_PER_KV)], axis=1)

    def scores_latent(c, slot):
        start = pl.multiple_of(c * tk, tk)
        s = jnp.dot(k_ref[0, pl.ds(start, tk), :], qs, preferred_element_type=F32)
        s_ref[slot] = s
        return jnp.max(s, axis=0, keepdims=True)

    def scores_context():
        s = jnp.dot(kc_ref[0], qs, preferred_element_type=F32)
        s_ref[0, 0:lc, :] = s
        return jnp.max(s, axis=0, keepdims=True)

    def v_latent(c):
        return v_ref[0, :, pl.ds(pl.multiple_of(c * tk, tk), tk)]

    def consume(slot, rows, vb, m, mx):
        s = s_ref[slot, 0:rows, :]
        m_new = jnp.maximum(m, mx)
        alpha = jnp.exp2(m - m_new)
        p = jnp.exp2(s - m_new).astype(BF16)
        acc_ref[...] = alpha * acc_ref[...] + jnp.dot(vb, p, preferred_element_type=F32)
        return m_new

    def pair(c0, carry, prefetch):
        m, mx0 = carry
        mx1 = scores_latent(c0 + 1, 1)
        m = consume(0, tk, v_latent(c0), m, mx0)
        mx0 = prefetch()
        return consume(1, tk, v_latent(c0 + 1), m, mx1), mx0

    def body(i, carry):
        return pair(2 * i, carry, lambda: scores_latent(2 * i + 2, 0))

    w = Q_PER_KV * tq
    acc_ref[...] = jnp.zeros_like(acc_ref)
    mx0 = scores_latent(0, 0)
    carry = lax.fori_loop(0, nch // 2 - 1, body, (jnp.full((1, w), -1e30, F32), mx0))
    m, mx_ctx = pair(nch - 2, carry, scores_context)
    consume(0, lc, vc_ref[0], m, mx_ctx)
    acc = acc_ref[...]
    o = acc[0:HEAD_DIM] / acc[HEAD_DIM:HEAD_DIM + 1]
    ot = jnp.concatenate([o, jnp.zeros_like(o)], axis=0).T
    h0, h1, h2 = (ot[j * tq:(j + 1) * tq] for j in range(Q_PER_KV))
    o_ref[0] = jnp.concatenate([h0 + pltpu.roll(h1, HEAD_DIM, axis=1), h2], axis=1).astype(BF16)


def _attention(q, k, v, kc, vc, *, tq, tk):
    n = k.shape[1]
    lc = kc.shape[1]
    kd = k.shape[2]
    gw = GROUP_PAD
    return pl.pallas_call(
        functools.partial(_attn_kernel, tk=tk),
        grid=(N_KV_HEADS, n // tq),
        in_specs=[pl.BlockSpec((Q_PER_KV, kd, tq), lambda g, i: (g, 0, i)),
                  pl.BlockSpec((1, n, kd), lambda g, i: (g, 0, 0)),
                  pl.BlockSpec((1, V_ROWS, n), lambda g, i: (g, 0, 0)),
                  pl.BlockSpec((1, lc, kd), lambda g, i: (g, 0, 0)),
                  pl.BlockSpec((1, V_ROWS, lc), lambda g, i: (g, 0, 0))],
        out_specs=pl.BlockSpec((1, tq, gw), lambda g, i: (g, i, 0)),
        out_shape=jax.ShapeDtypeStruct((N_KV_HEADS, n, gw), BF16),
        scratch_shapes=[pltpu.VMEM((2, tk, Q_PER_KV * tq), F32),
                        pltpu.VMEM((V_ROWS, Q_PER_KV * tq), F32)],
        compiler_params=_params(("arbitrary", "arbitrary")),
        name="attention",
    )(q, k, v, kc, vc)


def _dft_a_kernel(zr_ref, zi_ref, w_ref, tc_ref, ts_ref, br_ref, bi_ref):
    na = zr_ref.shape[0]
    fw = FOURIER_WIDTH
    zz = jnp.concatenate([zr_ref[...], zi_ref[...]], axis=0)
    a = jnp.dot(w_ref[...], zz, preferred_element_type=F32)
    tc = tc_ref[0]
    ts = ts_ref[0]
    for j in range(tc.shape[1]):
        ar = a[:na, j * fw:(j + 1) * fw]
        ai = a[na:, j * fw:(j + 1) * fw]
        c = tc[:, j:j + 1]
        s = ts[:, j:j + 1]
        br_ref[:, j * fw:(j + 1) * fw] = (ar * c + ai * s).astype(BF16)
        bi_ref[:, j * fw:(j + 1) * fw] = (ai * c - ar * s).astype(BF16)


def _dft_a(zr2, zi2, wa, tc3, ts3, *, cb):
    na, cols = zr2.shape
    blk = pl.BlockSpec((na, cb), lambda j: (0, j))
    tw = pl.BlockSpec((1,) + tc3.shape[1:], lambda j: (j, 0, 0))
    return pl.pallas_call(
        _dft_a_kernel,
        grid=(cols // cb,),
        in_specs=[blk, blk, pl.BlockSpec(wa.shape, lambda j: (0, 0)), tw, tw],
        out_specs=[blk, blk],
        out_shape=[jax.ShapeDtypeStruct((na, cols), BF16)] * 2,
        compiler_params=_params(("arbitrary",)),
        name="dft_a",
    )(zr2, zi2, wa, tc3, ts3)


def _dft_c_kernel(br_ref, bi_ref, m_ref, wf_ref, o_ref):
    kb = br_ref.shape[0]
    outs = []
    for j in range(kb):
        ab = jnp.concatenate([br_ref[j], bi_ref[j]], axis=0)
        y = jnp.dot(m_ref[...], ab, preferred_element_type=F32)
        outs.append(jnp.dot(y.astype(BF16), wf_ref[...], preferred_element_type=F32))
    o_ref[...] = jnp.concatenate(outs, axis=1).astype(BF16)


def _dft_c(br3, bi3, mm, wf, *, kb):
    na, nb, fw = br3.shape
    blk = pl.BlockSpec((kb, nb, fw), lambda i: (i, 0, 0))
    return pl.pallas_call(
        _dft_c_kernel,
        grid=(na // kb,),
        in_specs=[blk, blk,
                  pl.BlockSpec(mm.shape, lambda i: (0, 0)),
                  pl.BlockSpec(wf.shape, lambda i: (0, 0))],
        out_specs=pl.BlockSpec((nb, kb * fw), lambda i: (0, i)),
        out_shape=jax.ShapeDtypeStruct((nb, na * fw), BF16),
        compiler_params=_params(("arbitrary",)),
        name="dft_c",
    )(br3, bi3, mm, wf)


def _outproj_kernel(four_ref, attn_ref, x_ref, mod_ref, g_ref, wf_ref, wa_ref, wr_ref,
                    x1_ref, h_ref, aff_ref):
    mix = jnp.dot(four_ref[...], wf_ref[...], preferred_element_type=F32)
    for g in range(N_KV_HEADS):
        mix = mix + jnp.dot(attn_ref[g], wa_ref[g], preferred_element_type=F32)
    gate1 = mod_ref[0:1, 2 * D_MODEL:3 * D_MODEL]
    shift2 = mod_ref[0:1, 3 * D_MODEL:4 * D_MODEL]
    scale2 = mod_ref[0:1, 4 * D_MODEL:5 * D_MODEL]
    x1 = x_ref[...] + gate1 * mix
    x1_ref[...] = x1
    h = _rms_modulate(x1, g_ref[...], shift2, scale2).astype(BF16)
    h_ref[...] = h
    logits = lax.dot_general(wr_ref[...], h, (((1,), (1,)), ((), ())),
                             preferred_element_type=F32)
    mx = jnp.max(logits, axis=0, keepdims=True)
    e = jnp.exp(logits - mx)
    aff_ref[...] = e / jnp.sum(e, axis=0, keepdims=True)


def _outproj(four, attn, x, mod, g, wo_f, wo_a, wr_t, *, tm):
    n = x.shape[0]
    gw = GROUP_PAD
    const = lambda shape: pl.BlockSpec(shape, lambda i: (0,) * len(shape))
    return pl.pallas_call(
        _outproj_kernel,
        grid=(n // tm,),
        in_specs=[pl.BlockSpec((tm, FOURIER_WIDTH), lambda i: (i, 0)),
                  pl.BlockSpec((N_KV_HEADS, tm, gw), lambda i: (0, i, 0)),
                  pl.BlockSpec((tm, D_MODEL), lambda i: (i, 0)),
                  const(mod.shape), const(g.shape), const(wo_f.shape), const(wo_a.shape),
                  const(wr_t.shape)],
        out_specs=[pl.BlockSpec((tm, D_MODEL), lambda i: (i, 0)),
                   pl.BlockSpec((tm, D_MODEL), lambda i: (i, 0)),
                   pl.BlockSpec((N_EXPERTS, tm), lambda i: (0, i))],
        out_shape=[jax.ShapeDtypeStruct((n, D_MODEL), F32),
                   jax.ShapeDtypeStruct((n, D_MODEL), BF16),
                   jax.ShapeDtypeStruct((N_EXPERTS, n), F32)],
        compiler_params=_params(("arbitrary",)),
        name="outproj",
    )(four, attn, x, mod, g, wo_f, wo_a, wr_t)


def _count(m):
    return jnp.sum(jnp.sum(m, axis=2, keepdims=True), axis=1, keepdims=True)


def _topk_kernel(aff_ref, ut_ref, ones_ref, slt_ref, slot_ref, off_ref, *, cap):
    e_n, r_n, _ = aff_ref.shape
    aff = aff_ref[...]
    capf = jnp.float32(cap)

    def count_ge(th):
        return _count(jnp.where(aff >= th, 1.0, 0.0))

    def search(i, t):
        cand = t | jnp.left_shift(jnp.int32(1), 30 - i)
        c = count_ge(lax.bitcast_convert_type(cand, F32))
        return jnp.where(c >= capf, cand, t)

    tbits = lax.fori_loop(0, 31, search, jnp.zeros((e_n, 1, 1), jnp.int32))
    thr = lax.bitcast_convert_type(tbits, F32)
    ulp = lax.bitcast_convert_type(tbits + 1, F32) - thr

    def refine(_, carry):
        t, step = carry
        step = step * 0.5
        cand = t + step
        return jnp.where(count_ge(cand) >= capf, cand, t), step

    thr, _ = lax.fori_loop(0, 29, refine, (thr, ulp))
    gt = jnp.where(aff > thr, 1.0, 0.0)
    eq = jnp.where(aff == thr, 1.0, 0.0)
    need = capf - _count(gt)

    def excl_prefix(m):
        m2 = m.reshape(e_n * r_n, LANES).astype(BF16)
        within = jnp.dot(m2, ut_ref[...], preferred_element_type=F32).reshape(e_n, r_n, LANES)
        tot = jnp.dot(m2, ones_ref[...], preferred_element_type=F32).reshape(e_n, r_n, LANES)
        rows = jnp.stack([jnp.dot(slt_ref[...], tot[e].astype(BF16), preferred_element_type=F32)
                          for e in range(e_n)], axis=0)
        return within + rows, rows

    pe, _ = excl_prefix(eq)
    sel = jnp.maximum(gt, eq * jnp.where(pe < need, 1.0, 0.0))
    pos, rows = excl_prefix(sel)
    slot_ref[...] = jnp.where(sel > 0.5, pos, -1.0).astype(jnp.int32)
    off_ref[...] = rows.astype(jnp.int32)


def _topk(aff3, ut, ones, slt, *, cap):
    shp = aff3.shape
    full = lambda a: pl.BlockSpec(a.shape, lambda i: (0,) * a.ndim)
    return pl.pallas_call(
        functools.partial(_topk_kernel, cap=cap),
        grid=(1,),
        in_specs=[full(aff3), full(ut), full(ones), full(slt)],
        out_specs=[pl.BlockSpec(shp, lambda i: (0, 0, 0))] * 2,
        out_shape=[jax.ShapeDtypeStruct(shp, jnp.int32)] * 2,
        compiler_params=_params(("arbitrary",)),
        name="topk",
    )(aff3, ut, ones, slt)


def _window_start(off):
    return pl.multiple_of((off // BF16_SUBLANES) * BF16_SUBLANES, BF16_SUBLANES)


def _tile_max_count(off_ref, experts, t, nsub):
    cnts = [off_ref[e, (t + 1) * nsub] - off_ref[e, t * nsub] for e in experts]
    return functools.reduce(jnp.maximum, cnts)


def _gather_kernel(off_ref, slot_ref, aff_ref, h_ref, xe_ref, gate_ref, *, fast_max):
    p = pl.program_id(0)
    t = pl.program_id(1)
    eg, _, tt = slot_ref.shape
    nsub = tt // LANES
    experts = [p * eg + j for j in range(eg)]

    @pl.when(t == 0)
    def _():
        xe_ref[...] = jnp.zeros_like(xe_ref)
        gate_ref[...] = jnp.zeros_like(gate_ref)

    def scatter_rows(j, ws, hit, res, aff_row):
        xe_ref[j, pl.ds(ws, GATHER_WIN), :] += res.astype(BF16)
        g = jnp.sum(jnp.where(hit, aff_row, 0.0), axis=1, keepdims=True)
        gate_ref[j, pl.ds(ws, GATHER_WIN), :] += jnp.broadcast_to(g, (GATHER_WIN, LANES))

    fast = _tile_max_count(off_ref, experts, t, nsub) <= fast_max

    @pl.when(fast)
    def _():
        rows = lax.broadcasted_iota(jnp.int32, (GATHER_WIN, tt), 0)
        starts = [_window_start(off_ref[e, t * nsub]) for e in experts]
        hits = [rows == (slot_ref[j] - starts[j]) for j in range(eg)]
        onehot = jnp.concatenate([jnp.where(hit, 1.0, 0.0).astype(BF16) for hit in hits], axis=0)
        res = jnp.dot(onehot, h_ref[...], preferred_element_type=F32)
        for j in range(eg):
            scatter_rows(j, starts[j], hits[j], res[j * GATHER_WIN:(j + 1) * GATHER_WIN], aff_ref[j])

    @pl.when(jnp.logical_not(fast))
    def _():
        rows = lax.broadcasted_iota(jnp.int32, (GATHER_WIN, LANES), 0)
        for j in range(eg):
            for b in range(nsub):
                ws = _window_start(off_ref[experts[j], t * nsub + b])
                lanes = slice(b * LANES, (b + 1) * LANES)
                hit = rows == (slot_ref[j, :, lanes] - ws)
                res = jnp.dot(jnp.where(hit, 1.0, 0.0).astype(BF16), h_ref[lanes, :],
                              preferred_element_type=F32)
                scatter_rows(j, ws, hit, res, aff_ref[j, :, lanes])


def _gather(off, slot3, aff3, h, *, cap, tt, eg):
    n = h.shape[0]
    capp = cap + GATHER_WIN
    grid_spec = pltpu.PrefetchScalarGridSpec(
        num_scalar_prefetch=1,
        grid=(N_EXPERTS // eg, n // tt),
        in_specs=[pl.BlockSpec((eg, 1, tt), lambda p, t, off: (p, 0, t)),
                  pl.BlockSpec((eg, 1, tt), lambda p, t, off: (p, 0, t)),
                  pl.BlockSpec((tt, D_MODEL), lambda p, t, off: (t, 0))],
        out_specs=[pl.BlockSpec((eg, capp, D_MODEL), lambda p, t, off: (p, 0, 0)),
                   pl.BlockSpec((eg, capp, LANES), lambda p, t, off: (p, 0, 0))])
    return pl.pallas_call(
        functools.partial(_gather_kernel, fast_max=GATHER_FAST_MAX),
        grid_spec=grid_spec,
        out_shape=[jax.ShapeDtypeStruct((N_EXPERTS, capp, D_MODEL), BF16),
                   jax.ShapeDtypeStruct((N_EXPERTS, capp, LANES), F32)],
        compiler_params=_params(("arbitrary", "arbitrary")),
        name="gather",
    )(off, slot3, aff3, h)


def _ffn_kernel(xe_ref, gate_ref, wg_ref, wu_ref, wd_ref, ye_ref, acc_ref, *, mc):
    f = pl.program_id(1)
    cap = xe_ref.shape[1]
    wg = wg_ref[0].astype(BF16)
    wu = wu_ref[0].astype(BF16)
    wd = wd_ref[0].astype(BF16)

    @pl.when(f == 0)
    def _():
        acc_ref[...] = jnp.zeros_like(acc_ref)

    for m in range(cap // mc):
        rows = slice(m * mc, (m + 1) * mc)
        xm = xe_ref[0, rows, :]
        gate = jnp.dot(xm, wg, preferred_element_type=F32)
        up = jnp.dot(xm, wu, preferred_element_type=F32)
        hid = (jax.nn.silu(gate) * up).astype(BF16)
        acc_ref[rows, :] += jnp.dot(hid, wd, preferred_element_type=F32)

    @pl.when(f == pl.num_programs(1) - 1)
    def _():
        for m in range(cap // mc):
            rows = slice(m * mc, (m + 1) * mc)
            gate = jnp.concatenate([gate_ref[0, rows, :]] * (D_MODEL // LANES), axis=1)
            ye_ref[0, rows, :] = (acc_ref[rows, :] * gate).astype(BF16)


def _ffn(xe, gate, w_gate, w_up, w_down, *, cap, tf, mc):
    e_n = xe.shape[0]
    return pl.pallas_call(
        functools.partial(_ffn_kernel, mc=mc),
        grid=(e_n, D_EXPERT // tf),
        in_specs=[pl.BlockSpec((1, cap, D_MODEL), lambda e, f: (e, 0, 0)),
                  pl.BlockSpec((1, cap, LANES), lambda e, f: (e, 0, 0)),
                  pl.BlockSpec((1, D_MODEL, tf), lambda e, f: (e, 0, f)),
                  pl.BlockSpec((1, D_MODEL, tf), lambda e, f: (e, 0, f)),
                  pl.BlockSpec((1, tf, D_MODEL), lambda e, f: (e, f, 0))],
        out_specs=pl.BlockSpec((1, cap, D_MODEL), lambda e, f: (e, 0, 0)),
        out_shape=jax.ShapeDtypeStruct((e_n, cap, D_MODEL), BF16),
        scratch_shapes=[pltpu.VMEM((cap, D_MODEL), F32)],
        compiler_params=_params(("arbitrary", "arbitrary")),
        name="ffn",
    )(xe, gate, w_gate, w_up, w_down)


def _combine_kernel(off_ref, x1_ref, slot_ref, mod_ref, g_ref, ye_ref, o_ref,
                    wbuf_ref, buf_ref, wsem_ref, sem_ref, *, win, fast_max):
    t = pl.program_id(0)
    tt = x1_ref.shape[0]
    cap = ye_ref.shape[1]
    nsub = tt // LANES
    big = tt + BF16_SUBLANES
    experts = list(range(N_EXPERTS))

    def finish(moe):
        gate2 = mod_ref[0:1, 5 * D_MODEL:6 * D_MODEL]
        x2 = x1_ref[...] + gate2 * moe
        ms = jnp.mean(x2 * x2, axis=-1, keepdims=True)
        o_ref[...] = x2 * lax.rsqrt(ms + NORM_EPS) * g_ref[...]

    def clamped_start(e, rows, tile=t):
        return pl.multiple_of(jnp.minimum(_window_start(off_ref[e, tile * nsub]), cap - rows),
                              BF16_SUBLANES)

    def is_fast(tile):
        return _tile_max_count(off_ref, experts, tile, nsub) <= fast_max

    def window_copies(tile):
        buf = tile % 2
        return [pltpu.make_async_copy(ye_ref.at[e, pl.ds(clamped_start(e, LANES, tile), LANES), :],
                                      wbuf_ref.at[buf, pl.ds(e * LANES, LANES), :], wsem_ref.at[buf])
                for e in experts]

    def start_windows(tile):
        for cp in window_copies(tile):
            cp.start()

    fast = is_fast(t)
    pl.when(jnp.logical_and(t == 0, fast))(lambda: start_windows(t))
    nxt = jnp.minimum(t + 1, pl.num_programs(0) - 1)
    pl.when(jnp.logical_and(t + 1 < pl.num_programs(0), is_fast(nxt)))(lambda: start_windows(nxt))

    @pl.when(fast)
    def _():
        cols = lax.broadcasted_iota(jnp.int32, (tt, LANES), 1)
        onehot = jnp.concatenate(
            [jnp.where(cols == (slot_ref[:, e:e + 1] - clamped_start(e, LANES)), 1.0, 0.0).astype(BF16)
             for e in experts], axis=1)
        for cp in window_copies(t):
            cp.wait()
        finish(jnp.dot(onehot, wbuf_ref[t % 2], preferred_element_type=F32))

    @pl.when(jnp.logical_not(fast))
    def _():
        def copy(e, slot):
            return pltpu.make_async_copy(ye_ref.at[e, pl.ds(clamped_start(e, big), big), :],
                                         buf_ref.at[slot], sem_ref.at[slot])

        copy(0, 0).start()
        cols = lax.broadcasted_iota(jnp.int32, (LANES, win), 1)
        accs = [jnp.zeros((LANES, D_MODEL), F32) for _ in range(nsub)]
        for e in experts:
            slot = e % 2
            if e + 1 < N_EXPERTS:
                copy(e + 1, 1 - slot).start()
            copy(e, slot).wait()
            wb = clamped_start(e, big)
            for j in range(nsub):
                ws = jnp.minimum(_window_start(off_ref[e, t * nsub + j]), wb + (big - win))
                rel = slot_ref[j * LANES:(j + 1) * LANES, e:e + 1] - ws
                onehot = jnp.where(cols == rel, 1.0, 0.0).astype(BF16)
                r0 = pl.multiple_of(ws - wb, BF16_SUBLANES)
                accs[j] = accs[j] + jnp.dot(onehot, buf_ref[slot, pl.ds(r0, win), :],
                                            preferred_element_type=F32)
        finish(jnp.concatenate(accs, axis=0))


def _combine(off, x1, slot_t, mod, g, ye, *, tt):
    n = x1.shape[0]
    win = min(2 * LANES, tt + BF16_SUBLANES)
    grid_spec = pltpu.PrefetchScalarGridSpec(
        num_scalar_prefetch=1,
        grid=(n // tt,),
        in_specs=[pl.BlockSpec((tt, D_MODEL), lambda t, off: (t, 0)),
                  pl.BlockSpec((tt, N_EXPERTS), lambda t, off: (t, 0)),
                  pl.BlockSpec(mod.shape, lambda t, off: (0, 0)),
                  pl.BlockSpec(g.shape, lambda t, off: (0, 0)),
                  pl.BlockSpec(memory_space=pl.ANY)],
        out_specs=pl.BlockSpec((tt, D_MODEL), lambda t, off: (t, 0)),
        scratch_shapes=[pltpu.VMEM((2, N_EXPERTS * LANES, D_MODEL), BF16),
                        pltpu.VMEM((2, tt + BF16_SUBLANES, D_MODEL), BF16),
                        pltpu.SemaphoreType.DMA((2,)),
                        pltpu.SemaphoreType.DMA((2,))])
    return pl.pallas_call(
        functools.partial(_combine_kernel, win=win, fast_max=COMBINE_FAST_MAX),
        grid_spec=grid_spec,
        out_shape=jax.ShapeDtypeStruct((n, D_MODEL), F32),
        compiler_params=_params(("arbitrary",)),
        name="combine",
    )(off, x1, slot_t, mod, g, ye)


def _rope_tables(n):
    rows = n // GRID_W
    half = HEAD_DIM // 4
    freqs = ROPE_BASE ** (-jnp.arange(half, dtype=F32) / half)
    ang_r = jnp.arange(rows, dtype=jnp.int32).astype(F32)[:, None] * freqs[None, :]
    ang_c = jnp.arange(GRID_W, dtype=jnp.int32).astype(F32)[:, None] * freqs[None, :]
    zr = jnp.zeros((rows, 2 * half), F32)
    zc = jnp.zeros((GRID_W, 2 * half), F32)
    head = lambda a, b: jnp.concatenate([a, b] * 2, axis=1)
    rcos = head(jnp.concatenate([jnp.cos(ang_r)] * 2, axis=1), zr)
    rsin = head(jnp.concatenate([-jnp.sin(ang_r), jnp.sin(ang_r)], axis=1), zr)
    ccos = head(zc, jnp.concatenate([jnp.cos(ang_c)] * 2, axis=1))
    csin = head(zc, jnp.concatenate([-jnp.sin(ang_c), jnp.sin(ang_c)], axis=1))
    return rcos, rsin, ccos, csin


def _block_diag(blocks):
    k = len(blocks)
    r, c = blocks[0].shape
    out = jnp.zeros((k * r, k * c), blocks[0].dtype)
    for i, b in enumerate(blocks):
        out = out.at[i * r:(i + 1) * r, i * c:(i + 1) * c].set(b)
    return out


def _dft_tables(n, cb_n2):
    nb = LANES
    na = n // nb
    gd = FOURIER_GROUP_DIM
    scale = 1.0 / math.sqrt(n * gd)

    def cos_sin(i, j, period):
        ang = ((i * j) % period).astype(F32) * (2.0 * math.pi / period)
        return jnp.cos(ang), jnp.sin(ang)

    ar_g = jnp.arange(gd, dtype=jnp.int32)
    c64, s64 = cos_sin(ar_g[:, None], ar_g[None, :], gd)
    w0 = jnp.concatenate([_block_diag([c64 * scale] * N_FOURIER_GROUPS),
                          _block_diag([-s64 * scale] * N_FOURIER_GROUPS)], axis=1)
    ar_a = jnp.arange(na, dtype=jnp.int32)
    ca, sa = cos_sin(ar_a[:, None], ar_a[None, :], na)
    wa = jnp.concatenate([jnp.concatenate([ca, sa], axis=1),
                          jnp.concatenate([-sa, ca], axis=1)], axis=0)
    ar_b = jnp.arange(nb, dtype=jnp.int32)
    ct, st = cos_sin(ar_a[:, None], ar_b[None, :], n)
    group = lambda t: t.reshape(na, nb // cb_n2, cb_n2).transpose(1, 0, 2)
    cc, sc = cos_sin(ar_b[:, None], ar_b[None, :], nb)
    mm = jnp.concatenate([cc, sc], axis=1)
    return w0.astype(BF16), wa.astype(BF16), group(ct), group(st), mm.astype(BF16)


def _prefix_tables(r_n):
    i = np.arange(LANES)
    ut = (i[:, None] < i[None, :]).astype(np.float32)
    ones = np.ones((LANES, LANES), np.float32)
    r = np.arange(r_n)
    slt = (r[None, :] < r[:, None]).astype(np.float32)
    as_bf16 = lambda a: jnp.asarray(a).astype(BF16)
    return as_bf16(ut), as_bf16(ones), as_bf16(slt)


def kernel(x, c, ctx, c_ctx, w_mod, b_mod, norm_mix_g, norm_ffn_g, w_in, w_four, q_norm_g, k_norm_g,
           w_out, w_router, w_gate, w_up, w_down, final_norm_g):
    batch, n, _ = x.shape
    assert batch == 1 and w_mod.shape[0] == 1
    lc = ctx.shape[1]
    cap = EC_CAPACITY_FACTOR * n // N_EXPERTS
    x2d = x[0]
    ctx2d = ctx[0]

    cc = jnp.zeros((8, D_MODEL), F32).at[0].set(c[0]).at[1].set(c_ctx)
    mod = _mod(cc, w_mod[0], b_mod[0][None, :])

    rope_x = _rope_tables(n)
    zero_c = jnp.zeros((GRID_W, LANES), F32)
    rope_c = (jnp.ones((lc // GRID_W, LANES), F32), jnp.zeros((lc // GRID_W, LANES), F32), zero_c, zero_c)
    inv_hd = jnp.full((HEAD_DIM, HEAD_DIM), 1.0 / HEAD_DIM, BF16)
    bq = _block_diag([inv_hd] * N_Q_HEADS)
    bk = _block_diag([inv_hd] * N_KV_HEADS)
    q_scale = (HEAD_DIM ** -0.5) * math.log2(math.e)
    gq = jnp.tile(q_norm_g[0], N_Q_HEADS)[None, :] * q_scale
    gk = jnp.tile(k_norm_g[0], N_KV_HEADS)[None, :]
    nb = LANES
    na = n // nb
    dft_cb = min(4096, nb * FOURIER_WIDTH)
    w0, wa, tw_c, tw_s, mm = _dft_tables(n, dft_cb // FOURIER_WIDTH)
    w_in_b = w_in[0].astype(BF16)
    g_mix = norm_mix_g[0][None, :]
    tm = min(512, n)
    zr, zi, q_t, k_l, v_t = _inproj(x2d, mod, g_mix, w_in_b, rope_x, bq, bk, gq, gk, w0,
                                    mod_row=0, with_q=True, tm=tm)
    k_c, v_c = _inproj(ctx2d, mod, g_mix, w_in_b, rope_c, bq, bk, gq, gk, w0,
                       mod_row=1, with_q=False, tm=lc)

    attn = _attention(q_t, k_l, v_t, k_c, v_c, tq=min(256, n), tk=min(2048, n // 2))

    br, bi = _dft_a(zr.reshape(na, nb * FOURIER_WIDTH), zi.reshape(na, nb * FOURIER_WIDTH), wa,
                    tw_c, tw_s, cb=dft_cb)
    wf = _block_diag([w_four[0, g] for g in range(N_FOURIER_GROUPS)]).astype(BF16)
    four = _dft_c(br.reshape(na, nb, FOURIER_WIDTH), bi.reshape(na, nb, FOURIER_WIDTH), mm, wf,
                  kb=8).reshape(n, FOURIER_WIDTH)

    w_out_b = w_out[0].astype(BF16)
    wo_f = w_out_b[:FOURIER_WIDTH]
    wo_a = w_out_b[FOURIER_WIDTH:].reshape(N_KV_HEADS, Q_PER_KV * HEAD_DIM, D_MODEL)
    wo_a = jnp.pad(wo_a, ((0, 0), (0, GROUP_PAD - Q_PER_KV * HEAD_DIM), (0, 0)))
    wr_t = w_router[0].T.astype(BF16)
    x1, h2, aff = _outproj(four, attn, x2d, mod, norm_ffn_g[0][None, :], wo_f, wo_a, wr_t, tm=tm)

    r_n = n // LANES
    ut, ones, slt = _prefix_tables(r_n)
    slot3, off3 = _topk(aff.reshape(N_EXPERTS, r_n, LANES), ut, ones, slt, cap=cap)
    off = jnp.concatenate([off3[:, :, 0], jnp.full((N_EXPERTS, 1), cap, jnp.int32)], axis=1)
    slot = slot3.reshape(N_EXPERTS, n)

    tt = (min(512, cap - BF16_SUBLANES) // LANES) * LANES
    xe, gate = _gather(off, slot.reshape(N_EXPERTS, 1, n), aff.reshape(N_EXPERTS, 1, n), h2,
                       cap=cap, tt=tt, eg=GATHER_EXPERTS)
    ye = _ffn(xe, gate, w_gate[0], w_up[0], w_down[0], cap=cap, tf=512, mc=min(1024, cap))
    out = _combine(off, x1, slot.T, mod, final_norm_g[None, :], ye, tt=tt)
    return out[None]
```

```python
import functools
import math

import numpy as np
import jax
import jax.numpy as jnp
from jax import lax
from jax.experimental import pallas as pl
from jax.experimental.pallas import tpu as pltpu

D_MODEL = 1024
GRID_W = 64
HEAD_DIM = 64
N_Q_HEADS = 12
N_KV_HEADS = 4
Q_PER_KV = N_Q_HEADS // N_KV_HEADS
ATTN_WIDTH = N_Q_HEADS * HEAD_DIM
KV_WIDTH = N_KV_HEADS * HEAD_DIM
N_FOURIER_GROUPS = 4
FOURIER_GROUP_DIM = 64
FOURIER_WIDTH = N_FOURIER_GROUPS * FOURIER_GROUP_DIM
IN_WIDTH = FOURIER_WIDTH + ATTN_WIDTH + 2 * KV_WIDTH
ROPE_BASE = 10000.0
N_EXPERTS = 16
EC_CAPACITY_FACTOR = 2
D_EXPERT = 2048
N_MOD = 6
NORM_EPS = 1e-6

LANES = 128
BF16_SUBLANES = 16
GROUP_PAD = 2 * LANES
V_ROWS = HEAD_DIM + BF16_SUBLANES
GATHER_WIN = LANES + BF16_SUBLANES
GATHER_FAST_MAX = LANES
COMBINE_FAST_MAX = LANES - BF16_SUBLANES
GATHER_EXPERTS = 4
VMEM_LIMIT = 56 * 1024 * 1024

F32 = jnp.float32
BF16 = jnp.bfloat16
F8 = jnp.float8_e4m3fn
F8_MAX = 448.0
F8_LO_SCALE = 16.0
QK_DEPTH = 4 * HEAD_DIM


def _params(sem):
    return pltpu.CompilerParams(dimension_semantics=sem, vmem_limit_bytes=VMEM_LIMIT)


def _mod_kernel(c_ref, w_ref, b_ref, o_ref):
    s = jax.nn.silu(c_ref[...])
    o_ref[...] = jnp.dot(s, w_ref[...], precision=lax.Precision.HIGHEST,
                         preferred_element_type=F32) + b_ref[...]


def _mod(cc, w_mod, b_mod):
    tn = 1024
    n = w_mod.shape[1]
    return pl.pallas_call(
        _mod_kernel,
        grid=(n // tn,),
        in_specs=[pl.BlockSpec((8, D_MODEL), lambda j: (0, 0)),
                  pl.BlockSpec((D_MODEL, tn), lambda j: (0, j)),
                  pl.BlockSpec((1, tn), lambda j: (0, j))],
        out_specs=pl.BlockSpec((8, tn), lambda j: (0, j)),
        out_shape=jax.ShapeDtypeStruct((8, n), F32),
        compiler_params=_params(("arbitrary",)),
        name="mod",
    )(cc, w_mod, b_mod)


def _rms_modulate(x, g, shift, scale):
    ms = jnp.mean(x * x, axis=-1, keepdims=True)
    y = x * lax.rsqrt(ms + NORM_EPS) * g
    return y * (1.0 + scale) + shift


def _head_norm(t, bmat, gain):
    ms = jnp.dot((t * t).astype(BF16), bmat, preferred_element_type=F32)
    return t * lax.rsqrt(ms + NORM_EPS) * gain


def _f8_pair(x, weights):
    inv = 1.0 / F8_LO_SCALE
    hi = jnp.clip(x, -F8_MAX, F8_MAX).astype(F8).astype(F32)
    lo = ((x - hi) * F8_LO_SCALE).astype(F8).astype(F32)
    return (hi, hi * inv, lo, lo * inv) if weights else (hi, lo, hi * inv, lo * inv)


def _rope(t, cos, sin, lo_mask):
    outs = []
    for c in range(t.shape[1] // LANES):
        xc = t[:, c * LANES:(c + 1) * LANES]
        up = pltpu.roll(xc, LANES - 16, axis=1)
        dn = pltpu.roll(xc, 16, axis=1)
        rot = jnp.where(lo_mask, up, dn)
        outs.append(xc * cos + rot * sin)
    return jnp.concatenate(outs, axis=1)


def _inproj_kernel(x_ref, mod_ref, g_ref, w_ref, rcos_ref, rsin_ref, ccos_ref, csin_ref, bq_ref, bk_ref,
                   gq_ref, gk_ref, w0_ref, *out_refs, mod_row, with_q):
    if with_q:
        zr_ref, zi_ref, q_ref, k_ref, v_ref = out_refs
    else:
        k_ref, v_ref = out_refs
    tm = x_ref.shape[0]
    shift = mod_ref[mod_row:mod_row + 1, 0:D_MODEL]
    scale = mod_ref[mod_row:mod_row + 1, D_MODEL:2 * D_MODEL]
    h = _rms_modulate(x_ref[...], g_ref[...], shift, scale).astype(BF16)
    cos = jnp.concatenate([rcos_ref[r:r + 1, :] + ccos_ref[...] for r in range(tm // GRID_W)], axis=0)
    sin = jnp.concatenate([rsin_ref[r:r + 1, :] + csin_ref[...] for r in range(tm // GRID_W)], axis=0)
    lane = lax.broadcasted_iota(jnp.int32, (tm, LANES), 1)
    lo_mask = (lane % 32) < 16
    k0 = FOURIER_WIDTH + ATTN_WIDTH
    if with_q:
        p = jnp.dot(h, w_ref[...], preferred_element_type=F32)
        f = p[:, :FOURIER_WIDTH].astype(BF16)
        z = jnp.dot(f, w0_ref[...], preferred_element_type=F32)
        zr_ref[...] = z[:, :FOURIER_WIDTH].astype(BF16)
        zi_ref[...] = z[:, FOURIER_WIDTH:].astype(BF16)
        q = _head_norm(p[:, FOURIER_WIDTH:k0], bq_ref[...], gq_ref[...])
        q = _rope(q, cos, sin, lo_mask)
        for j, piece in enumerate(_f8_pair(q.T, weights=True)):
            q_ref[:, j * HEAD_DIM:(j + 1) * HEAD_DIM, :] = piece.astype(F8).reshape(
                N_Q_HEADS, HEAD_DIM, tm)
        pk = p[:, k0:k0 + KV_WIDTH]
        pv = p[:, k0 + KV_WIDTH:]
    else:
        pkv = jnp.dot(h, w_ref[:, k0:], preferred_element_type=F32)
        pk = pkv[:, :KV_WIDTH]
        pv = pkv[:, KV_WIDTH:]
    k = _head_norm(pk, bk_ref[...], gk_ref[...])
    k = _rope(k, cos, sin, lo_mask)
    pieces = _f8_pair(k, weights=False)
    for g in range(N_KV_HEADS):
        cols = slice(g * HEAD_DIM, (g + 1) * HEAD_DIM)
        k_ref[g] = jnp.concatenate([pc[:, cols] for pc in pieces], axis=1).astype(F8)
    v_ref[:, 0:HEAD_DIM, :] = pv.T.astype(BF16).reshape(N_KV_HEADS, HEAD_DIM, tm)
    v_ref[:, HEAD_DIM:V_ROWS, :] = jnp.ones((N_KV_HEADS, V_ROWS - HEAD_DIM, tm), BF16)


def _inproj(x, mod, g, w_in, rope, bq, bk, gq, gk, w0, *, mod_row, with_q, tm):
    n = x.shape[0]
    rcos, rsin, ccos, csin = rope
    const = lambda shape: pl.BlockSpec(shape, lambda i: (0,) * len(shape))
    in_specs = [pl.BlockSpec((tm, D_MODEL), lambda i: (i, 0)),
                const(mod.shape), const(g.shape), const(w_in.shape),
                pl.BlockSpec((tm // GRID_W, LANES), lambda i: (i, 0)),
                pl.BlockSpec((tm // GRID_W, LANES), lambda i: (i, 0)),
                const(ccos.shape), const(csin.shape),
                const(bq.shape), const(bk.shape), const(gq.shape), const(gk.shape),
                const(w0.shape)]
    kv_specs = [pl.BlockSpec((N_KV_HEADS, tm, QK_DEPTH), lambda i: (0, i, 0)),
                pl.BlockSpec((N_KV_HEADS, V_ROWS, tm), lambda i: (0, 0, i))]
    kv_shapes = [jax.ShapeDtypeStruct((N_KV_HEADS, n, QK_DEPTH), F8),
                 jax.ShapeDtypeStruct((N_KV_HEADS, V_ROWS, n), BF16)]
    if with_q:
        out_specs = [pl.BlockSpec((tm, FOURIER_WIDTH), lambda i: (i, 0)),
                     pl.BlockSpec((tm, FOURIER_WIDTH), lambda i: (i, 0)),
                     pl.BlockSpec((N_Q_HEADS, QK_DEPTH, tm), lambda i: (0, 0, i))] + kv_specs
        out_shapes = [jax.ShapeDtypeStruct((n, FOURIER_WIDTH), BF16),
                      jax.ShapeDtypeStruct((n, FOURIER_WIDTH), BF16),
                      jax.ShapeDtypeStruct((N_Q_HEADS, QK_DEPTH, n), F8)] + kv_shapes
    else:
        out_specs, out_shapes = kv_specs, kv_shapes
    return pl.pallas_call(
        functools.partial(_inproj_kernel, mod_row=mod_row, with_q=with_q),
        grid=(n // tm,),
        in_specs=in_specs,
        out_specs=out_specs,
        out_shape=out_shapes,
        compiler_params=_params(("arbitrary",)),
        name="inproj_x" if with_q else "inproj_ctx",
    )(x, mod, g, w_in, rcos, rsin, ccos, csin, bq, bk, gq, gk, w0)


def _attn_kernel(q_ref, k_ref, v_ref, kc_ref, vc_ref, o_ref, s_ref, acc_ref, *, tk):
    tq = q_ref.shape[2]
    n = k_ref.shape[1]
    lc = kc_ref.shape[1]
    nch = n // tk
    qs = jnp.concatenate([q_ref[j] for j in range(Q_PER_KV)], axis=1)

    def scores_latent(c, slot):
        start = pl.multiple_of(c * tk, tk)
        s = jnp.dot(k_ref[0, pl.ds(start, tk), :], qs, preferred_element_type=F32)
        s_ref[slot] = s
        return jnp.max(s, axis=0, keepdims=True)

    def scores_context():
        s = jnp.dot(kc_ref[0], qs, preferred_element_type=F32)
        s_ref[0, 0:lc, :] = s
        return jnp.max(s, axis=0, keepdims=True)

    def v_latent(c):
        return v_ref[0, :, pl.ds(pl.multiple_of(c * tk, tk), tk)]

    def consume(slot, rows, vb, m, mx):
        s = s_ref[slot, 0:rows, :]
        m_new = jnp.maximum(m, mx)
        alpha = jnp.exp2(m - m_new)
        p = jnp.exp2(s - m_new).astype(BF16)
        acc_ref[...] = alpha * acc_ref[...] + jnp.dot(vb, p, preferred_element_type=F32)
        return m_new

    def pair(c0, carry, prefetch):
        m, mx0 = carry
        mx1 = scores_latent(c0 + 1, 1)
        m = consume(0, tk, v_latent(c0), m, mx0)
        mx0 = prefetch()
        return consume(1, tk, v_latent(c0 + 1), m, mx1), mx0

    def body(i, carry):
        return pair(2 * i, carry, lambda: scores_latent(2 * i + 2, 0))

    w = Q_PER_KV * tq
    acc_ref[...] = jnp.zeros_like(acc_ref)
    mx0 = scores_latent(0, 0)
    carry = lax.fori_loop(0, nch // 2 - 1, body, (jnp.full((1, w), -1e30, F32), mx0))
    m, mx_ctx = pair(nch - 2, carry, scores_context)
    consume(0, lc, vc_ref[0], m, mx_ctx)
    acc = acc_ref[...]
    o = acc[0:HEAD_DIM] / acc[HEAD_DIM:HEAD_DIM + 1]
    ot = jnp.concatenate([o, jnp.zeros_like(o)], axis=0).T
    h0, h1, h2 = (ot[j * tq:(j + 1) * tq] for j in range(Q_PER_KV))
    o_ref[0] = jnp.concatenate([h0 + pltpu.roll(h1, HEAD_DIM, axis=1), h2], axis=1).astype(BF16)


def _attention(q, k, v, kc, vc, *, tq, tk):
    n = k.shape[1]
    lc = kc.shape[1]
    kd = k.shape[2]
    gw = GROUP_PAD
    return pl.pallas_call(
        functools.partial(_attn_kernel, tk=tk),
        grid=(N_KV_HEADS, n // tq),
        in_specs=[pl.BlockSpec((Q_PER_KV, kd, tq), lambda g, i: (g, 0, i)),
                  pl.BlockSpec((1, n, kd), lambda g, i: (g, 0, 0)),
                  pl.BlockSpec((1, V_ROWS, n), lambda g, i: (g, 0, 0)),
                  pl.BlockSpec((1, lc, kd), lambda g, i: (g, 0, 0)),
                  pl.BlockSpec((1, V_ROWS, lc), lambda g, i: (g, 0, 0))],
        out_specs=pl.BlockSpec((1, tq, gw), lambda g, i: (g, i, 0)),
        out_shape=jax.ShapeDtypeStruct((N_KV_HEADS, n, gw), BF16),
        scratch_shapes=[pltpu.VMEM((2, tk, Q_PER_KV * tq), F32),
                        pltpu.VMEM((V_ROWS, Q_PER_KV * tq), F32)],
        compiler_params=_params(("arbitrary", "arbitrary")),
        name="attention",
    )(q, k, v, kc, vc)


def _dft_a_kernel(zr_ref, zi_ref, w_ref, tc_ref, ts_ref, br_ref, bi_ref):
    na = zr_ref.shape[0]
    fw = FOURIER_WIDTH
    zz = jnp.concatenate([zr_ref[...], zi_ref[...]], axis=0)
    a = jnp.dot(w_ref[...], zz, preferred_element_type=F32)
    tc = tc_ref[0]
    ts = ts_ref[0]
    for j in range(tc.shape[1]):
        ar = a[:na, j * fw:(j + 1) * fw]
        ai = a[na:, j * fw:(j + 1) * fw]
        c = tc[:, j:j + 1]
        s = ts[:, j:j + 1]
        br_ref[:, j * fw:(j + 1) * fw] = (ar * c + ai * s).astype(BF16)
        bi_ref[:, j * fw:(j + 1) * fw] = (ai * c - ar * s).astype(BF16)


def _dft_a(zr2, zi2, wa, tc3, ts3, *, cb):
    na, cols = zr2.shape
    blk = pl.BlockSpec((na, cb), lambda j: (0, j))
    tw = pl.BlockSpec((1,) + tc3.shape[1:], lambda j: (j, 0, 0))
    return pl.pallas_call(
        _dft_a_kernel,
        grid=(cols // cb,),
        in_specs=[blk, blk, pl.BlockSpec(wa.shape, lambda j: (0, 0)), tw, tw],
        out_specs=[blk, blk],
        out_shape=[jax.ShapeDtypeStruct((na, cols), BF16)] * 2,
        compiler_params=_params(("arbitrary",)),
        name="dft_a",
    )(zr2, zi2, wa, tc3, ts3)


def _dft_c_kernel(br_ref, bi_ref, m_ref, wf_ref, o_ref):
    kb = br_ref.shape[0]
    outs = []
    for j in range(kb):
        ab = jnp.concatenate([br_ref[j], bi_ref[j]], axis=0)
        y = jnp.dot(m_ref[...], ab, preferred_element_type=F32)
        outs.append(jnp.dot(y.astype(BF16), wf_ref[...], preferred_element_type=F32))
    o_ref[...] = jnp.concatenate(outs, axis=1).astype(BF16)


def _dft_c(br3, bi3, mm, wf, *, kb):
    na, nb, fw = br3.shape
    blk = pl.BlockSpec((kb, nb, fw), lambda i: (i, 0, 0))
    return pl.pallas_call(
        _dft_c_kernel,
        grid=(na // kb,),
        in_specs=[blk, blk,
                  pl.BlockSpec(mm.shape, lambda i: (0, 0)),
                  pl.BlockSpec(wf.shape, lambda i: (0, 0))],
        out_specs=pl.BlockSpec((nb, kb * fw), lambda i: (0, i)),
        out_shape=jax.ShapeDtypeStruct((nb, na * fw), BF16),
        compiler_params=_params(("arbitrary",)),
        name="dft_c",
    )(br3, bi3, mm, wf)


def _outproj_kernel(four_ref, attn_ref, x_ref, mod_ref, g_ref, wf_ref, wa_ref, wr_ref,
                    x1_ref, h_ref, aff_ref):
    mix = jnp.dot(four_ref[...], wf_ref[...], preferred_element_type=F32)
    for g in range(N_KV_HEADS):
        mix = mix + jnp.dot(attn_ref[g], wa_ref[g], preferred_element_type=F32)
    gate1 = mod_ref[0:1, 2 * D_MODEL:3 * D_MODEL]
    shift2 = mod_ref[0:1, 3 * D_MODEL:4 * D_MODEL]
    scale2 = mod_ref[0:1, 4 * D_MODEL:5 * D_MODEL]
    x1 = x_ref[...] + gate1 * mix
    x1_ref[...] = x1
    h = _rms_modulate(x1, g_ref[...], shift2, scale2).astype(BF16)
    h_ref[...] = h
    logits = lax.dot_general(wr_ref[...], h, (((1,), (1,)), ((), ())),
                             preferred_element_type=F32)
    mx = jnp.max(logits, axis=0, keepdims=True)
    e = jnp.exp(logits - mx)
    aff_ref[...] = e / jnp.sum(e, axis=0, keepdims=True)


def _outproj(four, attn, x, mod, g, wo_f, wo_a, wr_t, *, tm):
    n = x.shape[0]
    gw = GROUP_PAD
    const = lambda shape: pl.BlockSpec(shape, lambda i: (0,) * len(shape))
    return pl.pallas_call(
        _outproj_kernel,
        grid=(n // tm,),
        in_specs=[pl.BlockSpec((tm, FOURIER_WIDTH), lambda i: (i, 0)),
                  pl.BlockSpec((N_KV_HEADS, tm, gw), lambda i: (0, i, 0)),
                  pl.BlockSpec((tm, D_MODEL), lambda i: (i, 0)),
                  const(mod.shape), const(g.shape), const(wo_f.shape), const(wo_a.shape),
                  const(wr_t.shape)],
        out_specs=[pl.BlockSpec((tm, D_MODEL), lambda i: (i, 0)),
                   pl.BlockSpec((tm, D_MODEL), lambda i: (i, 0)),
                   pl.BlockSpec((N_EXPERTS, tm), lambda i: (0, i))],
        out_shape=[jax.ShapeDtypeStruct((n, D_MODEL), F32),
                   jax.ShapeDtypeStruct((n, D_MODEL), BF16),
                   jax.ShapeDtypeStruct((N_EXPERTS, n), F32)],
        compiler_params=_params(("arbitrary",)),
        name="outproj",
    )(four, attn, x, mod, g, wo_f, wo_a, wr_t)


def _count(m):
    return jnp.sum(jnp.sum(m, axis=2, keepdims=True), axis=1, keepdims=True)


def _topk_kernel(aff_ref, ut_ref, ones_ref, slt_ref, slot_ref, off_ref, *, cap):
    e_n, r_n, _ = aff_ref.shape
    aff = aff_ref[...]
    capf = jnp.float32(cap)

    def count_ge(th):
        return _count(jnp.where(aff >= th, 1.0, 0.0))

    def search(i, t):
        cand = t | jnp.left_shift(jnp.int32(1), 30 - i)
        c = count_ge(lax.bitcast_convert_type(cand, F32))
        return jnp.where(c >= capf, cand, t)

    tbits = lax.fori_loop(0, 31, search, jnp.zeros((e_n, 1, 1), jnp.int32))
    thr = lax.bitcast_convert_type(tbits, F32)
    ulp = lax.bitcast_convert_type(tbits + 1, F32) - thr

    def refine(_, carry):
        t, step = carry
        step = step * 0.5
        cand = t + step
        return jnp.where(count_ge(cand) >= capf, cand, t), step

    thr, _ = lax.fori_loop(0, 29, refine, (thr, ulp))
    gt = jnp.where(aff > thr, 1.0, 0.0)
    eq = jnp.where(aff == thr, 1.0, 0.0)
    need = capf - _count(gt)

    def excl_prefix(m):
        m2 = m.reshape(e_n * r_n, LANES).astype(BF16)
        within = jnp.dot(m2, ut_ref[...], preferred_element_type=F32).reshape(e_n, r_n, LANES)
        tot = jnp.dot(m2, ones_ref[...], preferred_element_type=F32).reshape(e_n, r_n, LANES)
        rows = jnp.stack([jnp.dot(slt_ref[...], tot[e].astype(BF16), preferred_element_type=F32)
                          for e in range(e_n)], axis=0)
        return within + rows, rows

    pe, _ = excl_prefix(eq)
    sel = jnp.maximum(gt, eq * jnp.where(pe < need, 1.0, 0.0))
    pos, rows = excl_prefix(sel)
    slot_ref[...] = jnp.where(sel > 0.5, pos, -1.0).astype(jnp.int32)
    off_ref[...] = rows.astype(jnp.int32)


def _topk(aff3, ut, ones, slt, *, cap):
    shp = aff3.shape
    full = lambda a: pl.BlockSpec(a.shape, lambda i: (0,) * a.ndim)
    return pl.pallas_call(
        functools.partial(_topk_kernel, cap=cap),
        grid=(1,),
        in_specs=[full(aff3), full(ut), full(ones), full(slt)],
        out_specs=[pl.BlockSpec(shp, lambda i: (0, 0, 0))] * 2,
        out_shape=[jax.ShapeDtypeStruct(shp, jnp.int32)] * 2,
        compiler_params=_params(("arbitrary",)),
        name="topk",
    )(aff3, ut, ones, slt)


def _window_start(off):
    return pl.multiple_of((off // BF16_SUBLANES) * BF16_SUBLANES, BF16_SUBLANES)


def _tile_max_count(off_ref, experts, t, nsub):
    cnts = [off_ref[e, (t + 1) * nsub] - off_ref[e, t * nsub] for e in experts]
    return functools.reduce(jnp.maximum, cnts)


def _gather_kernel(off_ref, slot_ref, aff_ref, h_ref, xe_ref, gate_ref, *, fast_max):
    p = pl.program_id(0)
    t = pl.program_id(1)
    eg, _, tt = slot_ref.shape
    nsub = tt // LANES
    experts = [p * eg + j for j in range(eg)]

    @pl.when(t == 0)
    def _():
        xe_ref[...] = jnp.zeros_like(xe_ref)
        gate_ref[...] = jnp.zeros_like(gate_ref)

    def scatter_rows(j, ws, hit, res, aff_row):
        xe_ref[j, pl.ds(ws, GATHER_WIN), :] += res.astype(BF16)
        g = jnp.sum(jnp.where(hit, aff_row, 0.0), axis=1, keepdims=True)
        gate_ref[j, pl.ds(ws, GATHER_WIN), :] += jnp.broadcast_to(g, (GATHER_WIN, LANES))

    fast = _tile_max_count(off_ref, experts, t, nsub) <= fast_max

    @pl.when(fast)
    def _():
        rows = lax.broadcasted_iota(jnp.int32, (GATHER_WIN, tt), 0)
        starts = [_window_start(off_ref[e, t * nsub]) for e in experts]
        hits = [rows == (slot_ref[j] - starts[j]) for j in range(eg)]
        onehot = jnp.concatenate([jnp.where(hit, 1.0, 0.0).astype(BF16) for hit in hits], axis=0)
        res = jnp.dot(onehot, h_ref[...], preferred_element_type=F32)
        for j in range(eg):
            scatter_rows(j, starts[j], hits[j], res[j * GATHER_WIN:(j + 1) * GATHER_WIN], aff_ref[j])

    @pl.when(jnp.logical_not(fast))
    def _():
        rows = lax.broadcasted_iota(jnp.int32, (GATHER_WIN, LANES), 0)
        for j in range(eg):
            for b in range(nsub):
                ws = _window_start(off_ref[experts[j], t * nsub + b])
                lanes = slice(b * LANES, (b + 1) * LANES)
                hit = rows == (slot_ref[j, :, lanes] - ws)
                res = jnp.dot(jnp.where(hit, 1.0, 0.0).astype(BF16), h_ref[lanes, :],
                              preferred_element_type=F32)
                scatter_rows(j, ws, hit, res, aff_ref[j, :, lanes])


def _gather(off, slot3, aff3, h, *, cap, tt, eg):
    n = h.shape[0]
    capp = cap + GATHER_WIN
    grid_spec = pltpu.PrefetchScalarGridSpec(
        num_scalar_prefetch=1,
        grid=(N_EXPERTS // eg, n // tt),
        in_specs=[pl.BlockSpec((eg, 1, tt), lambda p, t, off: (p, 0, t)),
                  pl.BlockSpec((eg, 1, tt), lambda p, t, off: (p, 0, t)),
                  pl.BlockSpec((tt, D_MODEL), lambda p, t, off: (t, 0))],
        out_specs=[pl.BlockSpec((eg, capp, D_MODEL), lambda p, t, off: (p, 0, 0)),
                   pl.BlockSpec((eg, capp, LANES), lambda p, t, off: (p, 0, 0))])
    return pl.pallas_call(
        functools.partial(_gather_kernel, fast_max=GATHER_FAST_MAX),
        grid_spec=grid_spec,
        out_shape=[jax.ShapeDtypeStruct((N_EXPERTS, capp, D_MODEL), BF16),
                   jax.ShapeDtypeStruct((N_EXPERTS, capp, LANES), F32)],
        compiler_params=_params(("arbitrary", "arbitrary")),
        name="gather",
    )(off, slot3, aff3, h)


def _ffn_kernel(xe_ref, gate_ref, wg_ref, wu_ref, wd_ref, ye_ref, acc_ref, *, mc):
    f = pl.program_id(1)
    cap = xe_ref.shape[1]
    wg = wg_ref[0].astype(BF16)
    wu = wu_ref[0].astype(BF16)
    wd = wd_ref[0].astype(BF16)

    @pl.when(f == 0)
    def _():
        acc_ref[...] = jnp.zeros_like(acc_ref)

    for m in range(cap // mc):
        rows = slice(m * mc, (m + 1) * mc)
        xm = xe_ref[0, rows, :]
        gate = jnp.dot(xm, wg, preferred_element_type=F32)
        up = jnp.dot(xm, wu, preferred_element_type=F32)
        hid = (jax.nn.silu(gate) * up).astype(BF16)
        acc_ref[rows, :] += jnp.dot(hid, wd, preferred_element_type=F32)

    @pl.when(f == pl.num_programs(1) - 1)
    def _():
        for m in range(cap // mc):
            rows = slice(m * mc, (m + 1) * mc)
            gate = jnp.concatenate([gate_ref[0, rows, :]] * (D_MODEL // LANES), axis=1)
            ye_ref[0, rows, :] = (acc_ref[rows, :] * gate).astype(BF16)


def _ffn(xe, gate, w_gate, w_up, w_down, *, cap, tf, mc):
    e_n = xe.shape[0]
    return pl.pallas_call(
        functools.partial(_ffn_kernel, mc=mc),
        grid=(e_n, D_EXPERT // tf),
        in_specs=[pl.BlockSpec((1, cap, D_MODEL), lambda e, f: (e, 0, 0)),
                  pl.BlockSpec((1, cap, LANES), lambda e, f: (e, 0, 0)),
                  pl.BlockSpec((1, D_MODEL, tf), lambda e, f: (e, 0, f)),
                  pl.BlockSpec((1, D_MODEL, tf), lambda e, f: (e, 0, f)),
                  pl.BlockSpec((1, tf, D_MODEL), lambda e, f: (e, f, 0))],
        out_specs=pl.BlockSpec((1, cap, D_MODEL), lambda e, f: (e, 0, 0)),
        out_shape=jax.ShapeDtypeStruct((e_n, cap, D_MODEL), BF16),
        scratch_shapes=[pltpu.VMEM((cap, D_MODEL), F32)],
        compiler_params=_params(("arbitrary", "arbitrary")),
        name="ffn",
    )(xe, gate, w_gate, w_up, w_down)


def _combine_kernel(off_ref, x1_ref, slot_ref, mod_ref, g_ref, ye_ref, o_ref,
                    wbuf_ref, buf_ref, wsem_ref, sem_ref, *, win, fast_max):
    t = pl.program_id(0)
    tt = x1_ref.shape[0]
    cap = ye_ref.shape[1]
    nsub = tt // LANES
    big = tt + BF16_SUBLANES
    experts = list(range(N_EXPERTS))

    def finish(moe):
        gate2 = mod_ref[0:1, 5 * D_MODEL:6 * D_MODEL]
        x2 = x1_ref[...] + gate2 * moe
        ms = jnp.mean(x2 * x2, axis=-1, keepdims=True)
        o_ref[...] = x2 * lax.rsqrt(ms + NORM_EPS) * g_ref[...]

    def clamped_start(e, rows, tile=t):
        return pl.multiple_of(jnp.minimum(_window_start(off_ref[e, tile * nsub]), cap - rows),
                              BF16_SUBLANES)

    def is_fast(tile):
        return _tile_max_count(off_ref, experts, tile, nsub) <= fast_max

    def window_copies(tile):
        buf = tile % 2
        return [pltpu.make_async_copy(ye_ref.at[e, pl.ds(clamped_start(e, LANES, tile), LANES), :],
                                      wbuf_ref.at[buf, pl.ds(e * LANES, LANES), :], wsem_ref.at[buf])
                for e in experts]

    def start_windows(tile):
        for cp in window_copies(tile):
            cp.start()

    fast = is_fast(t)
    pl.when(jnp.logical_and(t == 0, fast))(lambda: start_windows(t))
    nxt = jnp.minimum(t + 1, pl.num_programs(0) - 1)
    pl.when(jnp.logical_and(t + 1 < pl.num_programs(0), is_fast(nxt)))(lambda: start_windows(nxt))

    @pl.when(fast)
    def _():
        cols = lax.broadcasted_iota(jnp.int32, (tt, LANES), 1)
        onehot = jnp.concatenate(
            [jnp.where(cols == (slot_ref[:, e:e + 1] - clamped_start(e, LANES)), 1.0, 0.0).astype(BF16)
             for e in experts], axis=1)
        for cp in window_copies(t):
            cp.wait()
        finish(jnp.dot(onehot, wbuf_ref[t % 2], preferred_element_type=F32))

    @pl.when(jnp.logical_not(fast))
    def _():
        def copy(e, slot):
            return pltpu.make_async_copy(ye_ref.at[e, pl.ds(clamped_start(e, big), big), :],
                                         buf_ref.at[slot], sem_ref.at[slot])

        copy(0, 0).start()
        cols = lax.broadcasted_iota(jnp.int32, (LANES, win), 1)
        accs = [jnp.zeros((LANES, D_MODEL), F32) for _ in range(nsub)]
        for e in experts:
            slot = e % 2
            if e + 1 < N_EXPERTS:
                copy(e + 1, 1 - slot).start()
            copy(e, slot).wait()
            wb = clamped_start(e, big)
            for j in range(nsub):
                ws = jnp.minimum(_window_start(off_ref[e, t * nsub + j]), wb + (big - win))
                rel = slot_ref[j * LANES:(j + 1) * LANES, e:e + 1] - ws
                onehot = jnp.where(cols == rel, 1.0, 0.0).astype(BF16)
                r0 = pl.multiple_of(ws - wb, BF16_SUBLANES)
                accs[j] = accs[j] + jnp.dot(onehot, buf_ref[slot, pl.ds(r0, win), :],
                                            preferred_element_type=F32)
        finish(jnp.concatenate(accs, axis=0))


def _combine(off, x1, slot_t, mod, g, ye, *, tt):
    n = x1.shape[0]
    win = min(2 * LANES, tt + BF16_SUBLANES)
    grid_spec = pltpu.PrefetchScalarGridSpec(
        num_scalar_prefetch=1,
        grid=(n // tt,),
        in_specs=[pl.BlockSpec((tt, D_MODEL), lambda t, off: (t, 0)),
                  pl.BlockSpec((tt, N_EXPERTS), lambda t, off: (t, 0)),
                  pl.BlockSpec(mod.shape, lambda t, off: (0, 0)),
                  pl.BlockSpec(g.shape, lambda t, off: (0, 0)),
                  pl.BlockSpec(memory_space=pl.ANY)],
        out_specs=pl.BlockSpec((tt, D_MODEL), lambda t, off: (t, 0)),
        scratch_shapes=[pltpu.VMEM((2, N_EXPERTS * LANES, D_MODEL), BF16),
                        pltpu.VMEM((2, tt + BF16_SUBLANES, D_MODEL), BF16),
                        pltpu.SemaphoreType.DMA((2,)),
                        pltpu.SemaphoreType.DMA((2,))])
    return pl.pallas_call(
        functools.partial(_combine_kernel, win=win, fast_max=COMBINE_FAST_MAX),
        grid_spec=grid_spec,
        out_shape=jax.ShapeDtypeStruct((n, D_MODEL), F32),
        compiler_params=_params(("arbitrary",)),
        name="combine",
    )(off, x1, slot_t, mod, g, ye)


def _rope_tables(n):
    rows = n // GRID_W
    half = HEAD_DIM // 4
    freqs = ROPE_BASE ** (-jnp.arange(half, dtype=F32) / half)
    ang_r = jnp.arange(rows, dtype=jnp.int32).astype(F32)[:, None] * freqs[None, :]
    ang_c = jnp.arange(GRID_W, dtype=jnp.int32).astype(F32)[:, None] * freqs[None, :]
    zr = jnp.zeros((rows, 2 * half), F32)
    zc = jnp.zeros((GRID_W, 2 * half), F32)
    head = lambda a, b: jnp.concatenate([a, b] * 2, axis=1)
    rcos = head(jnp.concatenate([jnp.cos(ang_r)] * 2, axis=1), zr)
    rsin = head(jnp.concatenate([-jnp.sin(ang_r), jnp.sin(ang_r)], axis=1), zr)
    ccos = head(zc, jnp.concatenate([jnp.cos(ang_c)] * 2, axis=1))
    csin = head(zc, jnp.concatenate([-jnp.sin(ang_c), jnp.sin(ang_c)], axis=1))
    return rcos, rsin, ccos, csin


def _block_diag(blocks):
    k = len(blocks)
    r, c = blocks[0].shape
    out = jnp.zeros((k * r, k * c), blocks[0].dtype)
    for i, b in enumerate(blocks):
        out = out.at[i * r:(i + 1) * r, i * c:(i + 1) * c].set(b)
    return out


def _dft_tables(n, cb_n2):
    nb = LANES
    na = n // nb
    gd = FOURIER_GROUP_DIM
    scale = 1.0 / math.sqrt(n * gd)

    def cos_sin(i, j, period):
        ang = ((i * j) % period).astype(F32) * (2.0 * math.pi / period)
        return jnp.cos(ang), jnp.sin(ang)

    ar_g = jnp.arange(gd, dtype=jnp.int32)
    c64, s64 = cos_sin(ar_g[:, None], ar_g[None, :], gd)
    w0 = jnp.concatenate([_block_diag([c64 * scale] * N_FOURIER_GROUPS),
                          _block_diag([-s64 * scale] * N_FOURIER_GROUPS)], axis=1)
    ar_a = jnp.arange(na, dtype=jnp.int32)
    ca, sa = cos_sin(ar_a[:, None], ar_a[None, :], na)
    wa = jnp.concatenate([jnp.concatenate([ca, sa], axis=1),
                          jnp.concatenate([-sa, ca], axis=1)], axis=0)
    ar_b = jnp.arange(nb, dtype=jnp.int32)
    ct, st = cos_sin(ar_a[:, None], ar_b[None, :], n)
    group = lambda t: t.reshape(na, nb // cb_n2, cb_n2).transpose(1, 0, 2)
    cc, sc = cos_sin(ar_b[:, None], ar_b[None, :], nb)
    mm = jnp.concatenate([cc, sc], axis=1)
    return w0.astype(BF16), wa.astype(BF16), group(ct), group(st), mm.astype(BF16)


def _prefix_tables(r_n):
    i = np.arange(LANES)
    ut = (i[:, None] < i[None, :]).astype(np.float32)
    ones = np.ones((LANES, LANES), np.float32)
    r = np.arange(r_n)
    slt = (r[None, :] < r[:, None]).astype(np.float32)
    as_bf16 = lambda a: jnp.asarray(a).astype(BF16)
    return as_bf16(ut), as_bf16(ones), as_bf16(slt)


def kernel(x, c, ctx, c_ctx, w_mod, b_mod, norm_mix_g, norm_ffn_g, w_in, w_four, q_norm_g, k_norm_g,
           w_out, w_router, w_gate, w_up, w_down, final_norm_g):
    batch, n, _ = x.shape
    assert batch == 1 and w_mod.shape[0] == 1
    lc = ctx.shape[1]
    cap = EC_CAPACITY_FACTOR * n // N_EXPERTS
    x2d = x[0]
    ctx2d = ctx[0]

    cc = jnp.zeros((8, D_MODEL), F32).at[0].set(c[0]).at[1].set(c_ctx)
    mod = _mod(cc, w_mod[0], b_mod[0][None, :])

    rope_x = _rope_tables(n)
    zero_c = jnp.zeros((GRID_W, LANES), F32)
    rope_c = (jnp.ones((lc // GRID_W, LANES), F32), jnp.zeros((lc // GRID_W, LANES), F32), zero_c, zero_c)
    inv_hd = jnp.full((HEAD_DIM, HEAD_DIM), 1.0 / HEAD_DIM, BF16)
    bq = _block_diag([inv_hd] * N_Q_HEADS)
    bk = _block_diag([inv_hd] * N_KV_HEADS)
    q_scale = (HEAD_DIM ** -0.5) * math.log2(math.e)
    ratio = jnp.max(jnp.abs(k_norm_g[0])) / jnp.maximum(jnp.max(jnp.abs(q_norm_g[0])) * q_scale, 1e-30)
    shift = jnp.clip(jnp.round(0.5 * jnp.log2(jnp.maximum(ratio, 1e-30))), -60, 60).astype(jnp.int32)
    pow2 = lambda e: lax.bitcast_convert_type((e + 127) << 23, F32)
    gq = jnp.tile(q_norm_g[0], N_Q_HEADS)[None, :] * q_scale * pow2(shift)
    gk = jnp.tile(k_norm_g[0], N_KV_HEADS)[None, :] * pow2(-shift)
    nb = LANES
    na = n // nb
    dft_cb = min(4096, nb * FOURIER_WIDTH)
    w0, wa, tw_c, tw_s, mm = _dft_tables(n, dft_cb // FOURIER_WIDTH)
    w_in_b = w_in[0].astype(BF16)
    g_mix = norm_mix_g[0][None, :]
    tm = min(512, n)
    zr, zi, q_t, k_l, v_t = _inproj(x2d, mod, g_mix, w_in_b, rope_x, bq, bk, gq, gk, w0,
                                    mod_row=0, with_q=True, tm=tm)
    k_c, v_c = _inproj(ctx2d, mod, g_mix, w_in_b, rope_c, bq, bk, gq, gk, w0,
                       mod_row=1, with_q=False, tm=lc)

    attn = _attention(q_t, k_l, v_t, k_c, v_c, tq=min(256, n), tk=min(2048, n // 2))

    br, bi = _dft_a(zr.reshape(na, nb * FOURIER_WIDTH), zi.reshape(na, nb * FOURIER_WIDTH), wa,
                    tw_c, tw_s, cb=dft_cb)
    wf = _block_diag([w_four[0, g] for g in range(N_FOURIER_GROUPS)]).astype(BF16)
    four = _dft_c(br.reshape(na, nb, FOURIER_WIDTH), bi.reshape(na, nb, FOURIER_WIDTH), mm, wf,
                  kb=8).reshape(n, FOURIER_WIDTH)

    w_out_b = w_out[0].astype(BF16)
    wo_f = w_out_b[:FOURIER_WIDTH]
    wo_a = w_out_b[FOURIER_WIDTH:].reshape(N_KV_HEADS, Q_PER_KV * HEAD_DIM, D_MODEL)
    wo_a = jnp.pad(wo_a, ((0, 0), (0, GROUP_PAD - Q_PER_KV * HEAD_DIM), (0, 0)))
    wr_t = w_router[0].T.astype(BF16)
    x1, h2, aff = _outproj(four, attn, x2d, mod, norm_ffn_g[0][None, :], wo_f, wo_a, wr_t, tm=tm)

    r_n = n // LANES
    ut, ones, slt = _prefix_tables(r_n)
    slot3, off3 = _topk(aff.reshape(N_EXPERTS, r_n, LANES), ut, ones, slt, cap=cap)
    off = jnp.concatenate([off3[:, :, 0], jnp.full((N_EXPERTS, 1), cap, jnp.int32)], axis=1)
    slot = slot3.reshape(N_EXPERTS, n)

    tt = (min(512, cap - BF16_SUBLANES) // LANES) * LANES
    xe, gate = _gather(off, slot.reshape(N_EXPERTS, 1, n), aff.reshape(N_EXPERTS, 1, n), h2,
                       cap=cap, tt=tt, eg=GATHER_EXPERTS)
    ye = _ffn(xe, gate, w_gate[0], w_up[0], w_down[0], cap=cap, tf=512, mc=min(1024, cap))
    out = _combine(off, x1, slot.T, mod, final_norm_g[None, :], ye, tt=tt)
    return out[None]
```

```python
import functools
import math

import numpy as np
import jax
import jax.numpy as jnp
from jax import lax
from jax.experimental import pallas as pl
from jax.experimental.pallas import tpu as pltpu

D_MODEL = 1024
GRID_W = 64
HEAD_DIM = 64
N_Q_HEADS = 12
N_KV_HEADS = 4
Q_PER_KV = N_Q_HEADS // N_KV_HEADS
ATTN_WIDTH = N_Q_HEADS * HEAD_DIM
KV_WIDTH = N_KV_HEADS * HEAD_DIM
N_FOURIER_GROUPS = 4
FOURIER_GROUP_DIM = 64
FOURIER_WIDTH = N_FOURIER_GROUPS * FOURIER_GROUP_DIM
IN_WIDTH = FOURIER_WIDTH + ATTN_WIDTH + 2 * KV_WIDTH
ROPE_BASE = 10000.0
N_EXPERTS = 16
EC_CAPACITY_FACTOR = 2
D_EXPERT = 2048
N_MOD = 6
NORM_EPS = 1e-6

LANES = 128
BF16_SUBLANES = 16
GROUP_PAD = 2 * LANES
V_ROWS = HEAD_DIM + BF16_SUBLANES
GATHER_WIN = LANES + BF16_SUBLANES
GATHER_FAST_MAX = LANES
COMBINE_FAST_MAX = LANES - BF16_SUBLANES
GATHER_EXPERTS = 4
INPROJ_SUB = 128
VMEM_LIMIT = 56 * 1024 * 1024

F32 = jnp.float32
BF16 = jnp.bfloat16
F8 = jnp.float8_e4m3fn
F8_MAX = 448.0
F8_LO_SCALE = 16.0
QK_DEPTH = 4 * HEAD_DIM


def _params(sem):
    return pltpu.CompilerParams(dimension_semantics=sem, vmem_limit_bytes=VMEM_LIMIT)


def _mod_kernel(c_ref, w_ref, b_ref, o_ref):
    s = jax.nn.silu(c_ref[...])
    o_ref[...] = jnp.dot(s, w_ref[...], precision=lax.Precision.HIGHEST,
                         preferred_element_type=F32) + b_ref[...]


def _mod(cc, w_mod, b_mod):
    tn = 1024
    n = w_mod.shape[1]
    return pl.pallas_call(
        _mod_kernel,
        grid=(n // tn,),
        in_specs=[pl.BlockSpec((8, D_MODEL), lambda j: (0, 0)),
                  pl.BlockSpec((D_MODEL, tn), lambda j: (0, j)),
                  pl.BlockSpec((1, tn), lambda j: (0, j))],
        out_specs=pl.BlockSpec((8, tn), lambda j: (0, j)),
        out_shape=jax.ShapeDtypeStruct((8, n), F32),
        compiler_params=_params(("arbitrary",)),
        name="mod",
    )(cc, w_mod, b_mod)


def _rms_modulate(x, g, shift, scale):
    ms = jnp.mean(x * x, axis=-1, keepdims=True)
    y = x * lax.rsqrt(ms + NORM_EPS) * g
    return y * (1.0 + scale) + shift


def _head_norm(t, bmat, gain):
    ms = jnp.dot((t * t).astype(BF16), bmat, preferred_element_type=F32)
    return t * lax.rsqrt(ms + NORM_EPS) * gain


def _f8_pair(x, weights):
    inv = 1.0 / F8_LO_SCALE
    hi = jnp.clip(x, -F8_MAX, F8_MAX).astype(F8).astype(F32)
    lo = ((x - hi) * F8_LO_SCALE).astype(F8).astype(F32)
    return (hi, hi * inv, lo, lo * inv) if weights else (hi, lo, hi * inv, lo * inv)


def _rope(t, cos, sin, lo_mask):
    outs = []
    for c in range(t.shape[1] // LANES):
        xc = t[:, c * LANES:(c + 1) * LANES]
        up = pltpu.roll(xc, LANES - 16, axis=1)
        dn = pltpu.roll(xc, 16, axis=1)
        rot = jnp.where(lo_mask, up, dn)
        outs.append(xc * cos + rot * sin)
    return jnp.concatenate(outs, axis=1)


def _inproj_kernel(x_ref, mod_ref, g_ref, w_ref, rcos_ref, rsin_ref, ccos_ref, csin_ref, bq_ref, bk_ref,
                   gq_ref, gk_ref, w0_ref, *out_refs, mod_row, with_q, sub):
    if with_q:
        zr_ref, zi_ref, q_ref, k_ref, v_ref = out_refs
    else:
        k_ref, v_ref = out_refs
    shift = mod_ref[mod_row:mod_row + 1, 0:D_MODEL]
    scale = mod_ref[mod_row:mod_row + 1, D_MODEL:2 * D_MODEL]
    lane = lax.broadcasted_iota(jnp.int32, (sub, LANES), 1)
    lo_mask = (lane % 32) < 16
    k0 = FOURIER_WIDTH + ATTN_WIDTH
    for r0 in range(0, x_ref.shape[0], sub):
        rows = slice(r0, r0 + sub)
        grid_rows = range(r0 // GRID_W, (r0 + sub) // GRID_W)
        h = _rms_modulate(x_ref[rows, :], g_ref[...], shift, scale).astype(BF16)
        cos = jnp.concatenate([rcos_ref[r:r + 1, :] + ccos_ref[...] for r in grid_rows], axis=0)
        sin = jnp.concatenate([rsin_ref[r:r + 1, :] + csin_ref[...] for r in grid_rows], axis=0)
        if with_q:
            p = jnp.dot(h, w_ref[...], preferred_element_type=F32)
            f = p[:, :FOURIER_WIDTH].astype(BF16)
            z = jnp.dot(f, w0_ref[...], preferred_element_type=F32)
            zr_ref[rows, :] = z[:, :FOURIER_WIDTH].astype(BF16)
            zi_ref[rows, :] = z[:, FOURIER_WIDTH:].astype(BF16)
            q = _head_norm(p[:, FOURIER_WIDTH:k0], bq_ref[...], gq_ref[...])
            q = _rope(q, cos, sin, lo_mask)
            for j, piece in enumerate(_f8_pair(q.T, weights=True)):
                q_ref[:, j * HEAD_DIM:(j + 1) * HEAD_DIM, rows] = piece.astype(F8).reshape(
                    N_Q_HEADS, HEAD_DIM, sub)
            pk = p[:, k0:k0 + KV_WIDTH]
            pv = p[:, k0 + KV_WIDTH:]
        else:
            pkv = jnp.dot(h, w_ref[:, k0:], preferred_element_type=F32)
            pk = pkv[:, :KV_WIDTH]
            pv = pkv[:, KV_WIDTH:]
        k = _head_norm(pk, bk_ref[...], gk_ref[...])
        k = _rope(k, cos, sin, lo_mask)
        pieces = _f8_pair(k, weights=False)
        for g in range(N_KV_HEADS):
            cols = slice(g * HEAD_DIM, (g + 1) * HEAD_DIM)
            k_ref[g, rows, :] = jnp.concatenate([pc[:, cols] for pc in pieces], axis=1).astype(F8)
        v_ref[:, 0:HEAD_DIM, rows] = pv.T.astype(BF16).reshape(N_KV_HEADS, HEAD_DIM, sub)
        v_ref[:, HEAD_DIM:V_ROWS, rows] = jnp.ones((N_KV_HEADS, V_ROWS - HEAD_DIM, sub), BF16)


def _inproj(x, mod, g, w_in, rope, bq, bk, gq, gk, w0, *, mod_row, with_q, tm):
    n = x.shape[0]
    rcos, rsin, ccos, csin = rope
    const = lambda shape: pl.BlockSpec(shape, lambda i: (0,) * len(shape))
    in_specs = [pl.BlockSpec((tm, D_MODEL), lambda i: (i, 0)),
                const(mod.shape), const(g.shape), const(w_in.shape),
                pl.BlockSpec((tm // GRID_W, LANES), lambda i: (i, 0)),
                pl.BlockSpec((tm // GRID_W, LANES), lambda i: (i, 0)),
                const(ccos.shape), const(csin.shape),
                const(bq.shape), const(bk.shape), const(gq.shape), const(gk.shape),
                const(w0.shape)]
    kv_specs = [pl.BlockSpec((N_KV_HEADS, tm, QK_DEPTH), lambda i: (0, i, 0)),
                pl.BlockSpec((N_KV_HEADS, V_ROWS, tm), lambda i: (0, 0, i))]
    kv_shapes = [jax.ShapeDtypeStruct((N_KV_HEADS, n, QK_DEPTH), F8),
                 jax.ShapeDtypeStruct((N_KV_HEADS, V_ROWS, n), BF16)]
    if with_q:
        out_specs = [pl.BlockSpec((tm, FOURIER_WIDTH), lambda i: (i, 0)),
                     pl.BlockSpec((tm, FOURIER_WIDTH), lambda i: (i, 0)),
                     pl.BlockSpec((N_Q_HEADS, QK_DEPTH, tm), lambda i: (0, 0, i))] + kv_specs
        out_shapes = [jax.ShapeDtypeStruct((n, FOURIER_WIDTH), BF16),
                      jax.ShapeDtypeStruct((n, FOURIER_WIDTH), BF16),
                      jax.ShapeDtypeStruct((N_Q_HEADS, QK_DEPTH, n), F8)] + kv_shapes
    else:
        out_specs, out_shapes = kv_specs, kv_shapes
    return pl.pallas_call(
        functools.partial(_inproj_kernel, mod_row=mod_row, with_q=with_q, sub=min(tm, INPROJ_SUB)),
        grid=(n // tm,),
        in_specs=in_specs,
        out_specs=out_specs,
        out_shape=out_shapes,
        compiler_params=_params(("arbitrary",)),
        name="inproj_x" if with_q else "inproj_ctx",
    )(x, mod, g, w_in, rcos, rsin, ccos, csin, bq, bk, gq, gk, w0)


def _attn_kernel(q_ref, k_ref, v_ref, kc_ref, vc_ref, o_ref, s_ref, acc_ref, *, tk):
    tq = q_ref.shape[2]
    n = k_ref.shape[1]
    lc = kc_ref.shape[1]
    nch = n // tk
    qs = jnp.concatenate([q_ref[j] for j in range(Q_PER_KV)], axis=1)

    def scores_latent(c, slot):
        start = pl.multiple_of(c * tk, tk)
        s = jnp.dot(k_ref[0, pl.ds(start, tk), :], qs, preferred_element_type=F32)
        s_ref[slot] = s
        return jnp.max(s, axis=0, keepdims=True)

    def scores_context():
        s = jnp.dot(kc_ref[0], qs, preferred_element_type=F32)
        s_ref[0, 0:lc, :] = s
        return jnp.max(s, axis=0, keepdims=True)

    def v_latent(c):
        return v_ref[0, :, pl.ds(pl.multiple_of(c * tk, tk), tk)]

    def consume(slot, rows, vb, m, mx):
        s = s_ref[slot, 0:rows, :]
        m_new = jnp.maximum(m, mx)
        alpha = jnp.exp2(m - m_new)
        p = jnp.exp2(s - m_new).astype(BF16)
        acc_ref[...] = alpha * acc_ref[...] + jnp.dot(vb, p, preferred_element_type=F32)
        return m_new

    def pair(c0, carry, prefetch):
        m, mx0 = carry
        mx1 = scores_latent(c0 + 1, 1)
        m = consume(0, tk, v_latent(c0), m, mx0)
        mx0 = prefetch()
        return consume(1, tk, v_latent(c0 + 1), m, mx1), mx0

    def body(i, carry):
        return pair(2 * i, carry, lambda: scores_latent(2 * i + 2, 0))

    w = Q_PER_KV * tq
    acc_ref[...] = jnp.zeros_like(acc_ref)
    mx0 = scores_latent(0, 0)
    carry = lax.fori_loop(0, nch // 2 - 1, body, (jnp.full((1, w), -1e30, F32), mx0))
    m, mx_ctx = pair(nch - 2, carry, scores_context)
    consume(0, lc, vc_ref[0], m, mx_ctx)
    acc = acc_ref[...]
    o = acc[0:HEAD_DIM] / acc[HEAD_DIM:HEAD_DIM + 1]
    ot = jnp.concatenate([o, jnp.zeros_like(o)], axis=0).T
    h0, h1, h2 = (ot[j * tq:(j + 1) * tq] for j in range(Q_PER_KV))
    o_ref[0] = jnp.concatenate([h0 + pltpu.roll(h1, HEAD_DIM, axis=1), h2], axis=1).astype(BF16)


def _attention(q, k, v, kc, vc, *, tq, tk):
    n = k.shape[1]
    lc = kc.shape[1]
    kd = k.shape[2]
    gw = GROUP_PAD
    return pl.pallas_call(
        functools.partial(_attn_kernel, tk=tk),
        grid=(N_KV_HEADS, n // tq),
        in_specs=[pl.BlockSpec((Q_PER_KV, kd, tq), lambda g, i: (g, 0, i)),
                  pl.BlockSpec((1, n, kd), lambda g, i: (g, 0, 0)),
                  pl.BlockSpec((1, V_ROWS, n), lambda g, i: (g, 0, 0)),
                  pl.BlockSpec((1, lc, kd), lambda g, i: (g, 0, 0)),
                  pl.BlockSpec((1, V_ROWS, lc), lambda g, i: (g, 0, 0))],
        out_specs=pl.BlockSpec((1, tq, gw), lambda g, i: (g, i, 0)),
        out_shape=jax.ShapeDtypeStruct((N_KV_HEADS, n, gw), BF16),
        scratch_shapes=[pltpu.VMEM((2, tk, Q_PER_KV * tq), F32),
                        pltpu.VMEM((V_ROWS, Q_PER_KV * tq), F32)],
        compiler_params=_params(("arbitrary", "arbitrary")),
        name="attention",
    )(q, k, v, kc, vc)


def _dft_a_kernel(zr_ref, zi_ref, w_ref, tc_ref, ts_ref, br_ref, bi_ref):
    na = zr_ref.shape[0]
    fw = FOURIER_WIDTH
    zz = jnp.concatenate([zr_ref[...], zi_ref[...]], axis=0)
    a = jnp.dot(w_ref[...], zz, preferred_element_type=F32)
    tc = tc_ref[0]
    ts = ts_ref[0]
    for j in range(tc.shape[1]):
        ar = a[:na, j * fw:(j + 1) * fw]
        ai = a[na:, j * fw:(j + 1) * fw]
        c = tc[:, j:j + 1]
        s = ts[:, j:j + 1]
        br_ref[:, j * fw:(j + 1) * fw] = (ar * c + ai * s).astype(BF16)
        bi_ref[:, j * fw:(j + 1) * fw] = (ai * c - ar * s).astype(BF16)


def _dft_a(zr2, zi2, wa, tc3, ts3, *, cb):
    na, cols = zr2.shape
    blk = pl.BlockSpec((na, cb), lambda j: (0, j))
    tw = pl.BlockSpec((1,) + tc3.shape[1:], lambda j: (j, 0, 0))
    return pl.pallas_call(
        _dft_a_kernel,
        grid=(cols // cb,),
        in_specs=[blk, blk, pl.BlockSpec(wa.shape, lambda j: (0, 0)), tw, tw],
        out_specs=[blk, blk],
        out_shape=[jax.ShapeDtypeStruct((na, cols), BF16)] * 2,
        compiler_params=_params(("arbitrary",)),
        name="dft_a",
    )(zr2, zi2, wa, tc3, ts3)


def _dft_c_kernel(br_ref, bi_ref, m_ref, wf_ref, o_ref):
    kb = br_ref.shape[0]
    outs = []
    for j in range(kb):
        ab = jnp.concatenate([br_ref[j], bi_ref[j]], axis=0)
        y = jnp.dot(m_ref[...], ab, preferred_element_type=F32)
        outs.append(jnp.dot(y.astype(BF16), wf_ref[...], preferred_element_type=F32))
    o_ref[...] = jnp.concatenate(outs, axis=1).astype(BF16)


def _dft_c(br3, bi3, mm, wf, *, kb):
    na, nb, fw = br3.shape
    blk = pl.BlockSpec((kb, nb, fw), lambda i: (i, 0, 0))
    return pl.pallas_call(
        _dft_c_kernel,
        grid=(na // kb,),
        in_specs=[blk, blk,
                  pl.BlockSpec(mm.shape, lambda i: (0, 0)),
                  pl.BlockSpec(wf.shape, lambda i: (0, 0))],
        out_specs=pl.BlockSpec((nb, kb * fw), lambda i: (0, i)),
        out_shape=jax.ShapeDtypeStruct((nb, na * fw), BF16),
        compiler_params=_params(("arbitrary",)),
        name="dft_c",
    )(br3, bi3, mm, wf)


def _outproj_kernel(four_ref, attn_ref, x_ref, mod_ref, g_ref, wf_ref, wa_ref, wr_ref,
                    x1_ref, h_ref, aff_ref):
    gate1 = mod_ref[0:1, 2 * D_MODEL:3 * D_MODEL]
    shift2 = mod_ref[0:1, 3 * D_MODEL:4 * D_MODEL]
    scale2 = mod_ref[0:1, 4 * D_MODEL:5 * D_MODEL]
    mix = jnp.dot(four_ref[...], wf_ref[...], preferred_element_type=F32)
    for g in range(N_KV_HEADS):
        mix = mix + jnp.dot(attn_ref[g], wa_ref[g], preferred_element_type=F32)
    x1 = x_ref[...] + gate1 * mix
    x1_ref[...] = x1
    h = _rms_modulate(x1, g_ref[...], shift2, scale2).astype(BF16)
    h_ref[...] = h
    logits = lax.dot_general(wr_ref[...], h, (((1,), (1,)), ((), ())),
                             preferred_element_type=F32)
    mx = jnp.max(logits, axis=0, keepdims=True)
    e = jnp.exp(logits - mx)
    aff_ref[...] = e / jnp.sum(e, axis=0, keepdims=True)


def _outproj(four, attn, x, mod, g, wo_f, wo_a, wr_t, *, tm):
    n = x.shape[0]
    gw = GROUP_PAD
    const = lambda shape: pl.BlockSpec(shape, lambda i: (0,) * len(shape))
    return pl.pallas_call(
        _outproj_kernel,
        grid=(n // tm,),
        in_specs=[pl.BlockSpec((tm, FOURIER_WIDTH), lambda i: (i, 0)),
                  pl.BlockSpec((N_KV_HEADS, tm, gw), lambda i: (0, i, 0)),
                  pl.BlockSpec((tm, D_MODEL), lambda i: (i, 0)),
                  const(mod.shape), const(g.shape), const(wo_f.shape), const(wo_a.shape),
                  const(wr_t.shape)],
        out_specs=[pl.BlockSpec((tm, D_MODEL), lambda i: (i, 0)),
                   pl.BlockSpec((tm, D_MODEL), lambda i: (i, 0)),
                   pl.BlockSpec((N_EXPERTS, tm), lambda i: (0, i))],
        out_shape=[jax.ShapeDtypeStruct((n, D_MODEL), F32),
                   jax.ShapeDtypeStruct((n, D_MODEL), BF16),
                   jax.ShapeDtypeStruct((N_EXPERTS, n), F32)],
        compiler_params=_params(("arbitrary",)),
        name="outproj",
    )(four, attn, x, mod, g, wo_f, wo_a, wr_t)


def _count(m):
    return jnp.sum(jnp.sum(m, axis=2, keepdims=True), axis=1, keepdims=True)


def _topk_kernel(aff_ref, ut_ref, ones_ref, slt_ref, slot_ref, off_ref, *, cap):
    e_n, r_n, _ = aff_ref.shape
    aff = aff_ref[...]
    capf = jnp.float32(cap)

    def count_ge(th):
        return _count(jnp.where(aff >= th, 1.0, 0.0))

    def search(i, t):
        cand = t | jnp.left_shift(jnp.int32(1), 30 - i)
        c = count_ge(lax.bitcast_convert_type(cand, F32))
        return jnp.where(c >= capf, cand, t)

    tbits = lax.fori_loop(0, 31, search, jnp.zeros((e_n, 1, 1), jnp.int32))
    thr = lax.bitcast_convert_type(tbits, F32)
    ulp = lax.bitcast_convert_type(tbits + 1, F32) - thr

    def refine(_, carry):
        t, step = carry
        step = step * 0.5
        cand = t + step
        return jnp.where(count_ge(cand) >= capf, cand, t), step

    thr, _ = lax.fori_loop(0, 29, refine, (thr, ulp))
    gt = jnp.where(aff > thr, 1.0, 0.0)
    eq = jnp.where(aff == thr, 1.0, 0.0)
    need = capf - _count(gt)

    def excl_prefix(m):
        m2 = m.reshape(e_n * r_n, LANES).astype(BF16)
        within = jnp.dot(m2, ut_ref[...], preferred_element_type=F32).reshape(e_n, r_n, LANES)
        tot = jnp.dot(m2, ones_ref[...], preferred_element_type=F32).reshape(e_n, r_n, LANES)
        rows = jnp.stack([jnp.dot(slt_ref[...], tot[e].astype(BF16), preferred_element_type=F32)
                          for e in range(e_n)], axis=0)
        return within + rows, rows

    pe, _ = excl_prefix(eq)
    sel = jnp.maximum(gt, eq * jnp.where(pe < need, 1.0, 0.0))
    pos, rows = excl_prefix(sel)
    slot_ref[...] = jnp.where(sel > 0.5, pos, -1.0).astype(jnp.int32)
    off_ref[...] = rows.astype(jnp.int32)


def _topk(aff3, ut, ones, slt, *, cap):
    shp = aff3.shape
    full = lambda a: pl.BlockSpec(a.shape, lambda i: (0,) * a.ndim)
    return pl.pallas_call(
        functools.partial(_topk_kernel, cap=cap),
        grid=(1,),
        in_specs=[full(aff3), full(ut), full(ones), full(slt)],
        out_specs=[pl.BlockSpec(shp, lambda i: (0, 0, 0))] * 2,
        out_shape=[jax.ShapeDtypeStruct(shp, jnp.int32)] * 2,
        compiler_params=_params(("arbitrary",)),
        name="topk",
    )(aff3, ut, ones, slt)


def _window_start(off):
    return pl.multiple_of((off // BF16_SUBLANES) * BF16_SUBLANES, BF16_SUBLANES)


def _tile_max_count(off_ref, experts, t, nsub):
    cnts = [off_ref[e, (t + 1) * nsub] - off_ref[e, t * nsub] for e in experts]
    return functools.reduce(jnp.maximum, cnts)


def _gather_kernel(off_ref, slot_ref, aff_ref, h_ref, xe_ref, gate_ref, *, fast_max):
    p = pl.program_id(0)
    t = pl.program_id(1)
    eg, _, tt = slot_ref.shape
    nsub = tt // LANES
    experts = [p * eg + j for j in range(eg)]

    @pl.when(t == 0)
    def _():
        xe_ref[...] = jnp.zeros_like(xe_ref)
        gate_ref[...] = jnp.zeros_like(gate_ref)

    def scatter_rows(j, ws, hit, res, aff_row):
        xe_ref[j, pl.ds(ws, GATHER_WIN), :] += res.astype(BF16)
        g = jnp.sum(jnp.where(hit, aff_row, 0.0), axis=1, keepdims=True)
        gate_ref[j, pl.ds(ws, GATHER_WIN), :] += jnp.broadcast_to(g, (GATHER_WIN, LANES))

    fast = _tile_max_count(off_ref, experts, t, nsub) <= fast_max

    @pl.when(fast)
    def _():
        rows = lax.broadcasted_iota(jnp.int32, (GATHER_WIN, tt), 0)
        starts = [_window_start(off_ref[e, t * nsub]) for e in experts]
        hits = [rows == (slot_ref[j] - starts[j]) for j in range(eg)]
        onehot = jnp.concatenate([jnp.where(hit, 1.0, 0.0).astype(BF16) for hit in hits], axis=0)
        res = jnp.dot(onehot, h_ref[...], preferred_element_type=F32)
        for j in range(eg):
            scatter_rows(j, starts[j], hits[j], res[j * GATHER_WIN:(j + 1) * GATHER_WIN], aff_ref[j])

    @pl.when(jnp.logical_not(fast))
    def _():
        rows = lax.broadcasted_iota(jnp.int32, (GATHER_WIN, LANES), 0)
        for j in range(eg):
            for b in range(nsub):
                ws = _window_start(off_ref[experts[j], t * nsub + b])
                lanes = slice(b * LANES, (b + 1) * LANES)
                hit = rows == (slot_ref[j, :, lanes] - ws)
                res = jnp.dot(jnp.where(hit, 1.0, 0.0).astype(BF16), h_ref[lanes, :],
                              preferred_element_type=F32)
                scatter_rows(j, ws, hit, res, aff_ref[j, :, lanes])


def _gather(off, slot3, aff3, h, *, cap, tt, eg):
    n = h.shape[0]
    capp = cap + GATHER_WIN
    grid_spec = pltpu.PrefetchScalarGridSpec(
        num_scalar_prefetch=1,
        grid=(N_EXPERTS // eg, n // tt),
        in_specs=[pl.BlockSpec((eg, 1, tt), lambda p, t, off: (p, 0, t)),
                  pl.BlockSpec((eg, 1, tt), lambda p, t, off: (p, 0, t)),
                  pl.BlockSpec((tt, D_MODEL), lambda p, t, off: (t, 0))],
        out_specs=[pl.BlockSpec((eg, capp, D_MODEL), lambda p, t, off: (p, 0, 0)),
                   pl.BlockSpec((eg, capp, LANES), lambda p, t, off: (p, 0, 0))])
    return pl.pallas_call(
        functools.partial(_gather_kernel, fast_max=GATHER_FAST_MAX),
        grid_spec=grid_spec,
        out_shape=[jax.ShapeDtypeStruct((N_EXPERTS, capp, D_MODEL), BF16),
                   jax.ShapeDtypeStruct((N_EXPERTS, capp, LANES), F32)],
        compiler_params=_params(("arbitrary", "arbitrary")),
        name="gather",
    )(off, slot3, aff3, h)


def _ffn_kernel(xe_ref, gate_ref, wg_ref, wu_ref, wd_ref, ye_ref, acc_ref, *, mc):
    f = pl.program_id(1)
    cap = xe_ref.shape[1]
    wg = wg_ref[0].astype(BF16)
    wu = wu_ref[0].astype(BF16)
    wd = wd_ref[0].astype(BF16)

    @pl.when(f == 0)
    def _():
        acc_ref[...] = jnp.zeros_like(acc_ref)

    for m in range(cap // mc):
        rows = slice(m * mc, (m + 1) * mc)
        xm = xe_ref[0, rows, :]
        gate = jnp.dot(xm, wg, preferred_element_type=F32)
        up = jnp.dot(xm, wu, preferred_element_type=F32)
        hid = (jax.nn.silu(gate) * up).astype(BF16)
        acc_ref[rows, :] += jnp.dot(hid, wd, preferred_element_type=F32)

    @pl.when(f == pl.num_programs(1) - 1)
    def _():
        for m in range(cap // mc):
            rows = slice(m * mc, (m + 1) * mc)
            gate = jnp.concatenate([gate_ref[0, rows, :]] * (D_MODEL // LANES), axis=1)
            ye_ref[0, rows, :] = (acc_ref[rows, :] * gate).astype(BF16)


def _ffn(xe, gate, w_gate, w_up, w_down, *, cap, tf, mc):
    e_n = xe.shape[0]
    return pl.pallas_call(
        functools.partial(_ffn_kernel, mc=mc),
        grid=(e_n, D_EXPERT // tf),
        in_specs=[pl.BlockSpec((1, cap, D_MODEL), lambda e, f: (e, 0, 0)),
                  pl.BlockSpec((1, cap, LANES), lambda e, f: (e, 0, 0)),
                  pl.BlockSpec((1, D_MODEL, tf), lambda e, f: (e, 0, f)),
                  pl.BlockSpec((1, D_MODEL, tf), lambda e, f: (e, 0, f)),
                  pl.BlockSpec((1, tf, D_MODEL), lambda e, f: (e, f, 0))],
        out_specs=pl.BlockSpec((1, cap, D_MODEL), lambda e, f: (e, 0, 0)),
        out_shape=jax.ShapeDtypeStruct((e_n, cap, D_MODEL), BF16),
        scratch_shapes=[pltpu.VMEM((cap, D_MODEL), F32)],
        compiler_params=_params(("arbitrary", "arbitrary")),
        name="ffn",
    )(xe, gate, w_gate, w_up, w_down)


def _combine_kernel(off_ref, x1_ref, slot_ref, mod_ref, g_ref, ye_ref, o_ref,
                    wbuf_ref, buf_ref, wsem_ref, sem_ref, *, win, fast_max):
    t = pl.program_id(0)
    tt = x1_ref.shape[0]
    cap = ye_ref.shape[1]
    nsub = tt // LANES
    big = tt + BF16_SUBLANES
    experts = list(range(N_EXPERTS))

    def finish(moe):
        gate2 = mod_ref[0:1, 5 * D_MODEL:6 * D_MODEL]
        x2 = x1_ref[...] + gate2 * moe
        ms = jnp.mean(x2 * x2, axis=-1, keepdims=True)
        o_ref[...] = x2 * lax.rsqrt(ms + NORM_EPS) * g_ref[...]

    def clamped_start(e, rows, tile=t):
        return pl.multiple_of(jnp.minimum(_window_start(off_ref[e, tile * nsub]), cap - rows),
                              BF16_SUBLANES)

    def is_fast(tile):
        return _tile_max_count(off_ref, experts, tile, nsub) <= fast_max

    def window_copies(tile):
        buf = tile % 2
        return [pltpu.make_async_copy(ye_ref.at[e, pl.ds(clamped_start(e, LANES, tile), LANES), :],
                                      wbuf_ref.at[buf, pl.ds(e * LANES, LANES), :], wsem_ref.at[buf])
                for e in experts]

    def start_windows(tile):
        for cp in window_copies(tile):
            cp.start()

    fast = is_fast(t)
    pl.when(jnp.logical_and(t == 0, fast))(lambda: start_windows(t))
    nxt = jnp.minimum(t + 1, pl.num_programs(0) - 1)
    pl.when(jnp.logical_and(t + 1 < pl.num_programs(0), is_fast(nxt)))(lambda: start_windows(nxt))

    @pl.when(fast)
    def _():
        cols = lax.broadcasted_iota(jnp.int32, (tt, LANES), 1)
        onehot = jnp.concatenate(
            [jnp.where(cols == (slot_ref[:, e:e + 1] - clamped_start(e, LANES)), 1.0, 0.0).astype(BF16)
             for e in experts], axis=1)
        for cp in window_copies(t):
            cp.wait()
        finish(jnp.dot(onehot, wbuf_ref[t % 2], preferred_element_type=F32))

    @pl.when(jnp.logical_not(fast))
    def _():
        def copy(e, slot):
            return pltpu.make_async_copy(ye_ref.at[e, pl.ds(clamped_start(e, big), big), :],
                                         buf_ref.at[slot], sem_ref.at[slot])

        copy(0, 0).start()
        cols = lax.broadcasted_iota(jnp.int32, (LANES, win), 1)
        accs = [jnp.zeros((LANES, D_MODEL), F32) for _ in range(nsub)]
        for e in experts:
            slot = e % 2
            if e + 1 < N_EXPERTS:
                copy(e + 1, 1 - slot).start()
            copy(e, slot).wait()
            wb = clamped_start(e, big)
            for j in range(nsub):
                ws = jnp.minimum(_window_start(off_ref[e, t * nsub + j]), wb + (big - win))
                rel = slot_ref[j * LANES:(j + 1) * LANES, e:e + 1] - ws
                onehot = jnp.where(cols == rel, 1.0, 0.0).astype(BF16)
                r0 = pl.multiple_of(ws - wb, BF16_SUBLANES)
                accs[j] = accs[j] + jnp.dot(onehot, buf_ref[slot, pl.ds(r0, win), :],
                                            preferred_element_type=F32)
        finish(jnp.concatenate(accs, axis=0))


def _combine(off, x1, slot_t, mod, g, ye, *, tt):
    n = x1.shape[0]
    win = min(2 * LANES, tt + BF16_SUBLANES)
    grid_spec = pltpu.PrefetchScalarGridSpec(
        num_scalar_prefetch=1,
        grid=(n // tt,),
        in_specs=[pl.BlockSpec((tt, D_MODEL), lambda t, off: (t, 0)),
                  pl.BlockSpec((tt, N_EXPERTS), lambda t, off: (t, 0)),
                  pl.BlockSpec(mod.shape, lambda t, off: (0, 0)),
                  pl.BlockSpec(g.shape, lambda t, off: (0, 0)),
                  pl.BlockSpec(memory_space=pl.ANY)],
        out_specs=pl.BlockSpec((tt, D_MODEL), lambda t, off: (t, 0)),
        scratch_shapes=[pltpu.VMEM((2, N_EXPERTS * LANES, D_MODEL), BF16),
                        pltpu.VMEM((2, tt + BF16_SUBLANES, D_MODEL), BF16),
                        pltpu.SemaphoreType.DMA((2,)),
                        pltpu.SemaphoreType.DMA((2,))])
    return pl.pallas_call(
        functools.partial(_combine_kernel, win=win, fast_max=COMBINE_FAST_MAX),
        grid_spec=grid_spec,
        out_shape=jax.ShapeDtypeStruct((n, D_MODEL), F32),
        compiler_params=_params(("arbitrary",)),
        name="combine",
    )(off, x1, slot_t, mod, g, ye)


def _rope_tables(n):
    rows = n // GRID_W
    half = HEAD_DIM // 4
    freqs = ROPE_BASE ** (-jnp.arange(half, dtype=F32) / half)
    ang_r = jnp.arange(rows, dtype=jnp.int32).astype(F32)[:, None] * freqs[None, :]
    ang_c = jnp.arange(GRID_W, dtype=jnp.int32).astype(F32)[:, None] * freqs[None, :]
    zr = jnp.zeros((rows, 2 * half), F32)
    zc = jnp.zeros((GRID_W, 2 * half), F32)
    head = lambda a, b: jnp.concatenate([a, b] * 2, axis=1)
    rcos = head(jnp.concatenate([jnp.cos(ang_r)] * 2, axis=1), zr)
    rsin = head(jnp.concatenate([-jnp.sin(ang_r), jnp.sin(ang_r)], axis=1), zr)
    ccos = head(zc, jnp.concatenate([jnp.cos(ang_c)] * 2, axis=1))
    csin = head(zc, jnp.concatenate([-jnp.sin(ang_c), jnp.sin(ang_c)], axis=1))
    return rcos, rsin, ccos, csin


def _block_diag(blocks):
    k = len(blocks)
    r, c = blocks[0].shape
    out = jnp.zeros((k * r, k * c), blocks[0].dtype)
    for i, b in enumerate(blocks):
        out = out.at[i * r:(i + 1) * r, i * c:(i + 1) * c].set(b)
    return out


def _dft_tables(n, cb_n2):
    nb = LANES
    na = n // nb
    gd = FOURIER_GROUP_DIM
    scale = 1.0 / math.sqrt(n * gd)

    def cos_sin(i, j, period):
        ang = ((i * j) % period).astype(F32) * (2.0 * math.pi / period)
        return jnp.cos(ang), jnp.sin(ang)

    ar_g = jnp.arange(gd, dtype=jnp.int32)
    c64, s64 = cos_sin(ar_g[:, None], ar_g[None, :], gd)
    w0 = jnp.concatenate([_block_diag([c64 * scale] * N_FOURIER_GROUPS),
                          _block_diag([-s64 * scale] * N_FOURIER_GROUPS)], axis=1)
    ar_a = jnp.arange(na, dtype=jnp.int32)
    ca, sa = cos_sin(ar_a[:, None], ar_a[None, :], na)
    wa = jnp.concatenate([jnp.concatenate([ca, sa], axis=1),
                          jnp.concatenate([-sa, ca], axis=1)], axis=0)
    ar_b = jnp.arange(nb, dtype=jnp.int32)
    ct, st = cos_sin(ar_a[:, None], ar_b[None, :], n)
    group = lambda t: t.reshape(na, nb // cb_n2, cb_n2).transpose(1, 0, 2)
    cc, sc = cos_sin(ar_b[:, None], ar_b[None, :], nb)
    mm = jnp.concatenate([cc, sc], axis=1)
    return w0.astype(BF16), wa.astype(BF16), group(ct), group(st), mm.astype(BF16)


def _prefix_tables(r_n):
    i = np.arange(LANES)
    ut = (i[:, None] < i[None, :]).astype(np.float32)
    ones = np.ones((LANES, LANES), np.float32)
    r = np.arange(r_n)
    slt = (r[None, :] < r[:, None]).astype(np.float32)
    as_bf16 = lambda a: jnp.asarray(a).astype(BF16)
    return as_bf16(ut), as_bf16(ones), as_bf16(slt)


def kernel(x, c, ctx, c_ctx, w_mod, b_mod, norm_mix_g, norm_ffn_g, w_in, w_four, q_norm_g, k_norm_g,
           w_out, w_router, w_gate, w_up, w_down, final_norm_g):
    batch, n, _ = x.shape
    assert batch == 1 and w_mod.shape[0] == 1
    lc = ctx.shape[1]
    cap = EC_CAPACITY_FACTOR * n // N_EXPERTS
    x2d = x[0]
    ctx2d = ctx[0]

    cc = jnp.zeros((8, D_MODEL), F32).at[0].set(c[0]).at[1].set(c_ctx)
    mod = _mod(cc, w_mod[0], b_mod[0][None, :])

    rope_x = _rope_tables(n)
    zero_c = jnp.zeros((GRID_W, LANES), F32)
    rope_c = (jnp.ones((lc // GRID_W, LANES), F32), jnp.zeros((lc // GRID_W, LANES), F32), zero_c, zero_c)
    inv_hd = jnp.full((HEAD_DIM, HEAD_DIM), 1.0 / HEAD_DIM, BF16)
    bq = _block_diag([inv_hd] * N_Q_HEADS)
    bk = _block_diag([inv_hd] * N_KV_HEADS)
    q_scale = (HEAD_DIM ** -0.5) * math.log2(math.e)
    ratio = jnp.max(jnp.abs(k_norm_g[0])) / jnp.maximum(jnp.max(jnp.abs(q_norm_g[0])) * q_scale, 1e-30)
    shift = jnp.clip(jnp.round(0.5 * jnp.log2(jnp.maximum(ratio, 1e-30))), -60, 60).astype(jnp.int32)
    pow2 = lambda e: lax.bitcast_convert_type((e + 127) << 23, F32)
    gq = jnp.tile(q_norm_g[0], N_Q_HEADS)[None, :] * q_scale * pow2(shift)
    gk = jnp.tile(k_norm_g[0], N_KV_HEADS)[None, :] * pow2(-shift)
    nb = LANES
    na = n // nb
    dft_cb = min(4096, nb * FOURIER_WIDTH)
    w0, wa, tw_c, tw_s, mm = _dft_tables(n, dft_cb // FOURIER_WIDTH)
    w_in_b = w_in[0].astype(BF16)
    g_mix = norm_mix_g[0][None, :]
    tm = min(512, n)
    zr, zi, q_t, k_l, v_t = _inproj(x2d, mod, g_mix, w_in_b, rope_x, bq, bk, gq, gk, w0,
                                    mod_row=0, with_q=True, tm=tm)
    k_c, v_c = _inproj(ctx2d, mod, g_mix, w_in_b, rope_c, bq, bk, gq, gk, w0,
                       mod_row=1, with_q=False, tm=lc)

    attn = _attention(q_t, k_l, v_t, k_c, v_c, tq=min(256, n), tk=min(2048, n // 2))

    br, bi = _dft_a(zr.reshape(na, nb * FOURIER_WIDTH), zi.reshape(na, nb * FOURIER_WIDTH), wa,
                    tw_c, tw_s, cb=dft_cb)
    wf = _block_diag([w_four[0, g] for g in range(N_FOURIER_GROUPS)]).astype(BF16)
    four = _dft_c(br.reshape(na, nb, FOURIER_WIDTH), bi.reshape(na, nb, FOURIER_WIDTH), mm, wf,
                  kb=8).reshape(n, FOURIER_WIDTH)

    w_out_b = w_out[0].astype(BF16)
    wo_f = w_out_b[:FOURIER_WIDTH]
    wo_a = w_out_b[FOURIER_WIDTH:].reshape(N_KV_HEADS, Q_PER_KV * HEAD_DIM, D_MODEL)
    wo_a = jnp.pad(wo_a, ((0, 0), (0, GROUP_PAD - Q_PER_KV * HEAD_DIM), (0, 0)))
    wr_t = w_router[0].T.astype(BF16)
    x1, h2, aff = _outproj(four, attn, x2d, mod, norm_ffn_g[0][None, :], wo_f, wo_a, wr_t, tm=tm)

    r_n = n // LANES
    ut, ones, slt = _prefix_tables(r_n)
    slot3, off3 = _topk(aff.reshape(N_EXPERTS, r_n, LANES), ut, ones, slt, cap=cap)
    off = jnp.concatenate([off3[:, :, 0], jnp.full((N_EXPERTS, 1), cap, jnp.int32)], axis=1)
    slot = slot3.reshape(N_EXPERTS, n)

    tt = (min(512, cap - BF16_SUBLANES) // LANES) * LANES
    xe, gate = _gather(off, slot.reshape(N_EXPERTS, 1, n), aff.reshape(N_EXPERTS, 1, n), h2,
                       cap=cap, tt=tt, eg=GATHER_EXPERTS)
    ye = _ffn(xe, gate, w_gate[0], w_up[0], w_down[0], cap=cap, tf=512, mc=min(1024, cap))
    out = _combine(off, x1, slot.T, mod, final_norm_g[None, :], ye, tt=tt)
    return out[None]
```

```python
import functools
import math

import numpy as np
import jax
import jax.numpy as jnp
from jax import lax
from jax.experimental import pallas as pl
from jax.experimental.pallas import tpu as pltpu

D_MODEL = 1024
GRID_W = 64
HEAD_DIM = 64
N_Q_HEADS = 12
N_KV_HEADS = 4
Q_PER_KV = N_Q_HEADS // N_KV_HEADS
ATTN_WIDTH = N_Q_HEADS * HEAD_DIM
KV_WIDTH = N_KV_HEADS * HEAD_DIM
N_FOURIER_GROUPS = 4
FOURIER_GROUP_DIM = 64
FOURIER_WIDTH = N_FOURIER_GROUPS * FOURIER_GROUP_DIM
IN_WIDTH = FOURIER_WIDTH + ATTN_WIDTH + 2 * KV_WIDTH
ROPE_BASE = 10000.0
N_EXPERTS = 16
EC_CAPACITY_FACTOR = 2
D_EXPERT = 2048
N_MOD = 6
NORM_EPS = 1e-6

LANES = 128
BF16_SUBLANES = 16
GROUP_PAD = 2 * LANES
V_ROWS = HEAD_DIM + BF16_SUBLANES
GATHER_WIN = LANES + BF16_SUBLANES
GATHER_FAST_MAX = LANES
COMBINE_FAST_MAX = LANES - BF16_SUBLANES
GATHER_EXPERTS = 4
GATHER_TILES = 2
INPROJ_SUB = 128
VMEM_LIMIT = 56 * 1024 * 1024

F32 = jnp.float32
BF16 = jnp.bfloat16
F8 = jnp.float8_e4m3fn
F8_MAX = 448.0
F8_LO_SCALE = 16.0
QK_DEPTH = 4 * HEAD_DIM


def _params(sem):
    return pltpu.CompilerParams(dimension_semantics=sem, vmem_limit_bytes=VMEM_LIMIT)


def _mod_kernel(c_ref, w_ref, b_ref, o_ref):
    s = jax.nn.silu(c_ref[...])
    o_ref[...] = jnp.dot(s, w_ref[...], precision=lax.Precision.HIGHEST,
                         preferred_element_type=F32) + b_ref[...]


def _mod(cc, w_mod, b_mod):
    tn = 1024
    n = w_mod.shape[1]
    return pl.pallas_call(
        _mod_kernel,
        grid=(n // tn,),
        in_specs=[pl.BlockSpec((8, D_MODEL), lambda j: (0, 0)),
                  pl.BlockSpec((D_MODEL, tn), lambda j: (0, j)),
                  pl.BlockSpec((1, tn), lambda j: (0, j))],
        out_specs=pl.BlockSpec((8, tn), lambda j: (0, j)),
        out_shape=jax.ShapeDtypeStruct((8, n), F32),
        compiler_params=_params(("arbitrary",)),
        name="mod",
    )(cc, w_mod, b_mod)


def _rms_modulate(x, g, shift, scale):
    ms = jnp.mean(x * x, axis=-1, keepdims=True)
    y = x * lax.rsqrt(ms + NORM_EPS) * g
    return y * (1.0 + scale) + shift


def _head_norm(t, bmat, gain):
    ms = jnp.dot((t * t).astype(BF16), bmat, preferred_element_type=F32)
    return t * lax.rsqrt(ms + NORM_EPS) * gain


def _f8_pair(x, weights):
    inv = 1.0 / F8_LO_SCALE
    hi = jnp.clip(x, -F8_MAX, F8_MAX).astype(F8).astype(F32)
    lo = ((x - hi) * F8_LO_SCALE).astype(F8).astype(F32)
    return (hi, hi * inv, lo, lo * inv) if weights else (hi, lo, hi * inv, lo * inv)


def _rope(t, cos, sin, lo_mask):
    outs = []
    for c in range(t.shape[1] // LANES):
        xc = t[:, c * LANES:(c + 1) * LANES]
        up = pltpu.roll(xc, LANES - 16, axis=1)
        dn = pltpu.roll(xc, 16, axis=1)
        rot = jnp.where(lo_mask, up, dn)
        outs.append(xc * cos + rot * sin)
    return jnp.concatenate(outs, axis=1)


def _inproj_kernel(x_ref, mod_ref, g_ref, w_ref, rcos_ref, rsin_ref, ccos_ref, csin_ref, bq_ref, bk_ref,
                   gq_ref, gk_ref, w0_ref, *out_refs, mod_row, with_q, sub):
    if with_q:
        zr_ref, zi_ref, q_ref, k_ref, v_ref = out_refs
    else:
        k_ref, v_ref = out_refs
    shift = mod_ref[mod_row:mod_row + 1, 0:D_MODEL]
    scale = mod_ref[mod_row:mod_row + 1, D_MODEL:2 * D_MODEL]
    lane = lax.broadcasted_iota(jnp.int32, (sub, LANES), 1)
    lo_mask = (lane % 32) < 16
    k0 = FOURIER_WIDTH + ATTN_WIDTH
    for r0 in range(0, x_ref.shape[0], sub):
        rows = slice(r0, r0 + sub)
        grid_rows = range(r0 // GRID_W, (r0 + sub) // GRID_W)
        h = _rms_modulate(x_ref[rows, :], g_ref[...], shift, scale).astype(BF16)
        cos = jnp.concatenate([rcos_ref[r:r + 1, :] + ccos_ref[...] for r in grid_rows], axis=0)
        sin = jnp.concatenate([rsin_ref[r:r + 1, :] + csin_ref[...] for r in grid_rows], axis=0)
        if with_q:
            p = jnp.dot(h, w_ref[...], preferred_element_type=F32)
            f = p[:, :FOURIER_WIDTH].astype(BF16)
            z = jnp.dot(f, w0_ref[...], preferred_element_type=F32)
            zr_ref[rows, :] = z[:, :FOURIER_WIDTH].astype(BF16)
            zi_ref[rows, :] = z[:, FOURIER_WIDTH:].astype(BF16)
            q = _head_norm(p[:, FOURIER_WIDTH:k0], bq_ref[...], gq_ref[...])
            q = _rope(q, cos, sin, lo_mask)
            for j, piece in enumerate(_f8_pair(q.T, weights=True)):
                q_ref[:, j * HEAD_DIM:(j + 1) * HEAD_DIM, rows] = piece.astype(F8).reshape(
                    N_Q_HEADS, HEAD_DIM, sub)
            pk = p[:, k0:k0 + KV_WIDTH]
            pv = p[:, k0 + KV_WIDTH:]
        else:
            pkv = jnp.dot(h, w_ref[:, k0:], preferred_element_type=F32)
            pk = pkv[:, :KV_WIDTH]
            pv = pkv[:, KV_WIDTH:]
        k = _head_norm(pk, bk_ref[...], gk_ref[...])
        k = _rope(k, cos, sin, lo_mask)
        pieces = _f8_pair(k, weights=False)
        for g in range(N_KV_HEADS):
            cols = slice(g * HEAD_DIM, (g + 1) * HEAD_DIM)
            k_ref[g, rows, :] = jnp.concatenate([pc[:, cols] for pc in pieces], axis=1).astype(F8)
        v_ref[:, 0:HEAD_DIM, rows] = pv.T.astype(BF16).reshape(N_KV_HEADS, HEAD_DIM, sub)
        v_ref[:, HEAD_DIM:V_ROWS, rows] = jnp.ones((N_KV_HEADS, V_ROWS - HEAD_DIM, sub), BF16)


def _inproj(x, mod, g, w_in, rope, bq, bk, gq, gk, w0, *, mod_row, with_q, tm):
    n = x.shape[0]
    rcos, rsin, ccos, csin = rope
    const = lambda shape: pl.BlockSpec(shape, lambda i: (0,) * len(shape))
    in_specs = [pl.BlockSpec((tm, D_MODEL), lambda i: (i, 0)),
                const(mod.shape), const(g.shape), const(w_in.shape),
                pl.BlockSpec((tm // GRID_W, LANES), lambda i: (i, 0)),
                pl.BlockSpec((tm // GRID_W, LANES), lambda i: (i, 0)),
                const(ccos.shape), const(csin.shape),
                const(bq.shape), const(bk.shape), const(gq.shape), const(gk.shape),
                const(w0.shape)]
    kv_specs = [pl.BlockSpec((N_KV_HEADS, tm, QK_DEPTH), lambda i: (0, i, 0)),
                pl.BlockSpec((N_KV_HEADS, V_ROWS, tm), lambda i: (0, 0, i))]
    kv_shapes = [jax.ShapeDtypeStruct((N_KV_HEADS, n, QK_DEPTH), F8),
                 jax.ShapeDtypeStruct((N_KV_HEADS, V_ROWS, n), BF16)]
    if with_q:
        out_specs = [pl.BlockSpec((tm, FOURIER_WIDTH), lambda i: (i, 0)),
                     pl.BlockSpec((tm, FOURIER_WIDTH), lambda i: (i, 0)),
                     pl.BlockSpec((N_Q_HEADS, QK_DEPTH, tm), lambda i: (0, 0, i))] + kv_specs
        out_shapes = [jax.ShapeDtypeStruct((n, FOURIER_WIDTH), BF16),
                      jax.ShapeDtypeStruct((n, FOURIER_WIDTH), BF16),
                      jax.ShapeDtypeStruct((N_Q_HEADS, QK_DEPTH, n), F8)] + kv_shapes
    else:
        out_specs, out_shapes = kv_specs, kv_shapes
    return pl.pallas_call(
        functools.partial(_inproj_kernel, mod_row=mod_row, with_q=with_q, sub=min(tm, INPROJ_SUB)),
        grid=(n // tm,),
        in_specs=in_specs,
        out_specs=out_specs,
        out_shape=out_shapes,
        compiler_params=_params(("arbitrary",)),
        name="inproj_x" if with_q else "inproj_ctx",
    )(x, mod, g, w_in, rcos, rsin, ccos, csin, bq, bk, gq, gk, w0)


def _attn_kernel(q_ref, k_ref, v_ref, kc_ref, vc_ref, o_ref, s_ref, acc_ref, *, tk):
    tq = q_ref.shape[2]
    n = k_ref.shape[1]
    lc = kc_ref.shape[1]
    nch = n // tk
    qs = jnp.concatenate([q_ref[j] for j in range(Q_PER_KV)], axis=1)

    def scores_latent(c, slot):
        start = pl.multiple_of(c * tk, tk)
        s = jnp.dot(k_ref[0, pl.ds(start, tk), :], qs, preferred_element_type=F32)
        s_ref[slot] = s
        return jnp.max(s, axis=0, keepdims=True)

    def scores_context():
        s = jnp.dot(kc_ref[0], qs, preferred_element_type=F32)
        s_ref[0, 0:lc, :] = s
        return jnp.max(s, axis=0, keepdims=True)

    def v_latent(c):
        return v_ref[0, :, pl.ds(pl.multiple_of(c * tk, tk), tk)]

    def consume(slot, rows, vb, m, mx):
        s = s_ref[slot, 0:rows, :]
        m_new = jnp.maximum(m, mx)
        alpha = jnp.exp2(m - m_new)
        p = jnp.exp2(s - m_new).astype(BF16)
        acc_ref[...] = alpha * acc_ref[...] + jnp.dot(vb, p, preferred_element_type=F32)
        return m_new

    def pair(c0, carry, prefetch):
        m, mx0 = carry
        mx1 = scores_latent(c0 + 1, 1)
        m = consume(0, tk, v_latent(c0), m, mx0)
        mx0 = prefetch()
        return consume(1, tk, v_latent(c0 + 1), m, mx1), mx0

    def body(i, carry):
        return pair(2 * i, carry, lambda: scores_latent(2 * i + 2, 0))

    w = Q_PER_KV * tq
    acc_ref[...] = jnp.zeros_like(acc_ref)
    mx0 = scores_latent(0, 0)
    carry = lax.fori_loop(0, nch // 2 - 1, body, (jnp.full((1, w), -1e30, F32), mx0))
    m, mx_ctx = pair(nch - 2, carry, scores_context)
    consume(0, lc, vc_ref[0], m, mx_ctx)
    acc = acc_ref[...]
    o = acc[0:HEAD_DIM] / acc[HEAD_DIM:HEAD_DIM + 1]
    ot = jnp.concatenate([o, jnp.zeros_like(o)], axis=0).T
    h0, h1, h2 = (ot[j * tq:(j + 1) * tq] for j in range(Q_PER_KV))
    o_ref[0] = jnp.concatenate([h0 + pltpu.roll(h1, HEAD_DIM, axis=1), h2], axis=1).astype(BF16)


def _attention(q, k, v, kc, vc, *, tq, tk):
    n = k.shape[1]
    lc = kc.shape[1]
    kd = k.shape[2]
    gw = GROUP_PAD
    return pl.pallas_call(
        functools.partial(_attn_kernel, tk=tk),
        grid=(N_KV_HEADS, n // tq),
        in_specs=[pl.BlockSpec((Q_PER_KV, kd, tq), lambda g, i: (g, 0, i)),
                  pl.BlockSpec((1, n, kd), lambda g, i: (g, 0, 0)),
                  pl.BlockSpec((1, V_ROWS, n), lambda g, i: (g, 0, 0)),
                  pl.BlockSpec((1, lc, kd), lambda g, i: (g, 0, 0)),
                  pl.BlockSpec((1, V_ROWS, lc), lambda g, i: (g, 0, 0))],
        out_specs=pl.BlockSpec((1, tq, gw), lambda g, i: (g, i, 0)),
        out_shape=jax.ShapeDtypeStruct((N_KV_HEADS, n, gw), BF16),
        scratch_shapes=[pltpu.VMEM((2, tk, Q_PER_KV * tq), F32),
                        pltpu.VMEM((V_ROWS, Q_PER_KV * tq), F32)],
        compiler_params=_params(("arbitrary", "arbitrary")),
        name="attention",
    )(q, k, v, kc, vc)


def _dft_a_kernel(zr_ref, zi_ref, w_ref, tc_ref, ts_ref, br_ref, bi_ref):
    na = zr_ref.shape[0]
    fw = FOURIER_WIDTH
    zz = jnp.concatenate([zr_ref[...], zi_ref[...]], axis=0)
    a = jnp.dot(w_ref[...], zz, preferred_element_type=F32)
    tc = tc_ref[0]
    ts = ts_ref[0]
    for j in range(tc.shape[1]):
        ar = a[:na, j * fw:(j + 1) * fw]
        ai = a[na:, j * fw:(j + 1) * fw]
        c = tc[:, j:j + 1]
        s = ts[:, j:j + 1]
        br_ref[:, j * fw:(j + 1) * fw] = (ar * c + ai * s).astype(BF16)
        bi_ref[:, j * fw:(j + 1) * fw] = (ai * c - ar * s).astype(BF16)


def _dft_a(zr2, zi2, wa, tc3, ts3, *, cb):
    na, cols = zr2.shape
    blk = pl.BlockSpec((na, cb), lambda j: (0, j))
    tw = pl.BlockSpec((1,) + tc3.shape[1:], lambda j: (j, 0, 0))
    return pl.pallas_call(
        _dft_a_kernel,
        grid=(cols // cb,),
        in_specs=[blk, blk, pl.BlockSpec(wa.shape, lambda j: (0, 0)), tw, tw],
        out_specs=[blk, blk],
        out_shape=[jax.ShapeDtypeStruct((na, cols), BF16)] * 2,
        compiler_params=_params(("arbitrary",)),
        name="dft_a",
    )(zr2, zi2, wa, tc3, ts3)


def _dft_c_kernel(br_ref, bi_ref, m_ref, wf_ref, o_ref):
    kb = br_ref.shape[0]
    outs = []
    for j in range(kb):
        ab = jnp.concatenate([br_ref[j], bi_ref[j]], axis=0)
        y = jnp.dot(m_ref[...], ab, preferred_element_type=F32)
        outs.append(jnp.dot(y.astype(BF16), wf_ref[...], preferred_element_type=F32))
    o_ref[...] = jnp.concatenate(outs, axis=1).astype(BF16)


def _dft_c(br3, bi3, mm, wf, *, kb):
    na, nb, fw = br3.shape
    blk = pl.BlockSpec((kb, nb, fw), lambda i: (i, 0, 0))
    return pl.pallas_call(
        _dft_c_kernel,
        grid=(na // kb,),
        in_specs=[blk, blk,
                  pl.BlockSpec(mm.shape, lambda i: (0, 0)),
                  pl.BlockSpec(wf.shape, lambda i: (0, 0))],
        out_specs=pl.BlockSpec((nb, kb * fw), lambda i: (0, i)),
        out_shape=jax.ShapeDtypeStruct((nb, na * fw), BF16),
        compiler_params=_params(("arbitrary",)),
        name="dft_c",
    )(br3, bi3, mm, wf)


def _outproj_kernel(four_ref, attn_ref, x_ref, mod_ref, g_ref, wf_ref, wa_ref, wr_ref,
                    x1_ref, h_ref, aff_ref):
    gate1 = mod_ref[0:1, 2 * D_MODEL:3 * D_MODEL]
    shift2 = mod_ref[0:1, 3 * D_MODEL:4 * D_MODEL]
    scale2 = mod_ref[0:1, 4 * D_MODEL:5 * D_MODEL]
    mix = jnp.dot(four_ref[...], wf_ref[...], preferred_element_type=F32)
    for g in range(N_KV_HEADS):
        mix = mix + jnp.dot(attn_ref[g], wa_ref[g], preferred_element_type=F32)
    x1 = x_ref[...] + gate1 * mix
    x1_ref[...] = x1
    h = _rms_modulate(x1, g_ref[...], shift2, scale2).astype(BF16)
    h_ref[...] = h
    logits = lax.dot_general(wr_ref[...], h, (((1,), (1,)), ((), ())),
                             preferred_element_type=F32)
    mx = jnp.max(logits, axis=0, keepdims=True)
    e = jnp.exp(logits - mx)
    aff_ref[...] = e / jnp.sum(e, axis=0, keepdims=True)


def _outproj(four, attn, x, mod, g, wo_f, wo_a, wr_t, *, tm):
    n = x.shape[0]
    gw = GROUP_PAD
    const = lambda shape: pl.BlockSpec(shape, lambda i: (0,) * len(shape))
    return pl.pallas_call(
        _outproj_kernel,
        grid=(n // tm,),
        in_specs=[pl.BlockSpec((tm, FOURIER_WIDTH), lambda i: (i, 0)),
                  pl.BlockSpec((N_KV_HEADS, tm, gw), lambda i: (0, i, 0)),
                  pl.BlockSpec((tm, D_MODEL), lambda i: (i, 0)),
                  const(mod.shape), const(g.shape), const(wo_f.shape), const(wo_a.shape),
                  const(wr_t.shape)],
        out_specs=[pl.BlockSpec((tm, D_MODEL), lambda i: (i, 0)),
                   pl.BlockSpec((tm, D_MODEL), lambda i: (i, 0)),
                   pl.BlockSpec((N_EXPERTS, tm), lambda i: (0, i))],
        out_shape=[jax.ShapeDtypeStruct((n, D_MODEL), F32),
                   jax.ShapeDtypeStruct((n, D_MODEL), BF16),
                   jax.ShapeDtypeStruct((N_EXPERTS, n), F32)],
        compiler_params=_params(("arbitrary",)),
        name="outproj",
    )(four, attn, x, mod, g, wo_f, wo_a, wr_t)


def _count(m):
    return jnp.sum(jnp.sum(m, axis=2, keepdims=True), axis=1, keepdims=True)


def _topk_kernel(aff_ref, ut_ref, ones_ref, slt_ref, slot_ref, off_ref, *, cap):
    e_n, r_n, _ = aff_ref.shape
    aff = aff_ref[...]
    capf = jnp.float32(cap)

    def count_ge(th):
        return _count(jnp.where(aff >= th, 1.0, 0.0))

    def search(i, t):
        cand = t | jnp.left_shift(jnp.int32(1), 30 - i)
        c = count_ge(lax.bitcast_convert_type(cand, F32))
        return jnp.where(c >= capf, cand, t)

    tbits = lax.fori_loop(0, 31, search, jnp.zeros((e_n, 1, 1), jnp.int32))
    thr = lax.bitcast_convert_type(tbits, F32)
    ulp = lax.bitcast_convert_type(tbits + 1, F32) - thr

    def refine(_, carry):
        t, step = carry
        step = step * 0.5
        cand = t + step
        return jnp.where(count_ge(cand) >= capf, cand, t), step

    thr, _ = lax.fori_loop(0, 29, refine, (thr, ulp))
    gt = jnp.where(aff > thr, 1.0, 0.0)
    eq = jnp.where(aff == thr, 1.0, 0.0)
    need = capf - _count(gt)

    def excl_prefix(m):
        m2 = m.reshape(e_n * r_n, LANES).astype(BF16)
        within = jnp.dot(m2, ut_ref[...], preferred_element_type=F32).reshape(e_n, r_n, LANES)
        tot = jnp.dot(m2, ones_ref[...], preferred_element_type=F32).reshape(e_n, r_n, LANES)
        rows = jnp.stack([jnp.dot(slt_ref[...], tot[e].astype(BF16), preferred_element_type=F32)
                          for e in range(e_n)], axis=0)
        return within + rows, rows

    pe, _ = excl_prefix(eq)
    sel = jnp.maximum(gt, eq * jnp.where(pe < need, 1.0, 0.0))
    pos, rows = excl_prefix(sel)
    slot_ref[...] = jnp.where(sel > 0.5, pos, -1.0).astype(jnp.int32)
    off_ref[...] = rows.astype(jnp.int32)


def _topk(aff3, ut, ones, slt, *, cap):
    shp = aff3.shape
    full = lambda a: pl.BlockSpec(a.shape, lambda i: (0,) * a.ndim)
    return pl.pallas_call(
        functools.partial(_topk_kernel, cap=cap),
        grid=(1,),
        in_specs=[full(aff3), full(ut), full(ones), full(slt)],
        out_specs=[pl.BlockSpec(shp, lambda i: (0, 0, 0))] * 2,
        out_shape=[jax.ShapeDtypeStruct(shp, jnp.int32)] * 2,
        compiler_params=_params(("arbitrary",)),
        name="topk",
    )(aff3, ut, ones, slt)


def _window_start(off):
    return pl.multiple_of((off // BF16_SUBLANES) * BF16_SUBLANES, BF16_SUBLANES)


def _tile_max_count(off_ref, experts, t, nsub):
    cnts = [off_ref[e, (t + 1) * nsub] - off_ref[e, t * nsub] for e in experts]
    return functools.reduce(jnp.maximum, cnts)


def _gather_kernel(off_ref, slot_ref, aff_ref, h_ref, xe_ref, gate_ref, *, tt, fast_max):
    p = pl.program_id(0)
    t = pl.program_id(1)
    eg = slot_ref.shape[0]
    tiles = slot_ref.shape[2] // tt
    nsub = tt // LANES
    experts = [p * eg + j for j in range(eg)]

    @pl.when(t == 0)
    def _():
        xe_ref[...] = jnp.zeros_like(xe_ref)
        gate_ref[...] = jnp.zeros_like(gate_ref)

    def scatter_rows(j, ws, hit, res, aff_row):
        xe_ref[j, pl.ds(ws, GATHER_WIN), :] += res.astype(BF16)
        g = jnp.sum(jnp.where(hit, aff_row, 0.0), axis=1, keepdims=True)
        gate_ref[j, pl.ds(ws, GATHER_WIN), :] += jnp.broadcast_to(g, (GATHER_WIN, LANES))

    def gather_tile(tile, base):
        fast = _tile_max_count(off_ref, experts, tile, nsub) <= fast_max

        @pl.when(fast)
        def _():
            rows = lax.broadcasted_iota(jnp.int32, (GATHER_WIN, tt), 0)
            toks = slice(base, base + tt)
            starts = [_window_start(off_ref[e, tile * nsub]) for e in experts]
            hits = [rows == (slot_ref[j, :, toks] - starts[j]) for j in range(eg)]
            onehot = jnp.concatenate([jnp.where(hit, 1.0, 0.0).astype(BF16) for hit in hits], axis=0)
            res = jnp.dot(onehot, h_ref[toks, :], preferred_element_type=F32)
            for j in range(eg):
                scatter_rows(j, starts[j], hits[j], res[j * GATHER_WIN:(j + 1) * GATHER_WIN],
                             aff_ref[j, :, toks])

        @pl.when(jnp.logical_not(fast))
        def _():
            rows = lax.broadcasted_iota(jnp.int32, (GATHER_WIN, LANES), 0)
            for j in range(eg):
                for b in range(nsub):
                    ws = _window_start(off_ref[experts[j], tile * nsub + b])
                    toks = slice(base + b * LANES, base + (b + 1) * LANES)
                    hit = rows == (slot_ref[j, :, toks] - ws)
                    res = jnp.dot(jnp.where(hit, 1.0, 0.0).astype(BF16), h_ref[toks, :],
                                  preferred_element_type=F32)
                    scatter_rows(j, ws, hit, res, aff_ref[j, :, toks])

    for u in range(tiles):
        gather_tile(t * tiles + u, u * tt)


def _gather(off, slot3, aff3, h, *, cap, tt, eg):
    n = h.shape[0]
    capp = cap + GATHER_WIN
    tb = min(n, GATHER_TILES * tt)
    grid_spec = pltpu.PrefetchScalarGridSpec(
        num_scalar_prefetch=1,
        grid=(N_EXPERTS // eg, n // tb),
        in_specs=[pl.BlockSpec((eg, 1, tb), lambda p, t, off: (p, 0, t)),
                  pl.BlockSpec((eg, 1, tb), lambda p, t, off: (p, 0, t)),
                  pl.BlockSpec((tb, D_MODEL), lambda p, t, off: (t, 0))],
        out_specs=[pl.BlockSpec((eg, capp, D_MODEL), lambda p, t, off: (p, 0, 0)),
                   pl.BlockSpec((eg, capp, LANES), lambda p, t, off: (p, 0, 0))])
    return pl.pallas_call(
        functools.partial(_gather_kernel, tt=tt, fast_max=GATHER_FAST_MAX),
        grid_spec=grid_spec,
        out_shape=[jax.ShapeDtypeStruct((N_EXPERTS, capp, D_MODEL), BF16),
                   jax.ShapeDtypeStruct((N_EXPERTS, capp, LANES), F32)],
        compiler_params=_params(("arbitrary", "arbitrary")),
        name="gather",
    )(off, slot3, aff3, h)


def _ffn_kernel(xe_ref, gate_ref, wg_ref, wu_ref, wd_ref, ye_ref, acc_ref, *, mc):
    f = pl.program_id(1)
    cap = xe_ref.shape[1]
    wg = wg_ref[0].astype(BF16)
    wu = wu_ref[0].astype(BF16)
    wd = wd_ref[0].astype(BF16)

    @pl.when(f == 0)
    def _():
        acc_ref[...] = jnp.zeros_like(acc_ref)

    for m in range(cap // mc):
        rows = slice(m * mc, (m + 1) * mc)
        xm = xe_ref[0, rows, :]
        gate = jnp.dot(xm, wg, preferred_element_type=F32)
        up = jnp.dot(xm, wu, preferred_element_type=F32)
        hid = (jax.nn.silu(gate) * up).astype(BF16)
        acc_ref[rows, :] += jnp.dot(hid, wd, preferred_element_type=F32)

    @pl.when(f == pl.num_programs(1) - 1)
    def _():
        for m in range(cap // mc):
            rows = slice(m * mc, (m + 1) * mc)
            gate = jnp.concatenate([gate_ref[0, rows, :]] * (D_MODEL // LANES), axis=1)
            ye_ref[0, rows, :] = (acc_ref[rows, :] * gate).astype(BF16)


def _ffn(xe, gate, w_gate, w_up, w_down, *, cap, tf, mc):
    e_n = xe.shape[0]
    return pl.pallas_call(
        functools.partial(_ffn_kernel, mc=mc),
        grid=(e_n, D_EXPERT // tf),
        in_specs=[pl.BlockSpec((1, cap, D_MODEL), lambda e, f: (e, 0, 0)),
                  pl.BlockSpec((1, cap, LANES), lambda e, f: (e, 0, 0)),
                  pl.BlockSpec((1, D_MODEL, tf), lambda e, f: (e, 0, f)),
                  pl.BlockSpec((1, D_MODEL, tf), lambda e, f: (e, 0, f)),
                  pl.BlockSpec((1, tf, D_MODEL), lambda e, f: (e, f, 0))],
        out_specs=pl.BlockSpec((1, cap, D_MODEL), lambda e, f: (e, 0, 0)),
        out_shape=jax.ShapeDtypeStruct((e_n, cap, D_MODEL), BF16),
        scratch_shapes=[pltpu.VMEM((cap, D_MODEL), F32)],
        compiler_params=_params(("arbitrary", "arbitrary")),
        name="ffn",
    )(xe, gate, w_gate, w_up, w_down)


def _combine_kernel(off_ref, x1_ref, slot_ref, mod_ref, g_ref, ye_ref, o_ref,
                    wbuf_ref, buf_ref, wsem_ref, sem_ref, *, win, fast_max):
    t = pl.program_id(0)
    tt = x1_ref.shape[0]
    cap = ye_ref.shape[1]
    nsub = tt // LANES
    big = tt + BF16_SUBLANES
    experts = list(range(N_EXPERTS))

    def finish(moe):
        gate2 = mod_ref[0:1, 5 * D_MODEL:6 * D_MODEL]
        x2 = x1_ref[...] + gate2 * moe
        ms = jnp.mean(x2 * x2, axis=-1, keepdims=True)
        o_ref[...] = x2 * lax.rsqrt(ms + NORM_EPS) * g_ref[...]

    def clamped_start(e, rows, tile=t):
        return pl.multiple_of(jnp.minimum(_window_start(off_ref[e, tile * nsub]), cap - rows),
                              BF16_SUBLANES)

    def is_fast(tile):
        return _tile_max_count(off_ref, experts, tile, nsub) <= fast_max

    def window_copies(tile):
        buf = tile % 2
        return [pltpu.make_async_copy(ye_ref.at[e, pl.ds(clamped_start(e, LANES, tile), LANES), :],
                                      wbuf_ref.at[buf, pl.ds(e * LANES, LANES), :], wsem_ref.at[buf])
                for e in experts]

    def start_windows(tile):
        for cp in window_copies(tile):
            cp.start()

    fast = is_fast(t)
    pl.when(jnp.logical_and(t == 0, fast))(lambda: start_windows(t))
    nxt = jnp.minimum(t + 1, pl.num_programs(0) - 1)
    pl.when(jnp.logical_and(t + 1 < pl.num_programs(0), is_fast(nxt)))(lambda: start_windows(nxt))

    @pl.when(fast)
    def _():
        cols = lax.broadcasted_iota(jnp.int32, (tt, LANES), 1)
        onehot = jnp.concatenate(
            [jnp.where(cols == (slot_ref[:, e:e + 1] - clamped_start(e, LANES)), 1.0, 0.0).astype(BF16)
             for e in experts], axis=1)
        for cp in window_copies(t):
            cp.wait()
        finish(jnp.dot(onehot, wbuf_ref[t % 2], preferred_element_type=F32))

    @pl.when(jnp.logical_not(fast))
    def _():
        def copy(e, slot):
            return pltpu.make_async_copy(ye_ref.at[e, pl.ds(clamped_start(e, big), big), :],
                                         buf_ref.at[slot], sem_ref.at[slot])

        copy(0, 0).start()
        cols = lax.broadcasted_iota(jnp.int32, (LANES, win), 1)
        accs = [jnp.zeros((LANES, D_MODEL), F32) for _ in range(nsub)]
        for e in experts:
            slot = e % 2
            if e + 1 < N_EXPERTS:
                copy(e + 1, 1 - slot).start()
            copy(e, slot).wait()
            wb = clamped_start(e, big)
            for j in range(nsub):
                ws = jnp.minimum(_window_start(off_ref[e, t * nsub + j]), wb + (big - win))
                rel = slot_ref[j * LANES:(j + 1) * LANES, e:e + 1] - ws
                onehot = jnp.where(cols == rel, 1.0, 0.0).astype(BF16)
                r0 = pl.multiple_of(ws - wb, BF16_SUBLANES)
                accs[j] = accs[j] + jnp.dot(onehot, buf_ref[slot, pl.ds(r0, win), :],
                                            preferred_element_type=F32)
        finish(jnp.concatenate(accs, axis=0))


def _combine(off, x1, slot_t, mod, g, ye, *, tt):
    n = x1.shape[0]
    win = min(2 * LANES, tt + BF16_SUBLANES)
    grid_spec = pltpu.PrefetchScalarGridSpec(
        num_scalar_prefetch=1,
        grid=(n // tt,),
        in_specs=[pl.BlockSpec((tt, D_MODEL), lambda t, off: (t, 0)),
                  pl.BlockSpec((tt, N_EXPERTS), lambda t, off: (t, 0)),
                  pl.BlockSpec(mod.shape, lambda t, off: (0, 0)),
                  pl.BlockSpec(g.shape, lambda t, off: (0, 0)),
                  pl.BlockSpec(memory_space=pl.ANY)],
        out_specs=pl.BlockSpec((tt, D_MODEL), lambda t, off: (t, 0)),
        scratch_shapes=[pltpu.VMEM((2, N_EXPERTS * LANES, D_MODEL), BF16),
                        pltpu.VMEM((2, tt + BF16_SUBLANES, D_MODEL), BF16),
                        pltpu.SemaphoreType.DMA((2,)),
                        pltpu.SemaphoreType.DMA((2,))])
    return pl.pallas_call(
        functools.partial(_combine_kernel, win=win, fast_max=COMBINE_FAST_MAX),
        grid_spec=grid_spec,
        out_shape=jax.ShapeDtypeStruct((n, D_MODEL), F32),
        compiler_params=_params(("arbitrary",)),
        name="combine",
    )(off, x1, slot_t, mod, g, ye)


def _rope_tables(n):
    rows = n // GRID_W
    half = HEAD_DIM // 4
    freqs = ROPE_BASE ** (-jnp.arange(half, dtype=F32) / half)
    ang_r = jnp.arange(rows, dtype=jnp.int32).astype(F32)[:, None] * freqs[None, :]
    ang_c = jnp.arange(GRID_W, dtype=jnp.int32).astype(F32)[:, None] * freqs[None, :]
    zr = jnp.zeros((rows, 2 * half), F32)
    zc = jnp.zeros((GRID_W, 2 * half), F32)
    head = lambda a, b: jnp.concatenate([a, b] * 2, axis=1)
    rcos = head(jnp.concatenate([jnp.cos(ang_r)] * 2, axis=1), zr)
    rsin = head(jnp.concatenate([-jnp.sin(ang_r), jnp.sin(ang_r)], axis=1), zr)
    ccos = head(zc, jnp.concatenate([jnp.cos(ang_c)] * 2, axis=1))
    csin = head(zc, jnp.concatenate([-jnp.sin(ang_c), jnp.sin(ang_c)], axis=1))
    return rcos, rsin, ccos, csin


def _block_diag(blocks):
    k = len(blocks)
    r, c = blocks[0].shape
    out = jnp.zeros((k * r, k * c), blocks[0].dtype)
    for i, b in enumerate(blocks):
        out = out.at[i * r:(i + 1) * r, i * c:(i + 1) * c].set(b)
    return out


def _dft_tables(n, cb_n2):
    nb = LANES
    na = n // nb
    gd = FOURIER_GROUP_DIM
    scale = 1.0 / math.sqrt(n * gd)

    def cos_sin(i, j, period):
        ang = ((i * j) % period).astype(F32) * (2.0 * math.pi / period)
        return jnp.cos(ang), jnp.sin(ang)

    ar_g = jnp.arange(gd, dtype=jnp.int32)
    c64, s64 = cos_sin(ar_g[:, None], ar_g[None, :], gd)
    w0 = jnp.concatenate([_block_diag([c64 * scale] * N_FOURIER_GROUPS),
                          _block_diag([-s64 * scale] * N_FOURIER_GROUPS)], axis=1)
    ar_a = jnp.arange(na, dtype=jnp.int32)
    ca, sa = cos_sin(ar_a[:, None], ar_a[None, :], na)
    wa = jnp.concatenate([jnp.concatenate([ca, sa], axis=1),
                          jnp.concatenate([-sa, ca], axis=1)], axis=0)
    ar_b = jnp.arange(nb, dtype=jnp.int32)
    ct, st = cos_sin(ar_a[:, None], ar_b[None, :], n)
    group = lambda t: t.reshape(na, nb // cb_n2, cb_n2).transpose(1, 0, 2)
    cc, sc = cos_sin(ar_b[:, None], ar_b[None, :], nb)
    mm = jnp.concatenate([cc, sc], axis=1)
    return w0.astype(BF16), wa.astype(BF16), group(ct), group(st), mm.astype(BF16)


def _prefix_tables(r_n):
    i = np.arange(LANES)
    ut = (i[:, None] < i[None, :]).astype(np.float32)
    ones = np.ones((LANES, LANES), np.float32)
    r = np.arange(r_n)
    slt = (r[None, :] < r[:, None]).astype(np.float32)
    as_bf16 = lambda a: jnp.asarray(a).astype(BF16)
    return as_bf16(ut), as_bf16(ones), as_bf16(slt)


def kernel(x, c, ctx, c_ctx, w_mod, b_mod, norm_mix_g, norm_ffn_g, w_in, w_four, q_norm_g, k_norm_g,
           w_out, w_router, w_gate, w_up, w_down, final_norm_g):
    batch, n, _ = x.shape
    assert batch == 1 and w_mod.shape[0] == 1
    lc = ctx.shape[1]
    cap = EC_CAPACITY_FACTOR * n // N_EXPERTS
    x2d = x[0]
    ctx2d = ctx[0]

    cc = jnp.zeros((8, D_MODEL), F32).at[0].set(c[0]).at[1].set(c_ctx)
    mod = _mod(cc, w_mod[0], b_mod[0][None, :])

    rope_x = _rope_tables(n)
    zero_c = jnp.zeros((GRID_W, LANES), F32)
    rope_c = (jnp.ones((lc // GRID_W, LANES), F32), jnp.zeros((lc // GRID_W, LANES), F32), zero_c, zero_c)
    inv_hd = jnp.full((HEAD_DIM, HEAD_DIM), 1.0 / HEAD_DIM, BF16)
    bq = _block_diag([inv_hd] * N_Q_HEADS)
    bk = _block_diag([inv_hd] * N_KV_HEADS)
    q_scale = (HEAD_DIM ** -0.5) * math.log2(math.e)
    ratio = jnp.max(jnp.abs(k_norm_g[0])) / jnp.maximum(jnp.max(jnp.abs(q_norm_g[0])) * q_scale, 1e-30)
    shift = jnp.clip(jnp.round(0.5 * jnp.log2(jnp.maximum(ratio, 1e-30))), -60, 60).astype(jnp.int32)
    pow2 = lambda e: lax.bitcast_convert_type((e + 127) << 23, F32)
    gq = jnp.tile(q_norm_g[0], N_Q_HEADS)[None, :] * q_scale * pow2(shift)
    gk = jnp.tile(k_norm_g[0], N_KV_HEADS)[None, :] * pow2(-shift)
    nb = LANES
    na = n // nb
    dft_cb = min(4096, nb * FOURIER_WIDTH)
    w0, wa, tw_c, tw_s, mm = _dft_tables(n, dft_cb // FOURIER_WIDTH)
    w_in_b = w_in[0].astype(BF16)
    g_mix = norm_mix_g[0][None, :]
    tm = min(512, n)
    zr, zi, q_t, k_l, v_t = _inproj(x2d, mod, g_mix, w_in_b, rope_x, bq, bk, gq, gk, w0,
                                    mod_row=0, with_q=True, tm=tm)
    k_c, v_c = _inproj(ctx2d, mod, g_mix, w_in_b, rope_c, bq, bk, gq, gk, w0,
                       mod_row=1, with_q=False, tm=lc)

    attn = _attention(q_t, k_l, v_t, k_c, v_c, tq=min(256, n), tk=min(2048, n // 2))

    br, bi = _dft_a(zr.reshape(na, nb * FOURIER_WIDTH), zi.reshape(na, nb * FOURIER_WIDTH), wa,
                    tw_c, tw_s, cb=dft_cb)
    wf = _block_diag([w_four[0, g] for g in range(N_FOURIER_GROUPS)]).astype(BF16)
    four = _dft_c(br.reshape(na, nb, FOURIER_WIDTH), bi.reshape(na, nb, FOURIER_WIDTH), mm, wf,
                  kb=8).reshape(n, FOURIER_WIDTH)

    w_out_b = w_out[0].astype(BF16)
    wo_f = w_out_b[:FOURIER_WIDTH]
    wo_a = w_out_b[FOURIER_WIDTH:].reshape(N_KV_HEADS, Q_PER_KV * HEAD_DIM, D_MODEL)
    wo_a = jnp.pad(wo_a, ((0, 0), (0, GROUP_PAD - Q_PER_KV * HEAD_DIM), (0, 0)))
    wr_t = w_router[0].T.astype(BF16)
    x1, h2, aff = _outproj(four, attn, x2d, mod, norm_ffn_g[0][None, :], wo_f, wo_a, wr_t, tm=tm)

    r_n = n // LANES
    ut, ones, slt = _prefix_tables(r_n)
    slot3, off3 = _topk(aff.reshape(N_EXPERTS, r_n, LANES), ut, ones, slt, cap=cap)
    off = jnp.concatenate([off3[:, :, 0], jnp.full((N_EXPERTS, 1), cap, jnp.int32)], axis=1)
    slot = slot3.reshape(N_EXPERTS, n)

    tt = (min(512, cap - BF16_SUBLANES) // LANES) * LANES
    xe, gate = _gather(off, slot.reshape(N_EXPERTS, 1, n), aff.reshape(N_EXPERTS, 1, n), h2,
                       cap=cap, tt=tt, eg=GATHER_EXPERTS)
    ye = _ffn(xe, gate, w_gate[0], w_up[0], w_down[0], cap=cap, tf=512, mc=min(1024, cap))
    out = _combine(off, x1, slot.T, mod, final_norm_g[None, :], ye, tt=tt)
    return out[None]
```

```python
import functools
import math

import numpy as np
import jax
import jax.numpy as jnp
from jax import lax
from jax.experimental import pallas as pl
from jax.experimental.pallas import tpu as pltpu

D_MODEL = 1024
GRID_W = 64
HEAD_DIM = 64
N_Q_HEADS = 12
N_KV_HEADS = 4
Q_PER_KV = N_Q_HEADS // N_KV_HEADS
ATTN_WIDTH = N_Q_HEADS * HEAD_DIM
KV_WIDTH = N_KV_HEADS * HEAD_DIM
N_FOURIER_GROUPS = 4
FOURIER_GROUP_DIM = 64
FOURIER_WIDTH = N_FOURIER_GROUPS * FOURIER_GROUP_DIM
IN_WIDTH = FOURIER_WIDTH + ATTN_WIDTH + 2 * KV_WIDTH
ROPE_BASE = 10000.0
N_EXPERTS = 16
EC_CAPACITY_FACTOR = 2
D_EXPERT = 2048
N_MOD = 6
NORM_EPS = 1e-6

LANES = 128
BF16_SUBLANES = 16
GROUP_PAD = 2 * LANES
V_ROWS = HEAD_DIM + BF16_SUBLANES
GATHER_WIN = LANES + BF16_SUBLANES
GATHER_FAST_MAX = LANES
COMBINE_FAST_MAX = LANES - BF16_SUBLANES
GATHER_EXPERTS = 4
GATHER_TILES = 2
SCORE_SLOT_PAD = 8
INPROJ_SUB = 128
VMEM_LIMIT = 56 * 1024 * 1024

F32 = jnp.float32
BF16 = jnp.bfloat16
F8 = jnp.float8_e4m3fn
F8_MAX = 448.0
F8_LO_SCALE = 16.0
QK_DEPTH = 4 * HEAD_DIM


def _params(sem):
    return pltpu.CompilerParams(dimension_semantics=sem, vmem_limit_bytes=VMEM_LIMIT)


def _mod_kernel(c_ref, w_ref, b_ref, o_ref):
    s = jax.nn.silu(c_ref[...])
    o_ref[...] = jnp.dot(s, w_ref[...], precision=lax.Precision.HIGHEST,
                         preferred_element_type=F32) + b_ref[...]


def _mod(cc, w_mod, b_mod):
    tn = 1024
    n = w_mod.shape[1]
    return pl.pallas_call(
        _mod_kernel,
        grid=(n // tn,),
        in_specs=[pl.BlockSpec((8, D_MODEL), lambda j: (0, 0)),
                  pl.BlockSpec((D_MODEL, tn), lambda j: (0, j)),
                  pl.BlockSpec((1, tn), lambda j: (0, j))],
        out_specs=pl.BlockSpec((8, tn), lambda j: (0, j)),
        out_shape=jax.ShapeDtypeStruct((8, n), F32),
        compiler_params=_params(("arbitrary",)),
        name="mod",
    )(cc, w_mod, b_mod)


def _rms_modulate(x, g, shift, scale):
    ms = jnp.mean(x * x, axis=-1, keepdims=True)
    y = x * lax.rsqrt(ms + NORM_EPS) * g
    return y * (1.0 + scale) + shift


def _head_norm(t, bmat, gain):
    ms = jnp.dot((t * t).astype(BF16), bmat, preferred_element_type=F32)
    return t * lax.rsqrt(ms + NORM_EPS) * gain


def _f8_pair(x, weights):
    inv = 1.0 / F8_LO_SCALE
    hi = jnp.clip(x, -F8_MAX, F8_MAX).astype(F8).astype(F32)
    lo = ((x - hi) * F8_LO_SCALE).astype(F8).astype(F32)
    return (hi, hi * inv, lo, lo * inv) if weights else (hi, lo, hi * inv, lo * inv)


def _rope(t, cos, sin, lo_mask):
    outs = []
    for c in range(t.shape[1] // LANES):
        xc = t[:, c * LANES:(c + 1) * LANES]
        up = pltpu.roll(xc, LANES - 16, axis=1)
        dn = pltpu.roll(xc, 16, axis=1)
        rot = jnp.where(lo_mask, up, dn)
        outs.append(xc * cos + rot * sin)
    return jnp.concatenate(outs, axis=1)


def _inproj_kernel(x_ref, mod_ref, g_ref, w_ref, rcos_ref, rsin_ref, ccos_ref, csin_ref, bq_ref, bk_ref,
                   gq_ref, gk_ref, w0_ref, *out_refs, mod_row, with_q, sub):
    if with_q:
        zr_ref, zi_ref, q_ref, k_ref, v_ref = out_refs
    else:
        k_ref, v_ref = out_refs
    shift = mod_ref[mod_row:mod_row + 1, 0:D_MODEL]
    scale = mod_ref[mod_row:mod_row + 1, D_MODEL:2 * D_MODEL]
    lane = lax.broadcasted_iota(jnp.int32, (sub, LANES), 1)
    lo_mask = (lane % 32) < 16
    k0 = FOURIER_WIDTH + ATTN_WIDTH
    for r0 in range(0, x_ref.shape[0], sub):
        rows = slice(r0, r0 + sub)
        grid_rows = range(r0 // GRID_W, (r0 + sub) // GRID_W)
        h = _rms_modulate(x_ref[rows, :], g_ref[...], shift, scale).astype(BF16)
        cos = jnp.concatenate([rcos_ref[r:r + 1, :] + ccos_ref[...] for r in grid_rows], axis=0)
        sin = jnp.concatenate([rsin_ref[r:r + 1, :] + csin_ref[...] for r in grid_rows], axis=0)
        if with_q:
            p = jnp.dot(h, w_ref[...], preferred_element_type=F32)
            f = p[:, :FOURIER_WIDTH].astype(BF16)
            z = jnp.dot(f, w0_ref[...], preferred_element_type=F32)
            zr_ref[rows, :] = z[:, :FOURIER_WIDTH].astype(BF16)
            zi_ref[rows, :] = z[:, FOURIER_WIDTH:].astype(BF16)
            q = _head_norm(p[:, FOURIER_WIDTH:k0], bq_ref[...], gq_ref[...])
            q = _rope(q, cos, sin, lo_mask)
            for j, piece in enumerate(_f8_pair(q.T, weights=True)):
                q_ref[:, j * HEAD_DIM:(j + 1) * HEAD_DIM, rows] = piece.astype(F8).reshape(
                    N_Q_HEADS, HEAD_DIM, sub)
            pk = p[:, k0:k0 + KV_WIDTH]
            pv = p[:, k0 + KV_WIDTH:]
        else:
            pkv = jnp.dot(h, w_ref[:, k0:], preferred_element_type=F32)
            pk = pkv[:, :KV_WIDTH]
            pv = pkv[:, KV_WIDTH:]
        k = _head_norm(pk, bk_ref[...], gk_ref[...])
        k = _rope(k, cos, sin, lo_mask)
        pieces = _f8_pair(k, weights=False)
        for g in range(N_KV_HEADS):
            cols = slice(g * HEAD_DIM, (g + 1) * HEAD_DIM)
            k_ref[g, rows, :] = jnp.concatenate([pc[:, cols] for pc in pieces], axis=1).astype(F8)
        v_ref[:, 0:HEAD_DIM, rows] = pv.T.astype(BF16).reshape(N_KV_HEADS, HEAD_DIM, sub)
        v_ref[:, HEAD_DIM:V_ROWS, rows] = jnp.ones((N_KV_HEADS, V_ROWS - HEAD_DIM, sub), BF16)


def _inproj(x, mod, g, w_in, rope, bq, bk, gq, gk, w0, *, mod_row, with_q, tm):
    n = x.shape[0]
    rcos, rsin, ccos, csin = rope
    const = lambda shape: pl.BlockSpec(shape, lambda i: (0,) * len(shape))
    in_specs = [pl.BlockSpec((tm, D_MODEL), lambda i: (i, 0)),
                const(mod.shape), const(g.shape), const(w_in.shape),
                pl.BlockSpec((tm // GRID_W, LANES), lambda i: (i, 0)),
                pl.BlockSpec((tm // GRID_W, LANES), lambda i: (i, 0)),
                const(ccos.shape), const(csin.shape),
                const(bq.shape), const(bk.shape), const(gq.shape), const(gk.shape),
                const(w0.shape)]
    kv_specs = [pl.BlockSpec((N_KV_HEADS, tm, QK_DEPTH), lambda i: (0, i, 0)),
                pl.BlockSpec((N_KV_HEADS, V_ROWS, tm), lambda i: (0, 0, i))]
    kv_shapes = [jax.ShapeDtypeStruct((N_KV_HEADS, n, QK_DEPTH), F8),
                 jax.ShapeDtypeStruct((N_KV_HEADS, V_ROWS, n), BF16)]
    if with_q:
        out_specs = [pl.BlockSpec((tm, FOURIER_WIDTH), lambda i: (i, 0)),
                     pl.BlockSpec((tm, FOURIER_WIDTH), lambda i: (i, 0)),
                     pl.BlockSpec((N_Q_HEADS, QK_DEPTH, tm), lambda i: (0, 0, i))] + kv_specs
        out_shapes = [jax.ShapeDtypeStruct((n, FOURIER_WIDTH), BF16),
                      jax.ShapeDtypeStruct((n, FOURIER_WIDTH), BF16),
                      jax.ShapeDtypeStruct((N_Q_HEADS, QK_DEPTH, n), F8)] + kv_shapes
    else:
        out_specs, out_shapes = kv_specs, kv_shapes
    return pl.pallas_call(
        functools.partial(_inproj_kernel, mod_row=mod_row, with_q=with_q, sub=min(tm, INPROJ_SUB)),
        grid=(n // tm,),
        in_specs=in_specs,
        out_specs=out_specs,
        out_shape=out_shapes,
        compiler_params=_params(("arbitrary",)),
        name="inproj_x" if with_q else "inproj_ctx",
    )(x, mod, g, w_in, rcos, rsin, ccos, csin, bq, bk, gq, gk, w0)


def _attn_kernel(q_ref, k_ref, v_ref, kc_ref, vc_ref, o_ref, s_ref, acc_ref, *, tk):
    tq = q_ref.shape[2]
    n = k_ref.shape[1]
    lc = kc_ref.shape[1]
    nch = n // tk
    qs = jnp.concatenate([q_ref[j] for j in range(Q_PER_KV)], axis=1)

    def scores_latent(c, slot):
        start = pl.multiple_of(c * tk, tk)
        s = jnp.dot(k_ref[0, pl.ds(start, tk), :], qs, preferred_element_type=F32)
        s_ref[slot, 0:tk, :] = s
        return jnp.max(s, axis=0, keepdims=True)

    def scores_context():
        s = jnp.dot(kc_ref[0], qs, preferred_element_type=F32)
        s_ref[0, 0:lc, :] = s
        return jnp.max(s, axis=0, keepdims=True)

    def v_latent(c):
        return v_ref[0, :, pl.ds(pl.multiple_of(c * tk, tk), tk)]

    def consume(slot, rows, vb, m, mx):
        s = s_ref[slot, 0:rows, :]
        m_new = jnp.maximum(m, mx)
        alpha = jnp.exp2(m - m_new)
        p = jnp.exp2(s - m_new).astype(BF16)
        acc_ref[...] = alpha * acc_ref[...] + jnp.dot(vb, p, preferred_element_type=F32)
        return m_new

    def pair(c0, carry, prefetch):
        m, mx0 = carry
        mx1 = scores_latent(c0 + 1, 1)
        m = consume(0, tk, v_latent(c0), m, mx0)
        mx0 = prefetch()
        return consume(1, tk, v_latent(c0 + 1), m, mx1), mx0

    def body(i, carry):
        return pair(2 * i, carry, lambda: scores_latent(2 * i + 2, 0))

    w = Q_PER_KV * tq
    acc_ref[...] = jnp.zeros_like(acc_ref)
    mx0 = scores_latent(0, 0)
    carry = lax.fori_loop(0, nch // 2 - 1, body, (jnp.full((1, w), -1e30, F32), mx0))
    m, mx_ctx = pair(nch - 2, carry, scores_context)
    consume(0, lc, vc_ref[0], m, mx_ctx)
    acc = acc_ref[...]
    o = acc[0:HEAD_DIM] / acc[HEAD_DIM:HEAD_DIM + 1]
    ot = jnp.concatenate([o, jnp.zeros_like(o)], axis=0).T
    h0, h1, h2 = (ot[j * tq:(j + 1) * tq] for j in range(Q_PER_KV))
    o_ref[0] = jnp.concatenate([h0 + pltpu.roll(h1, HEAD_DIM, axis=1), h2], axis=1).astype(BF16)


def _attention(q, k, v, kc, vc, *, tq, tk):
    n = k.shape[1]
    lc = kc.shape[1]
    kd = k.shape[2]
    gw = GROUP_PAD
    return pl.pallas_call(
        functools.partial(_attn_kernel, tk=tk),
        grid=(N_KV_HEADS, n // tq),
        in_specs=[pl.BlockSpec((Q_PER_KV, kd, tq), lambda g, i: (g, 0, i)),
                  pl.BlockSpec((1, n, kd), lambda g, i: (g, 0, 0)),
                  pl.BlockSpec((1, V_ROWS, n), lambda g, i: (g, 0, 0)),
                  pl.BlockSpec((1, lc, kd), lambda g, i: (g, 0, 0)),
                  pl.BlockSpec((1, V_ROWS, lc), lambda g, i: (g, 0, 0))],
        out_specs=pl.BlockSpec((1, tq, gw), lambda g, i: (g, i, 0)),
        out_shape=jax.ShapeDtypeStruct((N_KV_HEADS, n, gw), BF16),
        scratch_shapes=[pltpu.VMEM((2, tk + SCORE_SLOT_PAD, Q_PER_KV * tq), F32),
                        pltpu.VMEM((V_ROWS, Q_PER_KV * tq), F32)],
        compiler_params=_params(("arbitrary", "arbitrary")),
        name="attention",
    )(q, k, v, kc, vc)


def _dft_a_kernel(zr_ref, zi_ref, w_ref, tc_ref, ts_ref, br_ref, bi_ref):
    na = zr_ref.shape[0]
    fw = FOURIER_WIDTH
    zz = jnp.concatenate([zr_ref[...], zi_ref[...]], axis=0)
    a = jnp.dot(w_ref[...], zz, preferred_element_type=F32)
    tc = tc_ref[0]
    ts = ts_ref[0]
    for j in range(tc.shape[1]):
        ar = a[:na, j * fw:(j + 1) * fw]
        ai = a[na:, j * fw:(j + 1) * fw]
        c = tc[:, j:j + 1]
        s = ts[:, j:j + 1]
        br_ref[:, j * fw:(j + 1) * fw] = (ar * c + ai * s).astype(BF16)
        bi_ref[:, j * fw:(j + 1) * fw] = (ai * c - ar * s).astype(BF16)


def _dft_a(zr2, zi2, wa, tc3, ts3, *, cb):
    na, cols = zr2.shape
    blk = pl.BlockSpec((na, cb), lambda j: (0, j))
    tw = pl.BlockSpec((1,) + tc3.shape[1:], lambda j: (j, 0, 0))
    return pl.pallas_call(
        _dft_a_kernel,
        grid=(cols // cb,),
        in_specs=[blk, blk, pl.BlockSpec(wa.shape, lambda j: (0, 0)), tw, tw],
        out_specs=[blk, blk],
        out_shape=[jax.ShapeDtypeStruct((na, cols), BF16)] * 2,
        compiler_params=_params(("arbitrary",)),
        name="dft_a",
    )(zr2, zi2, wa, tc3, ts3)


def _dft_c_kernel(br_ref, bi_ref, m_ref, wf_ref, o_ref):
    kb = br_ref.shape[0]
    outs = []
    for j in range(kb):
        ab = jnp.concatenate([br_ref[j], bi_ref[j]], axis=0)
        y = jnp.dot(m_ref[...], ab, preferred_element_type=F32)
        outs.append(jnp.dot(y.astype(BF16), wf_ref[...], preferred_element_type=F32))
    o_ref[...] = jnp.concatenate(outs, axis=1).astype(BF16)


def _dft_c(br3, bi3, mm, wf, *, kb):
    na, nb, fw = br3.shape
    blk = pl.BlockSpec((kb, nb, fw), lambda i: (i, 0, 0))
    return pl.pallas_call(
        _dft_c_kernel,
        grid=(na // kb,),
        in_specs=[blk, blk,
                  pl.BlockSpec(mm.shape, lambda i: (0, 0)),
                  pl.BlockSpec(wf.shape, lambda i: (0, 0))],
        out_specs=pl.BlockSpec((nb, kb * fw), lambda i: (0, i)),
        out_shape=jax.ShapeDtypeStruct((nb, na * fw), BF16),
        compiler_params=_params(("arbitrary",)),
        name="dft_c",
    )(br3, bi3, mm, wf)


def _outproj_kernel(four_ref, attn_ref, x_ref, mod_ref, g_ref, wf_ref, wa_ref, wr_ref,
                    x1_ref, h_ref, aff_ref):
    gate1 = mod_ref[0:1, 2 * D_MODEL:3 * D_MODEL]
    shift2 = mod_ref[0:1, 3 * D_MODEL:4 * D_MODEL]
    scale2 = mod_ref[0:1, 4 * D_MODEL:5 * D_MODEL]
    mix = jnp.dot(four_ref[...], wf_ref[...], preferred_element_type=F32)
    for g in range(N_KV_HEADS):
        mix = mix + jnp.dot(attn_ref[g], wa_ref[g], preferred_element_type=F32)
    x1 = x_ref[...] + gate1 * mix
    x1_ref[...] = x1
    h = _rms_modulate(x1, g_ref[...], shift2, scale2).astype(BF16)
    h_ref[...] = h
    logits = lax.dot_general(wr_ref[...], h, (((1,), (1,)), ((), ())),
                             preferred_element_type=F32)
    mx = jnp.max(logits, axis=0, keepdims=True)
    e = jnp.exp(logits - mx)
    aff_ref[...] = e / jnp.sum(e, axis=0, keepdims=True)


def _outproj(four, attn, x, mod, g, wo_f, wo_a, wr_t, *, tm):
    n = x.shape[0]
    gw = GROUP_PAD
    const = lambda shape: pl.BlockSpec(shape, lambda i: (0,) * len(shape))
    return pl.pallas_call(
        _outproj_kernel,
        grid=(n // tm,),
        in_specs=[pl.BlockSpec((tm, FOURIER_WIDTH), lambda i: (i, 0)),
                  pl.BlockSpec((N_KV_HEADS, tm, gw), lambda i: (0, i, 0)),
                  pl.BlockSpec((tm, D_MODEL), lambda i: (i, 0)),
                  const(mod.shape), const(g.shape), const(wo_f.shape), const(wo_a.shape),
                  const(wr_t.shape)],
        out_specs=[pl.BlockSpec((tm, D_MODEL), lambda i: (i, 0)),
                   pl.BlockSpec((tm, D_MODEL), lambda i: (i, 0)),
                   pl.BlockSpec((N_EXPERTS, tm), lambda i: (0, i))],
        out_shape=[jax.ShapeDtypeStruct((n, D_MODEL), F32),
                   jax.ShapeDtypeStruct((n, D_MODEL), BF16),
                   jax.ShapeDtypeStruct((N_EXPERTS, n), F32)],
        compiler_params=_params(("arbitrary",)),
        name="outproj",
    )(four, attn, x, mod, g, wo_f, wo_a, wr_t)


def _count(m):
    return jnp.sum(jnp.sum(m, axis=2, keepdims=True), axis=1, keepdims=True)


def _topk_kernel(aff_ref, ut_ref, ones_ref, slt_ref, slot_ref, off_ref, *, cap):
    e_n, r_n, _ = aff_ref.shape
    aff = aff_ref[...]
    capf = jnp.float32(cap)

    def count_ge(th):
        return _count(jnp.where(aff >= th, 1.0, 0.0))

    def search(i, t):
        cand = t | jnp.left_shift(jnp.int32(1), 30 - i)
        c = count_ge(lax.bitcast_convert_type(cand, F32))
        return jnp.where(c >= capf, cand, t)

    tbits = lax.fori_loop(0, 31, search, jnp.zeros((e_n, 1, 1), jnp.int32))
    thr = lax.bitcast_convert_type(tbits, F32)
    ulp = lax.bitcast_convert_type(tbits + 1, F32) - thr

    def refine(_, carry):
        t, step = carry
        step = step * 0.5
        cand = t + step
        return jnp.where(count_ge(cand) >= capf, cand, t), step

    thr, _ = lax.fori_loop(0, 29, refine, (thr, ulp))
    gt = jnp.where(aff > thr, 1.0, 0.0)
    eq = jnp.where(aff == thr, 1.0, 0.0)
    need = capf - _count(gt)

    def excl_prefix(m):
        m2 = m.reshape(e_n * r_n, LANES).astype(BF16)
        within = jnp.dot(m2, ut_ref[...], preferred_element_type=F32).reshape(e_n, r_n, LANES)
        tot = jnp.dot(m2, ones_ref[...], preferred_element_type=F32).reshape(e_n, r_n, LANES)
        rows = jnp.stack([jnp.dot(slt_ref[...], tot[e].astype(BF16), preferred_element_type=F32)
                          for e in range(e_n)], axis=0)
        return within + rows, rows

    pe, _ = excl_prefix(eq)
    sel = jnp.maximum(gt, eq * jnp.where(pe < need, 1.0, 0.0))
    pos, rows = excl_prefix(sel)
    slot_ref[...] = jnp.where(sel > 0.5, pos, -1.0).astype(jnp.int32)
    off_ref[...] = rows.astype(jnp.int32)


def _topk(aff3, ut, ones, slt, *, cap):
    shp = aff3.shape
    full = lambda a: pl.BlockSpec(a.shape, lambda i: (0,) * a.ndim)
    return pl.pallas_call(
        functools.partial(_topk_kernel, cap=cap),
        grid=(1,),
        in_specs=[full(aff3), full(ut), full(ones), full(slt)],
        out_specs=[pl.BlockSpec(shp, lambda i: (0, 0, 0))] * 2,
        out_shape=[jax.ShapeDtypeStruct(shp, jnp.int32)] * 2,
        compiler_params=_params(("arbitrary",)),
        name="topk",
    )(aff3, ut, ones, slt)


def _window_start(off):
    return pl.multiple_of((off // BF16_SUBLANES) * BF16_SUBLANES, BF16_SUBLANES)


def _tile_max_count(off_ref, experts, t, nsub):
    cnts = [off_ref[e, (t + 1) * nsub] - off_ref[e, t * nsub] for e in experts]
    return functools.reduce(jnp.maximum, cnts)


def _gather_kernel(off_ref, slot_ref, aff_ref, h_ref, xe_ref, gate_ref, *, tt, fast_max):
    p = pl.program_id(0)
    t = pl.program_id(1)
    eg = slot_ref.shape[0]
    tiles = slot_ref.shape[2] // tt
    nsub = tt // LANES
    experts = [p * eg + j for j in range(eg)]

    @pl.when(t == 0)
    def _():
        xe_ref[...] = jnp.zeros_like(xe_ref)
        gate_ref[...] = jnp.zeros_like(gate_ref)

    def scatter_rows(j, ws, hit, res, aff_row):
        xe_ref[j, pl.ds(ws, GATHER_WIN), :] += res.astype(BF16)
        g = jnp.sum(jnp.where(hit, aff_row, 0.0), axis=1, keepdims=True)
        gate_ref[j, pl.ds(ws, GATHER_WIN), :] += jnp.broadcast_to(g, (GATHER_WIN, LANES))

    def gather_tile(tile, base):
        fast = _tile_max_count(off_ref, experts, tile, nsub) <= fast_max

        @pl.when(fast)
        def _():
            rows = lax.broadcasted_iota(jnp.int32, (GATHER_WIN, tt), 0)
            toks = slice(base, base + tt)
            starts = [_window_start(off_ref[e, tile * nsub]) for e in experts]
            hits = [rows == (slot_ref[j, :, toks] - starts[j]) for j in range(eg)]
            onehot = jnp.concatenate([jnp.where(hit, 1.0, 0.0).astype(BF16) for hit in hits], axis=0)
            res = jnp.dot(onehot, h_ref[toks, :], preferred_element_type=F32)
            for j in range(eg):
                scatter_rows(j, starts[j], hits[j], res[j * GATHER_WIN:(j + 1) * GATHER_WIN],
                             aff_ref[j, :, toks])

        @pl.when(jnp.logical_not(fast))
        def _():
            rows = lax.broadcasted_iota(jnp.int32, (GATHER_WIN, LANES), 0)
            for j in range(eg):
                for b in range(nsub):
                    ws = _window_start(off_ref[experts[j], tile * nsub + b])
                    toks = slice(base + b * LANES, base + (b + 1) * LANES)
                    hit = rows == (slot_ref[j, :, toks] - ws)
                    res = jnp.dot(jnp.where(hit, 1.0, 0.0).astype(BF16), h_ref[toks, :],
                                  preferred_element_type=F32)
                    scatter_rows(j, ws, hit, res, aff_ref[j, :, toks])

    for u in range(tiles):
        gather_tile(t * tiles + u, u * tt)


def _gather(off, slot3, aff3, h, *, cap, tt, eg):
    n = h.shape[0]
    capp = cap + GATHER_WIN
    tb = min(n, GATHER_TILES * tt)
    grid_spec = pltpu.PrefetchScalarGridSpec(
        num_scalar_prefetch=1,
        grid=(N_EXPERTS // eg, n // tb),
        in_specs=[pl.BlockSpec((eg, 1, tb), lambda p, t, off: (p, 0, t)),
                  pl.BlockSpec((eg, 1, tb), lambda p, t, off: (p, 0, t)),
                  pl.BlockSpec((tb, D_MODEL), lambda p, t, off: (t, 0))],
        out_specs=[pl.BlockSpec((eg, capp, D_MODEL), lambda p, t, off: (p, 0, 0)),
                   pl.BlockSpec((eg, capp, LANES), lambda p, t, off: (p, 0, 0))])
    return pl.pallas_call(
        functools.partial(_gather_kernel, tt=tt, fast_max=GATHER_FAST_MAX),
        grid_spec=grid_spec,
        out_shape=[jax.ShapeDtypeStruct((N_EXPERTS, capp, D_MODEL), BF16),
                   jax.ShapeDtypeStruct((N_EXPERTS, capp, LANES), F32)],
        compiler_params=_params(("arbitrary", "arbitrary")),
        name="gather",
    )(off, slot3, aff3, h)


def _ffn_kernel(xe_ref, gate_ref, wg_ref, wu_ref, wd_ref, ye_ref, acc_ref, *, mc):
    f = pl.program_id(1)
    cap = xe_ref.shape[1]
    wg = wg_ref[0].astype(BF16)
    wu = wu_ref[0].astype(BF16)
    wd = wd_ref[0].astype(BF16)

    @pl.when(f == 0)
    def _():
        acc_ref[...] = jnp.zeros_like(acc_ref)

    for m in range(cap // mc):
        rows = slice(m * mc, (m + 1) * mc)
        xm = xe_ref[0, rows, :]
        gate = jnp.dot(xm, wg, preferred_element_type=F32)
        up = jnp.dot(xm, wu, preferred_element_type=F32)
        hid = (jax.nn.silu(gate) * up).astype(BF16)
        acc_ref[rows, :] += jnp.dot(hid, wd, preferred_element_type=F32)

    @pl.when(f == pl.num_programs(1) - 1)
    def _():
        for m in range(cap // mc):
            rows = slice(m * mc, (m + 1) * mc)
            gate = jnp.concatenate([gate_ref[0, rows, :]] * (D_MODEL // LANES), axis=1)
            ye_ref[0, rows, :] = (acc_ref[rows, :] * gate).astype(BF16)


def _ffn(xe, gate, w_gate, w_up, w_down, *, cap, tf, mc):
    e_n = xe.shape[0]
    return pl.pallas_call(
        functools.partial(_ffn_kernel, mc=mc),
        grid=(e_n, D_EXPERT // tf),
        in_specs=[pl.BlockSpec((1, cap, D_MODEL), lambda e, f: (e, 0, 0)),
                  pl.BlockSpec((1, cap, LANES), lambda e, f: (e, 0, 0)),
                  pl.BlockSpec((1, D_MODEL, tf), lambda e, f: (e, 0, f)),
                  pl.BlockSpec((1, D_MODEL, tf), lambda e, f: (e, 0, f)),
                  pl.BlockSpec((1, tf, D_MODEL), lambda e, f: (e, f, 0))],
        out_specs=pl.BlockSpec((1, cap, D_MODEL), lambda e, f: (e, 0, 0)),
        out_shape=jax.ShapeDtypeStruct((e_n, cap, D_MODEL), BF16),
        scratch_shapes=[pltpu.VMEM((cap, D_MODEL), F32)],
        compiler_params=_params(("arbitrary", "arbitrary")),
        name="ffn",
    )(xe, gate, w_gate, w_up, w_down)


def _combine_kernel(off_ref, x1_ref, slot_ref, mod_ref, g_ref, ye_ref, o_ref,
                    wbuf_ref, buf_ref, wsem_ref, sem_ref, *, win, fast_max):
    t = pl.program_id(0)
    tt = x1_ref.shape[0]
    cap = ye_ref.shape[1]
    nsub = tt // LANES
    big = tt + BF16_SUBLANES
    experts = list(range(N_EXPERTS))

    def finish(moe):
        gate2 = mod_ref[0:1, 5 * D_MODEL:6 * D_MODEL]
        x2 = x1_ref[...] + gate2 * moe
        ms = jnp.mean(x2 * x2, axis=-1, keepdims=True)
        o_ref[...] = x2 * lax.rsqrt(ms + NORM_EPS) * g_ref[...]

    def clamped_start(e, rows, tile=t):
        return pl.multiple_of(jnp.minimum(_window_start(off_ref[e, tile * nsub]), cap - rows),
                              BF16_SUBLANES)

    def is_fast(tile):
        return _tile_max_count(off_ref, experts, tile, nsub) <= fast_max

    def window_copies(tile):
        buf = tile % 2
        return [pltpu.make_async_copy(ye_ref.at[e, pl.ds(clamped_start(e, LANES, tile), LANES), :],
                                      wbuf_ref.at[buf, pl.ds(e * LANES, LANES), :], wsem_ref.at[buf])
                for e in experts]

    def start_windows(tile):
        for cp in window_copies(tile):
            cp.start()

    fast = is_fast(t)
    pl.when(jnp.logical_and(t == 0, fast))(lambda: start_windows(t))
    nxt = jnp.minimum(t + 1, pl.num_programs(0) - 1)
    pl.when(jnp.logical_and(t + 1 < pl.num_programs(0), is_fast(nxt)))(lambda: start_windows(nxt))

    @pl.when(fast)
    def _():
        cols = lax.broadcasted_iota(jnp.int32, (tt, LANES), 1)
        onehot = jnp.concatenate(
            [jnp.where(cols == (slot_ref[:, e:e + 1] - clamped_start(e, LANES)), 1.0, 0.0).astype(BF16)
             for e in experts], axis=1)
        for cp in window_copies(t):
            cp.wait()
        finish(jnp.dot(onehot, wbuf_ref[t % 2], preferred_element_type=F32))

    @pl.when(jnp.logical_not(fast))
    def _():
        def copy(e, slot):
            return pltpu.make_async_copy(ye_ref.at[e, pl.ds(clamped_start(e, big), big), :],
                                         buf_ref.at[slot], sem_ref.at[slot])

        copy(0, 0).start()
        cols = lax.broadcasted_iota(jnp.int32, (LANES, win), 1)
        accs = [jnp.zeros((LANES, D_MODEL), F32) for _ in range(nsub)]
        for e in experts:
            slot = e % 2
            if e + 1 < N_EXPERTS:
                copy(e + 1, 1 - slot).start()
            copy(e, slot).wait()
            wb = clamped_start(e, big)
            for j in range(nsub):
                ws = jnp.minimum(_window_start(off_ref[e, t * nsub + j]), wb + (big - win))
                rel = slot_ref[j * LANES:(j + 1) * LANES, e:e + 1] - ws
                onehot = jnp.where(cols == rel, 1.0, 0.0).astype(BF16)
                r0 = pl.multiple_of(ws - wb, BF16_SUBLANES)
                accs[j] = accs[j] + jnp.dot(onehot, buf_ref[slot, pl.ds(r0, win), :],
                                            preferred_element_type=F32)
        finish(jnp.concatenate(accs, axis=0))


def _combine(off, x1, slot_t, mod, g, ye, *, tt):
    n = x1.shape[0]
    win = min(2 * LANES, tt + BF16_SUBLANES)
    grid_spec = pltpu.PrefetchScalarGridSpec(
        num_scalar_prefetch=1,
        grid=(n // tt,),
        in_specs=[pl.BlockSpec((tt, D_MODEL), lambda t, off: (t, 0)),
                  pl.BlockSpec((tt, N_EXPERTS), lambda t, off: (t, 0)),
                  pl.BlockSpec(mod.shape, lambda t, off: (0, 0)),
                  pl.BlockSpec(g.shape, lambda t, off: (0, 0)),
                  pl.BlockSpec(memory_space=pl.ANY)],
        out_specs=pl.BlockSpec((tt, D_MODEL), lambda t, off: (t, 0)),
        scratch_shapes=[pltpu.VMEM((2, N_EXPERTS * LANES, D_MODEL), BF16),
                        pltpu.VMEM((2, tt + BF16_SUBLANES, D_MODEL), BF16),
                        pltpu.SemaphoreType.DMA((2,)),
                        pltpu.SemaphoreType.DMA((2,))])
    return pl.pallas_call(
        functools.partial(_combine_kernel, win=win, fast_max=COMBINE_FAST_MAX),
        grid_spec=grid_spec,
        out_shape=jax.ShapeDtypeStruct((n, D_MODEL), F32),
        compiler_params=_params(("arbitrary",)),
        name="combine",
    )(off, x1, slot_t, mod, g, ye)


def _rope_tables(n):
    rows = n // GRID_W
    half = HEAD_DIM // 4
    freqs = ROPE_BASE ** (-jnp.arange(half, dtype=F32) / half)
    ang_r = jnp.arange(rows, dtype=jnp.int32).astype(F32)[:, None] * freqs[None, :]
    ang_c = jnp.arange(GRID_W, dtype=jnp.int32).astype(F32)[:, None] * freqs[None, :]
    zr = jnp.zeros((rows, 2 * half), F32)
    zc = jnp.zeros((GRID_W, 2 * half), F32)
    head = lambda a, b: jnp.concatenate([a, b] * 2, axis=1)
    rcos = head(jnp.concatenate([jnp.cos(ang_r)] * 2, axis=1), zr)
    rsin = head(jnp.concatenate([-jnp.sin(ang_r), jnp.sin(ang_r)], axis=1), zr)
    ccos = head(zc, jnp.concatenate([jnp.cos(ang_c)] * 2, axis=1))
    csin = head(zc, jnp.concatenate([-jnp.sin(ang_c), jnp.sin(ang_c)], axis=1))
    return rcos, rsin, ccos, csin


def _block_diag(blocks):
    k = len(blocks)
    r, c = blocks[0].shape
    out = jnp.zeros((k * r, k * c), blocks[0].dtype)
    for i, b in enumerate(blocks):
        out = out.at[i * r:(i + 1) * r, i * c:(i + 1) * c].set(b)
    return out


def _dft_tables(n, cb_n2):
    nb = LANES
    na = n // nb
    gd = FOURIER_GROUP_DIM
    scale = 1.0 / math.sqrt(n * gd)

    def cos_sin(i, j, period):
        ang = ((i * j) % period).astype(F32) * (2.0 * math.pi / period)
        return jnp.cos(ang), jnp.sin(ang)

    ar_g = jnp.arange(gd, dtype=jnp.int32)
    c64, s64 = cos_sin(ar_g[:, None], ar_g[None, :], gd)
    w0 = jnp.concatenate([_block_diag([c64 * scale] * N_FOURIER_GROUPS),
                          _block_diag([-s64 * scale] * N_FOURIER_GROUPS)], axis=1)
    ar_a = jnp.arange(na, dtype=jnp.int32)
    ca, sa = cos_sin(ar_a[:, None], ar_a[None, :], na)
    wa = jnp.concatenate([jnp.concatenate([ca, sa], axis=1),
                          jnp.concatenate([-sa, ca], axis=1)], axis=0)
    ar_b = jnp.arange(nb, dtype=jnp.int32)
    ct, st = cos_sin(ar_a[:, None], ar_b[None, :], n)
    group = lambda t: t.reshape(na, nb // cb_n2, cb_n2).transpose(1, 0, 2)
    cc, sc = cos_sin(ar_b[:, None], ar_b[None, :], nb)
    mm = jnp.concatenate([cc, sc], axis=1)
    return w0.astype(BF16), wa.astype(BF16), group(ct), group(st), mm.astype(BF16)


def _prefix_tables(r_n):
    i = np.arange(LANES)
    ut = (i[:, None] < i[None, :]).astype(np.float32)
    ones = np.ones((LANES, LANES), np.float32)
    r = np.arange(r_n)
    slt = (r[None, :] < r[:, None]).astype(np.float32)
    as_bf16 = lambda a: jnp.asarray(a).astype(BF16)
    return as_bf16(ut), as_bf16(ones), as_bf16(slt)


def kernel(x, c, ctx, c_ctx, w_mod, b_mod, norm_mix_g, norm_ffn_g, w_in, w_four, q_norm_g, k_norm_g,
           w_out, w_router, w_gate, w_up, w_down, final_norm_g):
    batch, n, _ = x.shape
    assert batch == 1 and w_mod.shape[0] == 1
    lc = ctx.shape[1]
    cap = EC_CAPACITY_FACTOR * n // N_EXPERTS
    x2d = x[0]
    ctx2d = ctx[0]

    cc = jnp.zeros((8, D_MODEL), F32).at[0].set(c[0]).at[1].set(c_ctx)
    mod = _mod(cc, w_mod[0], b_mod[0][None, :])

    rope_x = _rope_tables(n)
    zero_c = jnp.zeros((GRID_W, LANES), F32)
    rope_c = (jnp.ones((lc // GRID_W, LANES), F32), jnp.zeros((lc // GRID_W, LANES), F32), zero_c, zero_c)
    inv_hd = jnp.full((HEAD_DIM, HEAD_DIM), 1.0 / HEAD_DIM, BF16)
    bq = _block_diag([inv_hd] * N_Q_HEADS)
    bk = _block_diag([inv_hd] * N_KV_HEADS)
    q_scale = (HEAD_DIM ** -0.5) * math.log2(math.e)
    ratio = jnp.max(jnp.abs(k_norm_g[0])) / jnp.maximum(jnp.max(jnp.abs(q_norm_g[0])) * q_scale, 1e-30)
    shift = jnp.clip(jnp.round(0.5 * jnp.log2(jnp.maximum(ratio, 1e-30))), -60, 60).astype(jnp.int32)
    pow2 = lambda e: lax.bitcast_convert_type((e + 127) << 23, F32)
    gq = jnp.tile(q_norm_g[0], N_Q_HEADS)[None, :] * q_scale * pow2(shift)
    gk = jnp.tile(k_norm_g[0], N_KV_HEADS)[None, :] * pow2(-shift)
    nb = LANES
    na = n // nb
    dft_cb = min(4096, nb * FOURIER_WIDTH)
    w0, wa, tw_c, tw_s, mm = _dft_tables(n, dft_cb // FOURIER_WIDTH)
    w_in_b = w_in[0].astype(BF16)
    g_mix = norm_mix_g[0][None, :]
    tm = min(512, n)
    zr, zi, q_t, k_l, v_t = _inproj(x2d, mod, g_mix, w_in_b, rope_x, bq, bk, gq, gk, w0,
                                    mod_row=0, with_q=True, tm=tm)
    k_c, v_c = _inproj(ctx2d, mod, g_mix, w_in_b, rope_c, bq, bk, gq, gk, w0,
                       mod_row=1, with_q=False, tm=lc)

    attn = _attention(q_t, k_l, v_t, k_c, v_c, tq=min(256, n), tk=min(2048, n // 2))

    br, bi = _dft_a(zr.reshape(na, nb * FOURIER_WIDTH), zi.reshape(na, nb * FOURIER_WIDTH), wa,
                    tw_c, tw_s, cb=dft_cb)
    wf = _block_diag([w_four[0, g] for g in range(N_FOURIER_GROUPS)]).astype(BF16)
    four = _dft_c(br.reshape(na, nb, FOURIER_WIDTH), bi.reshape(na, nb, FOURIER_WIDTH), mm, wf,
                  kb=8).reshape(n, FOURIER_WIDTH)

    w_out_b = w_out[0].astype(BF16)
    wo_f = w_out_b[:FOURIER_WIDTH]
    wo_a = w_out_b[FOURIER_WIDTH:].reshape(N_KV_HEADS, Q_PER_KV * HEAD_DIM, D_MODEL)
    wo_a = jnp.pad(wo_a, ((0, 0), (0, GROUP_PAD - Q_PER_KV * HEAD_DIM), (0, 0)))
    wr_t = w_router[0].T.astype(BF16)
    x1, h2, aff = _outproj(four, attn, x2d, mod, norm_ffn_g[0][None, :], wo_f, wo_a, wr_t, tm=tm)

    r_n = n // LANES
    ut, ones, slt = _prefix_tables(r_n)
    slot3, off3 = _topk(aff.reshape(N_EXPERTS, r_n, LANES), ut, ones, slt, cap=cap)
    off = jnp.concatenate([off3[:, :, 0], jnp.full((N_EXPERTS, 1), cap, jnp.int32)], axis=1)
    slot = slot3.reshape(N_EXPERTS, n)

    tt = (min(512, cap - BF16_SUBLANES) // LANES) * LANES
    xe, gate = _gather(off, slot.reshape(N_EXPERTS, 1, n), aff.reshape(N_EXPERTS, 1, n), h2,
                       cap=cap, tt=tt, eg=GATHER_EXPERTS)
    ye = _ffn(xe, gate, w_gate[0], w_up[0], w_down[0], cap=cap, tf=512, mc=min(1024, cap))
    out = _combine(off, x1, slot.T, mod, final_norm_g[None, :], ye, tt=tt)
    return out[None]
```

```python
import functools
import math

import numpy as np
import jax
import jax.numpy as jnp
from jax import lax
from jax.experimental import pallas as pl
from jax.experimental.pallas import tpu as pltpu

D_MODEL = 1024
GRID_W = 64
HEAD_DIM = 64
N_Q_HEADS = 12
N_KV_HEADS = 4
Q_PER_KV = N_Q_HEADS // N_KV_HEADS
ATTN_WIDTH = N_Q_HEADS * HEAD_DIM
KV_WIDTH = N_KV_HEADS * HEAD_DIM
N_FOURIER_GROUPS = 4
FOURIER_GROUP_DIM = 64
FOURIER_WIDTH = N_FOURIER_GROUPS * FOURIER_GROUP_DIM
IN_WIDTH = FOURIER_WIDTH + ATTN_WIDTH + 2 * KV_WIDTH
ROPE_BASE = 10000.0
N_EXPERTS = 16
EC_CAPACITY_FACTOR = 2
D_EXPERT = 2048
N_MOD = 6
NORM_EPS = 1e-6

LANES = 128
BF16_SUBLANES = 16
GROUP_PAD = 2 * LANES
V_ROWS = HEAD_DIM + BF16_SUBLANES
GATHER_WIN = LANES + BF16_SUBLANES
GATHER_FAST_MAX = LANES
COMBINE_FAST_MAX = LANES - BF16_SUBLANES
GATHER_EXPERTS = 8
GATHER_TILES = 2
INPROJ_SUB = 128
VMEM_LIMIT = 56 * 1024 * 1024

F32 = jnp.float32
BF16 = jnp.bfloat16
F8 = jnp.float8_e4m3fn
F8_MAX = 448.0
F8_LO_SCALE = 16.0
QK_DEPTH = 4 * HEAD_DIM


def _params(sem):
    return pltpu.CompilerParams(dimension_semantics=sem, vmem_limit_bytes=VMEM_LIMIT)


def _mod_kernel(c_ref, w_ref, b_ref, o_ref):
    s = jax.nn.silu(c_ref[...])
    o_ref[...] = jnp.dot(s, w_ref[...], precision=lax.Precision.HIGHEST,
                         preferred_element_type=F32) + b_ref[...]


def _mod(cc, w_mod, b_mod):
    tn = 1024
    n = w_mod.shape[1]
    return pl.pallas_call(
        _mod_kernel,
        grid=(n // tn,),
        in_specs=[pl.BlockSpec((8, D_MODEL), lambda j: (0, 0)),
                  pl.BlockSpec((D_MODEL, tn), lambda j: (0, j)),
                  pl.BlockSpec((1, tn), lambda j: (0, j))],
        out_specs=pl.BlockSpec((8, tn), lambda j: (0, j)),
        out_shape=jax.ShapeDtypeStruct((8, n), F32),
        compiler_params=_params(("arbitrary",)),
        name="mod",
    )(cc, w_mod, b_mod)


def _rms_modulate(x, g, shift, scale):
    ms = jnp.mean(x * x, axis=-1, keepdims=True)
    y = x * lax.rsqrt(ms + NORM_EPS) * g
    return y * (1.0 + scale) + shift


def _head_norm(t, bmat, gain):
    ms = jnp.dot((t * t).astype(BF16), bmat, preferred_element_type=F32)
    return t * lax.rsqrt(ms + NORM_EPS) * gain


def _f8_pair(x, weights):
    inv = 1.0 / F8_LO_SCALE
    hi = jnp.clip(x, -F8_MAX, F8_MAX).astype(F8).astype(F32)
    lo = ((x - hi) * F8_LO_SCALE).astype(F8).astype(F32)
    return (hi, hi * inv, lo, lo * inv) if weights else (hi, lo, hi * inv, lo * inv)


def _rope(t, cos, sin, lo_mask):
    outs = []
    for c in range(t.shape[1] // LANES):
        xc = t[:, c * LANES:(c + 1) * LANES]
        up = pltpu.roll(xc, LANES - 16, axis=1)
        dn = pltpu.roll(xc, 16, axis=1)
        rot = jnp.where(lo_mask, up, dn)
        outs.append(xc * cos + rot * sin)
    return jnp.concatenate(outs, axis=1)


def _inproj_kernel(x_ref, mod_ref, g_ref, w_ref, rcos_ref, rsin_ref, ccos_ref, csin_ref, bq_ref, bk_ref,
                   gq_ref, gk_ref, w0_ref, *out_refs, mod_row, with_q, sub):
    if with_q:
        zr_ref, zi_ref, q_ref, k_ref, v_ref = out_refs
    else:
        k_ref, v_ref = out_refs
    shift = mod_ref[mod_row:mod_row + 1, 0:D_MODEL]
    scale = mod_ref[mod_row:mod_row + 1, D_MODEL:2 * D_MODEL]
    lane = lax.broadcasted_iota(jnp.int32, (sub, LANES), 1)
    lo_mask = (lane % 32) < 16
    k0 = FOURIER_WIDTH + ATTN_WIDTH
    for r0 in range(0, x_ref.shape[0], sub):
        rows = slice(r0, r0 + sub)
        grid_rows = range(r0 // GRID_W, (r0 + sub) // GRID_W)
        h = _rms_modulate(x_ref[rows, :], g_ref[...], shift, scale).astype(BF16)
        cos = jnp.concatenate([rcos_ref[r:r + 1, :] + ccos_ref[...] for r in grid_rows], axis=0)
        sin = jnp.concatenate([rsin_ref[r:r + 1, :] + csin_ref[...] for r in grid_rows], axis=0)
        if with_q:
            p = jnp.dot(h, w_ref[...], preferred_element_type=F32)
            f = p[:, :FOURIER_WIDTH].astype(BF16)
            z = jnp.dot(f, w0_ref[...], preferred_element_type=F32)
            zr_ref[rows, :] = z[:, :FOURIER_WIDTH].astype(BF16)
            zi_ref[rows, :] = z[:, FOURIER_WIDTH:].astype(BF16)
            q = _head_norm(p[:, FOURIER_WIDTH:k0], bq_ref[...], gq_ref[...])
            q = _rope(q, cos, sin, lo_mask)
            for j, piece in enumerate(_f8_pair(q.T, weights=True)):
                q_ref[:, j * HEAD_DIM:(j + 1) * HEAD_DIM, rows] = piece.astype(F8).reshape(
                    N_Q_HEADS, HEAD_DIM, sub)
            pk = p[:, k0:k0 + KV_WIDTH]
            pv = p[:, k0 + KV_WIDTH:]
        else:
            pkv = jnp.dot(h, w_ref[:, k0:], preferred_element_type=F32)
            pk = pkv[:, :KV_WIDTH]
            pv = pkv[:, KV_WIDTH:]
        k = _head_norm(pk, bk_ref[...], gk_ref[...])
        k = _rope(k, cos, sin, lo_mask)
        pieces = _f8_pair(k, weights=False)
        for g in range(N_KV_HEADS):
            cols = slice(g * HEAD_DIM, (g + 1) * HEAD_DIM)
            k_ref[g, rows, :] = jnp.concatenate([pc[:, cols] for pc in pieces], axis=1).astype(F8)
        v_ref[:, 0:HEAD_DIM, rows] = pv.T.astype(BF16).reshape(N_KV_HEADS, HEAD_DIM, sub)
        v_ref[:, HEAD_DIM:V_ROWS, rows] = jnp.ones((N_KV_HEADS, V_ROWS - HEAD_DIM, sub), BF16)


def _inproj(x, mod, g, w_in, rope, bq, bk, gq, gk, w0, *, mod_row, with_q, tm):
    n = x.shape[0]
    rcos, rsin, ccos, csin = rope
    const = lambda shape: pl.BlockSpec(shape, lambda i: (0,) * len(shape))
    in_specs = [pl.BlockSpec((tm, D_MODEL), lambda i: (i, 0)),
                const(mod.shape), const(g.shape), const(w_in.shape),
                pl.BlockSpec((tm // GRID_W, LANES), lambda i: (i, 0)),
                pl.BlockSpec((tm // GRID_W, LANES), lambda i: (i, 0)),
                const(ccos.shape), const(csin.shape),
                const(bq.shape), const(bk.shape), const(gq.shape), const(gk.shape),
                const(w0.shape)]
    kv_specs = [pl.BlockSpec((N_KV_HEADS, tm, QK_DEPTH), lambda i: (0, i, 0)),
                pl.BlockSpec((N_KV_HEADS, V_ROWS, tm), lambda i: (0, 0, i))]
    kv_shapes = [jax.ShapeDtypeStruct((N_KV_HEADS, n, QK_DEPTH), F8),
                 jax.ShapeDtypeStruct((N_KV_HEADS, V_ROWS, n), BF16)]
    if with_q:
        out_specs = [pl.BlockSpec((tm, FOURIER_WIDTH), lambda i: (i, 0)),
                     pl.BlockSpec((tm, FOURIER_WIDTH), lambda i: (i, 0)),
                     pl.BlockSpec((N_Q_HEADS, QK_DEPTH, tm), lambda i: (0, 0, i))] + kv_specs
        out_shapes = [jax.ShapeDtypeStruct((n, FOURIER_WIDTH), BF16),
                      jax.ShapeDtypeStruct((n, FOURIER_WIDTH), BF16),
                      jax.ShapeDtypeStruct((N_Q_HEADS, QK_DEPTH, n), F8)] + kv_shapes
    else:
        out_specs, out_shapes = kv_specs, kv_shapes
    return pl.pallas_call(
        functools.partial(_inproj_kernel, mod_row=mod_row, with_q=with_q, sub=min(tm, INPROJ_SUB)),
        grid=(n // tm,),
        in_specs=in_specs,
        out_specs=out_specs,
        out_shape=out_shapes,
        compiler_params=_params(("arbitrary",)),
        name="inproj_x" if with_q else "inproj_ctx",
    )(x, mod, g, w_in, rcos, rsin, ccos, csin, bq, bk, gq, gk, w0)


def _attn_kernel(q_ref, k_ref, v_ref, kc_ref, vc_ref, o_ref, s_ref, acc_ref, *, tk):
    tq = q_ref.shape[2]
    n = k_ref.shape[1]
    lc = kc_ref.shape[1]
    nch = n // tk
    qs = jnp.concatenate([q_ref[j] for j in range(Q_PER_KV)], axis=1)

    def scores_latent(c, slot):
        start = pl.multiple_of(c * tk, tk)
        s = jnp.dot(k_ref[0, pl.ds(start, tk), :], qs, preferred_element_type=F32)
        s_ref[slot] = s
        return jnp.max(s, axis=0, keepdims=True)

    def scores_context():
        s = jnp.dot(kc_ref[0], qs, preferred_element_type=F32)
        s_ref[0, 0:lc, :] = s
        return jnp.max(s, axis=0, keepdims=True)

    def v_latent(c):
        return v_ref[0, :, pl.ds(pl.multiple_of(c * tk, tk), tk)]

    def consume(slot, rows, vb, m, mx):
        s = s_ref[slot, 0:rows, :]
        m_new = jnp.maximum(m, mx)
        alpha = jnp.exp2(m - m_new)
        p = jnp.exp2(s - m_new).astype(BF16)
        acc_ref[...] = alpha * acc_ref[...] + jnp.dot(vb, p, preferred_element_type=F32)
        return m_new

    def pair(c0, carry, prefetch):
        m, mx0 = carry
        mx1 = scores_latent(c0 + 1, 1)
        m = consume(0, tk, v_latent(c0), m, mx0)
        mx0 = prefetch()
        return consume(1, tk, v_latent(c0 + 1), m, mx1), mx0

    def body(i, carry):
        return pair(2 * i, carry, lambda: scores_latent(2 * i + 2, 0))

    w = Q_PER_KV * tq
    acc_ref[...] = jnp.zeros_like(acc_ref)
    mx0 = scores_latent(0, 0)
    carry = lax.fori_loop(0, nch // 2 - 1, body, (jnp.full((1, w), -1e30, F32), mx0))
    m, mx_ctx = pair(nch - 2, carry, scores_context)
    consume(0, lc, vc_ref[0], m, mx_ctx)
    acc = acc_ref[...]
    o = acc[0:HEAD_DIM] / acc[HEAD_DIM:HEAD_DIM + 1]
    ot = jnp.concatenate([o, jnp.zeros_like(o)], axis=0).T
    h0, h1, h2 = (ot[j * tq:(j + 1) * tq] for j in range(Q_PER_KV))
    o_ref[0] = jnp.concatenate([h0 + pltpu.roll(h1, HEAD_DIM, axis=1), h2], axis=1).astype(BF16)


def _attention(q, k, v, kc, vc, *, tq, tk):
    n = k.shape[1]
    lc = kc.shape[1]
    kd = k.shape[2]
    gw = GROUP_PAD
    return pl.pallas_call(
        functools.partial(_attn_kernel, tk=tk),
        grid=(N_KV_HEADS, n // tq),
        in_specs=[pl.BlockSpec((Q_PER_KV, kd, tq), lambda g, i: (g, 0, i)),
                  pl.BlockSpec((1, n, kd), lambda g, i: (g, 0, 0)),
                  pl.BlockSpec((1, V_ROWS, n), lambda g, i: (g, 0, 0)),
                  pl.BlockSpec((1, lc, kd), lambda g, i: (g, 0, 0)),
                  pl.BlockSpec((1, V_ROWS, lc), lambda g, i: (g, 0, 0))],
        out_specs=pl.BlockSpec((1, tq, gw), lambda g, i: (g, i, 0)),
        out_shape=jax.ShapeDtypeStruct((N_KV_HEADS, n, gw), BF16),
        scratch_shapes=[pltpu.VMEM((2, tk, Q_PER_KV * tq), F32),
                        pltpu.VMEM((V_ROWS, Q_PER_KV * tq), F32)],
        compiler_params=_params(("arbitrary", "arbitrary")),
        name="attention",
    )(q, k, v, kc, vc)


def _dft_a_kernel(zr_ref, zi_ref, w_ref, tc_ref, ts_ref, br_ref, bi_ref):
    na = zr_ref.shape[0]
    fw = FOURIER_WIDTH
    zz = jnp.concatenate([zr_ref[...], zi_ref[...]], axis=0)
    a = jnp.dot(w_ref[...], zz, preferred_element_type=F32)
    tc = tc_ref[0]
    ts = ts_ref[0]
    for j in range(tc.shape[1]):
        ar = a[:na, j * fw:(j + 1) * fw]
        ai = a[na:, j * fw:(j + 1) * fw]
        c = tc[:, j:j + 1]
        s = ts[:, j:j + 1]
        br_ref[:, j * fw:(j + 1) * fw] = (ar * c + ai * s).astype(BF16)
        bi_ref[:, j * fw:(j + 1) * fw] = (ai * c - ar * s).astype(BF16)


def _dft_a(zr2, zi2, wa, tc3, ts3, *, cb):
    na, cols = zr2.shape
    blk = pl.BlockSpec((na, cb), lambda j: (0, j))
    tw = pl.BlockSpec((1,) + tc3.shape[1:], lambda j: (j, 0, 0))
    return pl.pallas_call(
        _dft_a_kernel,
        grid=(cols // cb,),
        in_specs=[blk, blk, pl.BlockSpec(wa.shape, lambda j: (0, 0)), tw, tw],
        out_specs=[blk, blk],
        out_shape=[jax.ShapeDtypeStruct((na, cols), BF16)] * 2,
        compiler_params=_params(("arbitrary",)),
        name="dft_a",
    )(zr2, zi2, wa, tc3, ts3)


def _dft_c_kernel(br_ref, bi_ref, m_ref, wf_ref, o_ref):
    kb = br_ref.shape[0]
    outs = []
    for j in range(kb):
        ab = jnp.concatenate([br_ref[j], bi_ref[j]], axis=0)
        y = jnp.dot(m_ref[...], ab, preferred_element_type=F32)
        outs.append(jnp.dot(y.astype(BF16), wf_ref[...], preferred_element_type=F32))
    o_ref[...] = jnp.concatenate(outs, axis=1).astype(BF16)


def _dft_c(br3, bi3, mm, wf, *, kb):
    na, nb, fw = br3.shape
    blk = pl.BlockSpec((kb, nb, fw), lambda i: (i, 0, 0))
    return pl.pallas_call(
        _dft_c_kernel,
        grid=(na // kb,),
        in_specs=[blk, blk,
                  pl.BlockSpec(mm.shape, lambda i: (0, 0)),
                  pl.BlockSpec(wf.shape, lambda i: (0, 0))],
        out_specs=pl.BlockSpec((nb, kb * fw), lambda i: (0, i)),
        out_shape=jax.ShapeDtypeStruct((nb, na * fw), BF16),
        compiler_params=_params(("arbitrary",)),
        name="dft_c",
    )(br3, bi3, mm, wf)


def _outproj_kernel(four_ref, attn_ref, x_ref, mod_ref, g_ref, wf_ref, wa_ref, wr_ref,
                    x1_ref, h_ref, aff_ref):
    gate1 = mod_ref[0:1, 2 * D_MODEL:3 * D_MODEL]
    shift2 = mod_ref[0:1, 3 * D_MODEL:4 * D_MODEL]
    scale2 = mod_ref[0:1, 4 * D_MODEL:5 * D_MODEL]
    mix = jnp.dot(four_ref[...], wf_ref[...], preferred_element_type=F32)
    for g in range(N_KV_HEADS):
        mix = mix + jnp.dot(attn_ref[g], wa_ref[g], preferred_element_type=F32)
    x1 = x_ref[...] + gate1 * mix
    x1_ref[...] = x1
    h = _rms_modulate(x1, g_ref[...], shift2, scale2).astype(BF16)
    h_ref[...] = h
    logits = lax.dot_general(wr_ref[...], h, (((1,), (1,)), ((), ())),
                             preferred_element_type=F32)
    mx = jnp.max(logits, axis=0, keepdims=True)
    e = jnp.exp(logits - mx)
    aff_ref[...] = e / jnp.sum(e, axis=0, keepdims=True)


def _outproj(four, attn, x, mod, g, wo_f, wo_a, wr_t, *, tm):
    n = x.shape[0]
    gw = GROUP_PAD
    const = lambda shape: pl.BlockSpec(shape, lambda i: (0,) * len(shape))
    return pl.pallas_call(
        _outproj_kernel,
        grid=(n // tm,),
        in_specs=[pl.BlockSpec((tm, FOURIER_WIDTH), lambda i: (i, 0)),
                  pl.BlockSpec((N_KV_HEADS, tm, gw), lambda i: (0, i, 0)),
                  pl.BlockSpec((tm, D_MODEL), lambda i: (i, 0)),
                  const(mod.shape), const(g.shape), const(wo_f.shape), const(wo_a.shape),
                  const(wr_t.shape)],
        out_specs=[pl.BlockSpec((tm, D_MODEL), lambda i: (i, 0)),
                   pl.BlockSpec((tm, D_MODEL), lambda i: (i, 0)),
                   pl.BlockSpec((N_EXPERTS, tm), lambda i: (0, i))],
        out_shape=[jax.ShapeDtypeStruct((n, D_MODEL), F32),
                   jax.ShapeDtypeStruct((n, D_MODEL), BF16),
                   jax.ShapeDtypeStruct((N_EXPERTS, n), F32)],
        compiler_params=_params(("arbitrary",)),
        name="outproj",
    )(four, attn, x, mod, g, wo_f, wo_a, wr_t)


def _count(m):
    return jnp.sum(jnp.sum(m, axis=2, keepdims=True), axis=1, keepdims=True)


def _topk_kernel(aff_ref, ut_ref, ones_ref, slt_ref, slot_ref, off_ref, *, cap):
    e_n, r_n, _ = aff_ref.shape
    aff = aff_ref[...]
    capf = jnp.float32(cap)

    def count_ge(th):
        return _count(jnp.where(aff >= th, 1.0, 0.0))

    def search(i, t):
        cand = t | jnp.left_shift(jnp.int32(1), 30 - i)
        c = count_ge(lax.bitcast_convert_type(cand, F32))
        return jnp.where(c >= capf, cand, t)

    tbits = lax.fori_loop(0, 31, search, jnp.zeros((e_n, 1, 1), jnp.int32))
    thr = lax.bitcast_convert_type(tbits, F32)
    ulp = lax.bitcast_convert_type(tbits + 1, F32) - thr

    def refine(_, carry):
        t, step = carry
        step = step * 0.5
        cand = t + step
        return jnp.where(count_ge(cand) >= capf, cand, t), step

    thr, _ = lax.fori_loop(0, 29, refine, (thr, ulp))
    gt = jnp.where(aff > thr, 1.0, 0.0)
    eq = jnp.where(aff == thr, 1.0, 0.0)
    need = capf - _count(gt)

    def excl_prefix(m):
        m2 = m.reshape(e_n * r_n, LANES).astype(BF16)
        within = jnp.dot(m2, ut_ref[...], preferred_element_type=F32).reshape(e_n, r_n, LANES)
        tot = jnp.dot(m2, ones_ref[...], preferred_element_type=F32).reshape(e_n, r_n, LANES)
        rows = jnp.stack([jnp.dot(slt_ref[...], tot[e].astype(BF16), preferred_element_type=F32)
                          for e in range(e_n)], axis=0)
        return within + rows, rows

    pe, _ = excl_prefix(eq)
    sel = jnp.maximum(gt, eq * jnp.where(pe < need, 1.0, 0.0))
    pos, rows = excl_prefix(sel)
    slot_ref[...] = jnp.where(sel > 0.5, pos, -1.0).astype(jnp.int32)
    off_ref[...] = rows.astype(jnp.int32)


def _topk(aff3, ut, ones, slt, *, cap):
    shp = aff3.shape
    full = lambda a: pl.BlockSpec(a.shape, lambda i: (0,) * a.ndim)
    return pl.pallas_call(
        functools.partial(_topk_kernel, cap=cap),
        grid=(1,),
        in_specs=[full(aff3), full(ut), full(ones), full(slt)],
        out_specs=[pl.BlockSpec(shp, lambda i: (0, 0, 0))] * 2,
        out_shape=[jax.ShapeDtypeStruct(shp, jnp.int32)] * 2,
        compiler_params=_params(("arbitrary",)),
        name="topk",
    )(aff3, ut, ones, slt)


def _window_start(off):
    return pl.multiple_of((off // BF16_SUBLANES) * BF16_SUBLANES, BF16_SUBLANES)


def _tile_max_count(off_ref, experts, t, nsub):
    cnts = [off_ref[e, (t + 1) * nsub] - off_ref[e, t * nsub] for e in experts]
    return functools.reduce(jnp.maximum, cnts)


def _gather_kernel(off_ref, slot_ref, aff_ref, h_ref, xe_ref, gate_ref, *, tt, fast_max):
    p = pl.program_id(0)
    t = pl.program_id(1)
    eg = slot_ref.shape[0]
    tiles = slot_ref.shape[2] // tt
    nsub = tt // LANES
    experts = [p * eg + j for j in range(eg)]

    @pl.when(t == 0)
    def _():
        xe_ref[...] = jnp.zeros_like(xe_ref)
        gate_ref[...] = jnp.zeros_like(gate_ref)

    def scatter_rows(j, ws, hit, res, aff_row):
        xe_ref[j, pl.ds(ws, GATHER_WIN), :] += res.astype(BF16)
        g = jnp.sum(jnp.where(hit, aff_row, 0.0), axis=1, keepdims=True)
        gate_ref[j, pl.ds(ws, GATHER_WIN), :] += jnp.broadcast_to(g, (GATHER_WIN, LANES))

    def gather_tile(tile, base):
        fast = _tile_max_count(off_ref, experts, tile, nsub) <= fast_max

        @pl.when(fast)
        def _():
            rows = lax.broadcasted_iota(jnp.int32, (GATHER_WIN, tt), 0)
            toks = slice(base, base + tt)
            starts = [_window_start(off_ref[e, tile * nsub]) for e in experts]
            hits = [rows == (slot_ref[j, :, toks] - starts[j]) for j in range(eg)]
            onehot = jnp.concatenate([jnp.where(hit, 1.0, 0.0).astype(BF16) for hit in hits], axis=0)
            res = jnp.dot(onehot, h_ref[toks, :], preferred_element_type=F32)
            for j in range(eg):
                scatter_rows(j, starts[j], hits[j], res[j * GATHER_WIN:(j + 1) * GATHER_WIN],
                             aff_ref[j, :, toks])

        @pl.when(jnp.logical_not(fast))
        def _():
            rows = lax.broadcasted_iota(jnp.int32, (GATHER_WIN, LANES), 0)
            for j in range(eg):
                for b in range(nsub):
                    ws = _window_start(off_ref[experts[j], tile * nsub + b])
                    toks = slice(base + b * LANES, base + (b + 1) * LANES)
                    hit = rows == (slot_ref[j, :, toks] - ws)
                    res = jnp.dot(jnp.where(hit, 1.0, 0.0).astype(BF16), h_ref[toks, :],
                                  preferred_element_type=F32)
                    scatter_rows(j, ws, hit, res, aff_ref[j, :, toks])

    for u in range(tiles):
        gather_tile(t * tiles + u, u * tt)


def _gather(off, slot3, aff3, h, *, cap, tt, eg):
    n = h.shape[0]
    capp = cap + GATHER_WIN
    tb = min(n, GATHER_TILES * tt)
    grid_spec = pltpu.PrefetchScalarGridSpec(
        num_scalar_prefetch=1,
        grid=(N_EXPERTS // eg, n // tb),
        in_specs=[pl.BlockSpec((eg, 1, tb), lambda p, t, off: (p, 0, t)),
                  pl.BlockSpec((eg, 1, tb), lambda p, t, off: (p, 0, t)),
                  pl.BlockSpec((tb, D_MODEL), lambda p, t, off: (t, 0))],
        out_specs=[pl.BlockSpec((eg, capp, D_MODEL), lambda p, t, off: (p, 0, 0),
                                pipeline_mode=pl.Buffered(1)),
                   pl.BlockSpec((eg, capp, LANES), lambda p, t, off: (p, 0, 0),
                                pipeline_mode=pl.Buffered(1))])
    return pl.pallas_call(
        functools.partial(_gather_kernel, tt=tt, fast_max=GATHER_FAST_MAX),
        grid_spec=grid_spec,
        out_shape=[jax.ShapeDtypeStruct((N_EXPERTS, capp, D_MODEL), BF16),
                   jax.ShapeDtypeStruct((N_EXPERTS, capp, LANES), F32)],
        compiler_params=_params(("arbitrary", "arbitrary")),
        name="gather",
    )(off, slot3, aff3, h)


def _ffn_kernel(xe_ref, gate_ref, wg_ref, wu_ref, wd_ref, ye_ref, acc_ref, *, mc):
    f = pl.program_id(1)
    cap = xe_ref.shape[1]
    wg = wg_ref[0].astype(BF16)
    wu = wu_ref[0].astype(BF16)
    wd = wd_ref[0].astype(BF16)

    @pl.when(f == 0)
    def _():
        acc_ref[...] = jnp.zeros_like(acc_ref)

    for m in range(cap // mc):
        rows = slice(m * mc, (m + 1) * mc)
        xm = xe_ref[0, rows, :]
        gate = jnp.dot(xm, wg, preferred_element_type=F32)
        up = jnp.dot(xm, wu, preferred_element_type=F32)
        hid = (jax.nn.silu(gate) * up).astype(BF16)
        acc_ref[rows, :] += jnp.dot(hid, wd, preferred_element_type=F32)

    @pl.when(f == pl.num_programs(1) - 1)
    def _():
        for m in range(cap // mc):
            rows = slice(m * mc, (m + 1) * mc)
            gate = jnp.concatenate([gate_ref[0, rows, :]] * (D_MODEL // LANES), axis=1)
            ye_ref[0, rows, :] = (acc_ref[rows, :] * gate).astype(BF16)


def _ffn(xe, gate, w_gate, w_up, w_down, *, cap, tf, mc):
    e_n = xe.shape[0]
    return pl.pallas_call(
        functools.partial(_ffn_kernel, mc=mc),
        grid=(e_n, D_EXPERT // tf),
        in_specs=[pl.BlockSpec((1, cap, D_MODEL), lambda e, f: (e, 0, 0)),
                  pl.BlockSpec((1, cap, LANES), lambda e, f: (e, 0, 0)),
                  pl.BlockSpec((1, D_MODEL, tf), lambda e, f: (e, 0, f)),
                  pl.BlockSpec((1, D_MODEL, tf), lambda e, f: (e, 0, f)),
                  pl.BlockSpec((1, tf, D_MODEL), lambda e, f: (e, f, 0))],
        out_specs=pl.BlockSpec((1, cap, D_MODEL), lambda e, f: (e, 0, 0)),
        out_shape=jax.ShapeDtypeStruct((e_n, cap, D_MODEL), BF16),
        scratch_shapes=[pltpu.VMEM((cap, D_MODEL), F32)],
        compiler_params=_params(("arbitrary", "arbitrary")),
        name="ffn",
    )(xe, gate, w_gate, w_up, w_down)


def _combine_kernel(off_ref, x1_ref, slot_ref, mod_ref, g_ref, ye_ref, o_ref,
                    wbuf_ref, buf_ref, wsem_ref, sem_ref, *, win, fast_max):
    t = pl.program_id(0)
    tt = x1_ref.shape[0]
    cap = ye_ref.shape[1]
    nsub = tt // LANES
    big = tt + BF16_SUBLANES
    experts = list(range(N_EXPERTS))

    def finish(moe):
        gate2 = mod_ref[0:1, 5 * D_MODEL:6 * D_MODEL]
        x2 = x1_ref[...] + gate2 * moe
        ms = jnp.mean(x2 * x2, axis=-1, keepdims=True)
        o_ref[...] = x2 * lax.rsqrt(ms + NORM_EPS) * g_ref[...]

    def clamped_start(e, rows, tile=t):
        return pl.multiple_of(jnp.minimum(_window_start(off_ref[e, tile * nsub]), cap - rows),
                              BF16_SUBLANES)

    def is_fast(tile):
        return _tile_max_count(off_ref, experts, tile, nsub) <= fast_max

    def window_copies(tile):
        buf = tile % 2
        return [pltpu.make_async_copy(ye_ref.at[e, pl.ds(clamped_start(e, LANES, tile), LANES), :],
                                      wbuf_ref.at[buf, pl.ds(e * LANES, LANES), :], wsem_ref.at[buf])
                for e in experts]

    def start_windows(tile):
        for cp in window_copies(tile):
            cp.start()

    fast = is_fast(t)
    pl.when(jnp.logical_and(t == 0, fast))(lambda: start_windows(t))
    nxt = jnp.minimum(t + 1, pl.num_programs(0) - 1)
    pl.when(jnp.logical_and(t + 1 < pl.num_programs(0), is_fast(nxt)))(lambda: start_windows(nxt))

    @pl.when(fast)
    def _():
        cols = lax.broadcasted_iota(jnp.int32, (tt, LANES), 1)
        onehot = jnp.concatenate(
            [jnp.where(cols == (slot_ref[:, e:e + 1] - clamped_start(e, LANES)), 1.0, 0.0).astype(BF16)
             for e in experts], axis=1)
        for cp in window_copies(t):
            cp.wait()
        finish(jnp.dot(onehot, wbuf_ref[t % 2], preferred_element_type=F32))

    @pl.when(jnp.logical_not(fast))
    def _():
        def copy(e, slot):
            return pltpu.make_async_copy(ye_ref.at[e, pl.ds(clamped_start(e, big), big), :],
                                         buf_ref.at[slot], sem_ref.at[slot])

        copy(0, 0).start()
        cols = lax.broadcasted_iota(jnp.int32, (LANES, win), 1)
        accs = [jnp.zeros((LANES, D_MODEL), F32) for _ in range(nsub)]
        for e in experts:
            slot = e % 2
            if e + 1 < N_EXPERTS:
                copy(e + 1, 1 - slot).start()
            copy(e, slot).wait()
            wb = clamped_start(e, big)
            for j in range(nsub):
                ws = jnp.minimum(_window_start(off_ref[e, t * nsub + j]), wb + (big - win))
                rel = slot_ref[j * LANES:(j + 1) * LANES, e:e + 1] - ws
                onehot = jnp.where(cols == rel, 1.0, 0.0).astype(BF16)
                r0 = pl.multiple_of(ws - wb, BF16_SUBLANES)
                accs[j] = accs[j] + jnp.dot(onehot, buf_ref[slot, pl.ds(r0, win), :],
                                            preferred_element_type=F32)
        finish(jnp.concatenate(accs, axis=0))


def _combine(off, x1, slot_t, mod, g, ye, *, tt):
    n = x1.shape[0]
    win = min(2 * LANES, tt + BF16_SUBLANES)
    grid_spec = pltpu.PrefetchScalarGridSpec(
        num_scalar_prefetch=1,
        grid=(n // tt,),
        in_specs=[pl.BlockSpec((tt, D_MODEL), lambda t, off: (t, 0)),
                  pl.BlockSpec((tt, N_EXPERTS), lambda t, off: (t, 0)),
                  pl.BlockSpec(mod.shape, lambda t, off: (0, 0)),
                  pl.BlockSpec(g.shape, lambda t, off: (0, 0)),
                  pl.BlockSpec(memory_space=pl.ANY)],
        out_specs=pl.BlockSpec((tt, D_MODEL), lambda t, off: (t, 0)),
        scratch_shapes=[pltpu.VMEM((2, N_EXPERTS * LANES, D_MODEL), BF16),
                        pltpu.VMEM((2, tt + BF16_SUBLANES, D_MODEL), BF16),
                        pltpu.SemaphoreType.DMA((2,)),
                        pltpu.SemaphoreType.DMA((2,))])
    return pl.pallas_call(
        functools.partial(_combine_kernel, win=win, fast_max=COMBINE_FAST_MAX),
        grid_spec=grid_spec,
        out_shape=jax.ShapeDtypeStruct((n, D_MODEL), F32),
        compiler_params=_params(("arbitrary",)),
        name="combine",
    )(off, x1, slot_t, mod, g, ye)


def _rope_tables(n):
    rows = n // GRID_W
    half = HEAD_DIM // 4
    freqs = ROPE_BASE ** (-jnp.arange(half, dtype=F32) / half)
    ang_r = jnp.arange(rows, dtype=jnp.int32).astype(F32)[:, None] * freqs[None, :]
    ang_c = jnp.arange(GRID_W, dtype=jnp.int32).astype(F32)[:, None] * freqs[None, :]
    zr = jnp.zeros((rows, 2 * half), F32)
    zc = jnp.zeros((GRID_W, 2 * half), F32)
    head = lambda a, b: jnp.concatenate([a, b] * 2, axis=1)
    rcos = head(jnp.concatenate([jnp.cos(ang_r)] * 2, axis=1), zr)
    rsin = head(jnp.concatenate([-jnp.sin(ang_r), jnp.sin(ang_r)], axis=1), zr)
    ccos = head(zc, jnp.concatenate([jnp.cos(ang_c)] * 2, axis=1))
    csin = head(zc, jnp.concatenate([-jnp.sin(ang_c), jnp.sin(ang_c)], axis=1))
    return rcos, rsin, ccos, csin


def _block_diag(blocks):
    k = len(blocks)
    r, c = blocks[0].shape
    out = jnp.zeros((k * r, k * c), blocks[0].dtype)
    for i, b in enumerate(blocks):
        out = out.at[i * r:(i + 1) * r, i * c:(i + 1) * c].set(b)
    return out


def _dft_tables(n, cb_n2):
    nb = LANES
    na = n // nb
    gd = FOURIER_GROUP_DIM
    scale = 1.0 / math.sqrt(n * gd)

    def cos_sin(i, j, period):
        ang = ((i * j) % period).astype(F32) * (2.0 * math.pi / period)
        return jnp.cos(ang), jnp.sin(ang)

    ar_g = jnp.arange(gd, dtype=jnp.int32)
    c64, s64 = cos_sin(ar_g[:, None], ar_g[None, :], gd)
    w0 = jnp.concatenate([_block_diag([c64 * scale] * N_FOURIER_GROUPS),
                          _block_diag([-s64 * scale] * N_FOURIER_GROUPS)], axis=1)
    ar_a = jnp.arange(na, dtype=jnp.int32)
    ca, sa = cos_sin(ar_a[:, None], ar_a[None, :], na)
    wa = jnp.concatenate([jnp.concatenate([ca, sa], axis=1),
                          jnp.concatenate([-sa, ca], axis=1)], axis=0)
    ar_b = jnp.arange(nb, dtype=jnp.int32)
    ct, st = cos_sin(ar_a[:, None], ar_b[None, :], n)
    group = lambda t: t.reshape(na, nb // cb_n2, cb_n2).transpose(1, 0, 2)
    cc, sc = cos_sin(ar_b[:, None], ar_b[None, :], nb)
    mm = jnp.concatenate([cc, sc], axis=1)
    return w0.astype(BF16), wa.astype(BF16), group(ct), group(st), mm.astype(BF16)


def _prefix_tables(r_n):
    i = np.arange(LANES)
    ut = (i[:, None] < i[None, :]).astype(np.float32)
    ones = np.ones((LANES, LANES), np.float32)
    r = np.arange(r_n)
    slt = (r[None, :] < r[:, None]).astype(np.float32)
    as_bf16 = lambda a: jnp.asarray(a).astype(BF16)
    return as_bf16(ut), as_bf16(ones), as_bf16(slt)


def kernel(x, c, ctx, c_ctx, w_mod, b_mod, norm_mix_g, norm_ffn_g, w_in, w_four, q_norm_g, k_norm_g,
           w_out, w_router, w_gate, w_up, w_down, final_norm_g):
    batch, n, _ = x.shape
    assert batch == 1 and w_mod.shape[0] == 1
    lc = ctx.shape[1]
    cap = EC_CAPACITY_FACTOR * n // N_EXPERTS
    x2d = x[0]
    ctx2d = ctx[0]

    cc = jnp.zeros((8, D_MODEL), F32).at[0].set(c[0]).at[1].set(c_ctx)
    mod = _mod(cc, w_mod[0], b_mod[0][None, :])

    rope_x = _rope_tables(n)
    zero_c = jnp.zeros((GRID_W, LANES), F32)
    rope_c = (jnp.ones((lc // GRID_W, LANES), F32), jnp.zeros((lc // GRID_W, LANES), F32), zero_c, zero_c)
    inv_hd = jnp.full((HEAD_DIM, HEAD_DIM), 1.0 / HEAD_DIM, BF16)
    bq = _block_diag([inv_hd] * N_Q_HEADS)
    bk = _block_diag([inv_hd] * N_KV_HEADS)
    q_scale = (HEAD_DIM ** -0.5) * math.log2(math.e)
    ratio = jnp.max(jnp.abs(k_norm_g[0])) / jnp.maximum(jnp.max(jnp.abs(q_norm_g[0])) * q_scale, 1e-30)
    shift = jnp.clip(jnp.round(0.5 * jnp.log2(jnp.maximum(ratio, 1e-30))), -60, 60).astype(jnp.int32)
    pow2 = lambda e: lax.bitcast_convert_type((e + 127) << 23, F32)
    gq = jnp.tile(q_norm_g[0], N_Q_HEADS)[None, :] * q_scale * pow2(shift)
    gk = jnp.tile(k_norm_g[0], N_KV_HEADS)[None, :] * pow2(-shift)
    nb = LANES
    na = n // nb
    dft_cb = min(4096, nb * FOURIER_WIDTH)
    w0, wa, tw_c, tw_s, mm = _dft_tables(n, dft_cb // FOURIER_WIDTH)
    w_in_b = w_in[0].astype(BF16)
    g_mix = norm_mix_g[0][None, :]
    tm = min(512, n)
    zr, zi, q_t, k_l, v_t = _inproj(x2d, mod, g_mix, w_in_b, rope_x, bq, bk, gq, gk, w0,
                                    mod_row=0, with_q=True, tm=tm)
    k_c, v_c = _inproj(ctx2d, mod, g_mix, w_in_b, rope_c, bq, bk, gq, gk, w0,
                       mod_row=1, with_q=False, tm=lc)

    attn = _attention(q_t, k_l, v_t, k_c, v_c, tq=min(256, n), tk=min(2048, n // 2))

    br, bi = _dft_a(zr.reshape(na, nb * FOURIER_WIDTH), zi.reshape(na, nb * FOURIER_WIDTH), wa,
                    tw_c, tw_s, cb=dft_cb)
    wf = _block_diag([w_four[0, g] for g in range(N_FOURIER_GROUPS)]).astype(BF16)
    four = _dft_c(br.reshape(na, nb, FOURIER_WIDTH), bi.reshape(na, nb, FOURIER_WIDTH), mm, wf,
                  kb=8).reshape(n, FOURIER_WIDTH)

    w_out_b = w_out[0].astype(BF16)
    wo_f = w_out_b[:FOURIER_WIDTH]
    wo_a = w_out_b[FOURIER_WIDTH:].reshape(N_KV_HEADS, Q_PER_KV * HEAD_DIM, D_MODEL)
    wo_a = jnp.pad(wo_a, ((0, 0), (0, GROUP_PAD - Q_PER_KV * HEAD_DIM), (0, 0)))
    wr_t = w_router[0].T.astype(BF16)
    x1, h2, aff = _outproj(four, attn, x2d, mod, norm_ffn_g[0][None, :], wo_f, wo_a, wr_t, tm=tm)

    r_n = n // LANES
    ut, ones, slt = _prefix_tables(r_n)
    slot3, off3 = _topk(aff.reshape(N_EXPERTS, r_n, LANES), ut, ones, slt, cap=cap)
    off = jnp.concatenate([off3[:, :, 0], jnp.full((N_EXPERTS, 1), cap, jnp.int32)], axis=1)
    slot = slot3.reshape(N_EXPERTS, n)

    tt = (min(512, cap - BF16_SUBLANES) // LANES) * LANES
    xe, gate = _gather(off, slot.reshape(N_EXPERTS, 1, n), aff.reshape(N_EXPERTS, 1, n), h2,
                       cap=cap, tt=tt, eg=GATHER_EXPERTS)
    ye = _ffn(xe, gate, w_gate[0], w_up[0], w_down[0], cap=cap, tf=512, mc=min(1024, cap))
    out = _combine(off, x1, slot.T, mod, final_norm_g[None, :], ye, tt=tt)
    return out[None]
```

```python
import functools
import math

import numpy as np
import jax
import jax.numpy as jnp
from jax import lax
from jax.experimental import pallas as pl
from jax.experimental.pallas import tpu as pltpu

D_MODEL = 1024
GRID_W = 64
HEAD_DIM = 64
N_Q_HEADS = 12
N_KV_HEADS = 4
Q_PER_KV = N_Q_HEADS // N_KV_HEADS
ATTN_WIDTH = N_Q_HEADS * HEAD_DIM
KV_WIDTH = N_KV_HEADS * HEAD_DIM
N_FOURIER_GROUPS = 4
FOURIER_GROUP_DIM = 64
FOURIER_WIDTH = N_FOURIER_GROUPS * FOURIER_GROUP_DIM
IN_WIDTH = FOURIER_WIDTH + ATTN_WIDTH + 2 * KV_WIDTH
ROPE_BASE = 10000.0
N_EXPERTS = 16
EC_CAPACITY_FACTOR = 2
D_EXPERT = 2048
N_MOD = 6
NORM_EPS = 1e-6

LANES = 128
BF16_SUBLANES = 16
GROUP_PAD = 2 * LANES
V_ROWS = HEAD_DIM + BF16_SUBLANES
GATHER_WIN = LANES + BF16_SUBLANES
GATHER_FAST_MAX = LANES
COMBINE_FAST_MAX = LANES - BF16_SUBLANES
GATHER_EXPERTS = 8
GATHER_TILES = 2
INPROJ_SUB = 128
VMEM_LIMIT = 56 * 1024 * 1024

F32 = jnp.float32
BF16 = jnp.bfloat16
F8 = jnp.float8_e4m3fn
F8_MAX = 448.0
F8_LO_SCALE = 16.0
QK_DEPTH = 4 * HEAD_DIM


def _params(sem):
    return pltpu.CompilerParams(dimension_semantics=sem, vmem_limit_bytes=VMEM_LIMIT)


def _mod_kernel(c_ref, w_ref, b_ref, o_ref):
    s = jax.nn.silu(c_ref[...])
    o_ref[...] = jnp.dot(s, w_ref[...], precision=lax.Precision.HIGHEST,
                         preferred_element_type=F32) + b_ref[...]


def _mod(cc, w_mod, b_mod):
    tn = 1024
    n = w_mod.shape[1]
    return pl.pallas_call(
        _mod_kernel,
        grid=(n // tn,),
        in_specs=[pl.BlockSpec((8, D_MODEL), lambda j: (0, 0)),
                  pl.BlockSpec((D_MODEL, tn), lambda j: (0, j)),
                  pl.BlockSpec((1, tn), lambda j: (0, j))],
        out_specs=pl.BlockSpec((8, tn), lambda j: (0, j)),
        out_shape=jax.ShapeDtypeStruct((8, n), F32),
        compiler_params=_params(("arbitrary",)),
        name="mod",
    )(cc, w_mod, b_mod)


def _rms_modulate(x, g, shift, scale):
    ms = jnp.mean(x * x, axis=-1, keepdims=True)
    y = x * lax.rsqrt(ms + NORM_EPS) * g
    return y * (1.0 + scale) + shift


def _head_norm(t, bmat, gain):
    ms = jnp.dot((t * t).astype(BF16), bmat, preferred_element_type=F32)
    return t * lax.rsqrt(ms + NORM_EPS) * gain


def _f8_pair(x, weights):
    inv = 1.0 / F8_LO_SCALE
    hi = jnp.clip(x, -F8_MAX, F8_MAX).astype(F8).astype(F32)
    lo = ((x - hi) * F8_LO_SCALE).astype(F8).astype(F32)
    return (hi, hi * inv, lo, lo * inv) if weights else (hi, lo, hi * inv, lo * inv)


def _rope(t, cos, sin, lo_mask):
    outs = []
    for c in range(t.shape[1] // LANES):
        xc = t[:, c * LANES:(c + 1) * LANES]
        up = pltpu.roll(xc, LANES - 16, axis=1)
        dn = pltpu.roll(xc, 16, axis=1)
        rot = jnp.where(lo_mask, up, dn)
        outs.append(xc * cos + rot * sin)
    return jnp.concatenate(outs, axis=1)


def _inproj_kernel(x_ref, mod_ref, g_ref, w_ref, rcos_ref, rsin_ref, ccos_ref, csin_ref, bq_ref, bk_ref,
                   gq_ref, gk_ref, w0_ref, *out_refs, mod_row, with_q, sub):
    if with_q:
        zr_ref, zi_ref, q_ref, k_ref, v_ref = out_refs
    else:
        k_ref, v_ref = out_refs
    shift = mod_ref[mod_row:mod_row + 1, 0:D_MODEL]
    scale = mod_ref[mod_row:mod_row + 1, D_MODEL:2 * D_MODEL]
    lane = lax.broadcasted_iota(jnp.int32, (sub, LANES), 1)
    lo_mask = (lane % 32) < 16
    k0 = FOURIER_WIDTH + ATTN_WIDTH
    for r0 in range(0, x_ref.shape[0], sub):
        rows = slice(r0, r0 + sub)
        grid_rows = range(r0 // GRID_W, (r0 + sub) // GRID_W)
        h = _rms_modulate(x_ref[rows, :], g_ref[...], shift, scale).astype(BF16)
        cos = jnp.concatenate([rcos_ref[r:r + 1, :] + ccos_ref[...] for r in grid_rows], axis=0)
        sin = jnp.concatenate([rsin_ref[r:r + 1, :] + csin_ref[...] for r in grid_rows], axis=0)
        if with_q:
            p = jnp.dot(h, w_ref[...], preferred_element_type=F32)
            f = p[:, :FOURIER_WIDTH].astype(BF16)
            z = jnp.dot(f, w0_ref[...], preferred_element_type=F32)
            zr_ref[rows, :] = z[:, :FOURIER_WIDTH].astype(BF16)
            zi_ref[rows, :] = z[:, FOURIER_WIDTH:].astype(BF16)
            q = _head_norm(p[:, FOURIER_WIDTH:k0], bq_ref[...], gq_ref[...])
            q = _rope(q, cos, sin, lo_mask)
            for j, piece in enumerate(_f8_pair(q.T, weights=True)):
                q_ref[:, j * HEAD_DIM:(j + 1) * HEAD_DIM, rows] = piece.astype(F8).reshape(
                    N_Q_HEADS, HEAD_DIM, sub)
            pk = p[:, k0:k0 + KV_WIDTH]
            pv = p[:, k0 + KV_WIDTH:]
        else:
            pkv = jnp.dot(h, w_ref[:, k0:], preferred_element_type=F32)
            pk = pkv[:, :KV_WIDTH]
            pv = pkv[:, KV_WIDTH:]
        k = _head_norm(pk, bk_ref[...], gk_ref[...])
        k = _rope(k, cos, sin, lo_mask)
        pieces = _f8_pair(k, weights=False)
        for g in range(N_KV_HEADS):
            cols = slice(g * HEAD_DIM, (g + 1) * HEAD_DIM)
            k_ref[g, rows, :] = jnp.concatenate([pc[:, cols] for pc in pieces], axis=1).astype(F8)
        v_ref[:, 0:HEAD_DIM, rows] = pv.T.astype(BF16).reshape(N_KV_HEADS, HEAD_DIM, sub)
        v_ref[:, HEAD_DIM:V_ROWS, rows] = jnp.ones((N_KV_HEADS, V_ROWS - HEAD_DIM, sub), BF16)


def _inproj(x, mod, g, w_in, rope, bq, bk, gq, gk, w0, *, mod_row, with_q, tm):
    n = x.shape[0]
    rcos, rsin, ccos, csin = rope
    const = lambda shape: pl.BlockSpec(shape, lambda i: (0,) * len(shape))
    in_specs = [pl.BlockSpec((tm, D_MODEL), lambda i: (i, 0)),
                const(mod.shape), const(g.shape), const(w_in.shape),
                pl.BlockSpec((tm // GRID_W, LANES), lambda i: (i, 0)),
                pl.BlockSpec((tm // GRID_W, LANES), lambda i: (i, 0)),
                const(ccos.shape), const(csin.shape),
                const(bq.shape), const(bk.shape), const(gq.shape), const(gk.shape),
                const(w0.shape)]
    kv_specs = [pl.BlockSpec((N_KV_HEADS, tm, QK_DEPTH), lambda i: (0, i, 0)),
                pl.BlockSpec((N_KV_HEADS, V_ROWS, tm), lambda i: (0, 0, i))]
    kv_shapes = [jax.ShapeDtypeStruct((N_KV_HEADS, n, QK_DEPTH), F8),
                 jax.ShapeDtypeStruct((N_KV_HEADS, V_ROWS, n), BF16)]
    if with_q:
        out_specs = [pl.BlockSpec((tm, FOURIER_WIDTH), lambda i: (i, 0)),
                     pl.BlockSpec((tm, FOURIER_WIDTH), lambda i: (i, 0)),
                     pl.BlockSpec((N_Q_HEADS, QK_DEPTH, tm), lambda i: (0, 0, i))] + kv_specs
        out_shapes = [jax.ShapeDtypeStruct((n, FOURIER_WIDTH), BF16),
                      jax.ShapeDtypeStruct((n, FOURIER_WIDTH), BF16),
                      jax.ShapeDtypeStruct((N_Q_HEADS, QK_DEPTH, n), F8)] + kv_shapes
    else:
        out_specs, out_shapes = kv_specs, kv_shapes
    return pl.pallas_call(
        functools.partial(_inproj_kernel, mod_row=mod_row, with_q=with_q, sub=min(tm, INPROJ_SUB)),
        grid=(n // tm,),
        in_specs=in_specs,
        out_specs=out_specs,
        out_shape=out_shapes,
        compiler_params=_params(("arbitrary",)),
        name="inproj_x" if with_q else "inproj_ctx",
    )(x, mod, g, w_in, rcos, rsin, ccos, csin, bq, bk, gq, gk, w0)


def _attn_kernel(q_ref, k_ref, v_ref, kc_ref, vc_ref, o_ref, s_ref, acc_ref, *, tk):
    tq = q_ref.shape[2]
    n = k_ref.shape[1]
    lc = kc_ref.shape[1]
    nch = n // tk
    qs = jnp.concatenate([q_ref[j] for j in range(Q_PER_KV)], axis=1)

    def scores_latent(c, slot):
        start = pl.multiple_of(c * tk, tk)
        s = jnp.dot(k_ref[0, pl.ds(start, tk), :], qs, preferred_element_type=F32)
        s_ref[slot] = s
        return jnp.max(s, axis=0, keepdims=True)

    def scores_context():
        s = jnp.dot(kc_ref[0], qs, preferred_element_type=F32)
        s_ref[0, 0:lc, :] = s
        return jnp.max(s, axis=0, keepdims=True)

    def v_latent(c):
        return v_ref[0, :, pl.ds(pl.multiple_of(c * tk, tk), tk)]

    def consume(slot, rows, vb, m, mx):
        s = s_ref[slot, 0:rows, :]
        m_new = jnp.maximum(m, mx)
        alpha = jnp.exp2(m - m_new)
        p = jnp.exp2(s - m_new).astype(BF16)
        acc_ref[...] = alpha * acc_ref[...] + jnp.dot(vb, p, preferred_element_type=F32)
        return m_new

    def pair(c0, carry, prefetch):
        m, mx0 = carry
        mx1 = scores_latent(c0 + 1, 1)
        m = consume(0, tk, v_latent(c0), m, mx0)
        mx0 = prefetch()
        return consume(1, tk, v_latent(c0 + 1), m, mx1), mx0

    def body(i, carry):
        return pair(2 * i, carry, lambda: scores_latent(2 * i + 2, 0))

    w = Q_PER_KV * tq
    acc_ref[...] = jnp.zeros_like(acc_ref)
    mx0 = scores_latent(0, 0)
    carry = lax.fori_loop(0, nch // 2 - 1, body, (jnp.full((1, w), -1e30, F32), mx0))
    m, mx_ctx = pair(nch - 2, carry, scores_context)
    consume(0, lc, vc_ref[0], m, mx_ctx)
    acc = acc_ref[...]
    o = acc[0:HEAD_DIM] / acc[HEAD_DIM:HEAD_DIM + 1]
    ot = jnp.concatenate([o, jnp.zeros_like(o)], axis=0).T
    h0, h1, h2 = (ot[j * tq:(j + 1) * tq] for j in range(Q_PER_KV))
    o_ref[0] = jnp.concatenate([h0 + pltpu.roll(h1, HEAD_DIM, axis=1), h2], axis=1).astype(BF16)


def _attention(q, k, v, kc, vc, *, tq, tk):
    n = k.shape[1]
    lc = kc.shape[1]
    kd = k.shape[2]
    gw = GROUP_PAD
    return pl.pallas_call(
        functools.partial(_attn_kernel, tk=tk),
        grid=(N_KV_HEADS, n // tq),
        in_specs=[pl.BlockSpec((Q_PER_KV, kd, tq), lambda g, i: (g, 0, i)),
                  pl.BlockSpec((1, n, kd), lambda g, i: (g, 0, 0)),
                  pl.BlockSpec((1, V_ROWS, n), lambda g, i: (g, 0, 0)),
                  pl.BlockSpec((1, lc, kd), lambda g, i: (g, 0, 0)),
                  pl.BlockSpec((1, V_ROWS, lc), lambda g, i: (g, 0, 0))],
        out_specs=pl.BlockSpec((1, tq, gw), lambda g, i: (g, i, 0)),
        out_shape=jax.ShapeDtypeStruct((N_KV_HEADS, n, gw), BF16),
        scratch_shapes=[pltpu.VMEM((2, tk, Q_PER_KV * tq), F32),
                        pltpu.VMEM((V_ROWS, Q_PER_KV * tq), F32)],
        compiler_params=_params(("arbitrary", "arbitrary")),
        name="attention",
    )(q, k, v, kc, vc)


def _dft_a_kernel(zr_ref, zi_ref, w_ref, tc_ref, ts_ref, br_ref, bi_ref):
    na = zr_ref.shape[0]
    fw = FOURIER_WIDTH
    zz = jnp.concatenate([zr_ref[...], zi_ref[...]], axis=0)
    a = jnp.dot(w_ref[...], zz, preferred_element_type=F32)
    tc = tc_ref[0]
    ts = ts_ref[0]
    for j in range(tc.shape[1]):
        ar = a[:na, j * fw:(j + 1) * fw]
        ai = a[na:, j * fw:(j + 1) * fw]
        c = tc[:, j:j + 1]
        s = ts[:, j:j + 1]
        br_ref[:, j * fw:(j + 1) * fw] = (ar * c + ai * s).astype(BF16)
        bi_ref[:, j * fw:(j + 1) * fw] = (ai * c - ar * s).astype(BF16)


def _dft_a(zr2, zi2, wa, tc3, ts3, *, cb):
    na, cols = zr2.shape
    blk = pl.BlockSpec((na, cb), lambda j: (0, j))
    tw = pl.BlockSpec((1,) + tc3.shape[1:], lambda j: (j, 0, 0))
    return pl.pallas_call(
        _dft_a_kernel,
        grid=(cols // cb,),
        in_specs=[blk, blk, pl.BlockSpec(wa.shape, lambda j: (0, 0)), tw, tw],
        out_specs=[blk, blk],
        out_shape=[jax.ShapeDtypeStruct((na, cols), BF16)] * 2,
        compiler_params=_params(("arbitrary",)),
        name="dft_a",
    )(zr2, zi2, wa, tc3, ts3)


def _dft_c_kernel(br_ref, bi_ref, m_ref, wf_ref, o_ref):
    kb = br_ref.shape[0]
    for j in range(kb):
        ab = jnp.concatenate([br_ref[j], bi_ref[j]], axis=0)
        y = jnp.dot(m_ref[...], ab, preferred_element_type=F32)
        o_ref[:, j, :] = jnp.dot(y.astype(BF16), wf_ref[...], preferred_element_type=F32)


def _dft_c(br3, bi3, mm, wf, *, kb):
    na, nb, fw = br3.shape
    blk = pl.BlockSpec((kb, nb, fw), lambda i: (i, 0, 0))
    return pl.pallas_call(
        _dft_c_kernel,
        grid=(na // kb,),
        in_specs=[blk, blk,
                  pl.BlockSpec(mm.shape, lambda i: (0, 0)),
                  pl.BlockSpec(wf.shape, lambda i: (0, 0))],
        out_specs=pl.BlockSpec((nb, kb, fw), lambda i: (0, i, 0)),
        out_shape=jax.ShapeDtypeStruct((nb, na, fw), F32),
        compiler_params=_params(("arbitrary",)),
        name="dft_c",
    )(br3, bi3, mm, wf)


def _outproj_kernel(four_ref, attn_ref, x_ref, mod_ref, g_ref, wf_ref, wa_ref, wr_ref,
                    x1_ref, h_ref, aff_ref):
    gate1 = mod_ref[0:1, 2 * D_MODEL:3 * D_MODEL]
    shift2 = mod_ref[0:1, 3 * D_MODEL:4 * D_MODEL]
    scale2 = mod_ref[0:1, 4 * D_MODEL:5 * D_MODEL]
    mix = jnp.dot(four_ref[...].astype(BF16), wf_ref[...], preferred_element_type=F32)
    for g in range(N_KV_HEADS):
        mix = mix + jnp.dot(attn_ref[g], wa_ref[g], preferred_element_type=F32)
    x1 = x_ref[...] + gate1 * mix
    x1_ref[...] = x1
    h = _rms_modulate(x1, g_ref[...], shift2, scale2).astype(BF16)
    h_ref[...] = h
    logits = lax.dot_general(wr_ref[...], h, (((1,), (1,)), ((), ())),
                             preferred_element_type=F32)
    mx = jnp.max(logits, axis=0, keepdims=True)
    e = jnp.exp(logits - mx)
    aff_ref[...] = e / jnp.sum(e, axis=0, keepdims=True)


def _outproj(four, attn, x, mod, g, wo_f, wo_a, wr_t, *, tm):
    n = x.shape[0]
    gw = GROUP_PAD
    const = lambda shape: pl.BlockSpec(shape, lambda i: (0,) * len(shape))
    return pl.pallas_call(
        _outproj_kernel,
        grid=(n // tm,),
        in_specs=[pl.BlockSpec((tm, FOURIER_WIDTH), lambda i: (i, 0)),
                  pl.BlockSpec((N_KV_HEADS, tm, gw), lambda i: (0, i, 0)),
                  pl.BlockSpec((tm, D_MODEL), lambda i: (i, 0)),
                  const(mod.shape), const(g.shape), const(wo_f.shape), const(wo_a.shape),
                  const(wr_t.shape)],
        out_specs=[pl.BlockSpec((tm, D_MODEL), lambda i: (i, 0)),
                   pl.BlockSpec((tm, D_MODEL), lambda i: (i, 0)),
                   pl.BlockSpec((N_EXPERTS, tm), lambda i: (0, i))],
        out_shape=[jax.ShapeDtypeStruct((n, D_MODEL), F32),
                   jax.ShapeDtypeStruct((n, D_MODEL), BF16),
                   jax.ShapeDtypeStruct((N_EXPERTS, n), F32)],
        compiler_params=_params(("arbitrary",)),
        name="outproj",
    )(four, attn, x, mod, g, wo_f, wo_a, wr_t)


def _count(m):
    return jnp.sum(jnp.sum(m, axis=2, keepdims=True), axis=1, keepdims=True)


def _topk_kernel(aff_ref, ut_ref, ones_ref, slt_ref, slot_ref, off_ref, *, cap):
    e_n, r_n, _ = aff_ref.shape
    aff = aff_ref[...]
    capf = jnp.float32(cap)

    def count_ge(th):
        return _count(jnp.where(aff >= th, 1.0, 0.0))

    def search(i, t):
        cand = t | jnp.left_shift(jnp.int32(1), 30 - i)
        c = count_ge(lax.bitcast_convert_type(cand, F32))
        return jnp.where(c >= capf, cand, t)

    tbits = lax.fori_loop(0, 31, search, jnp.zeros((e_n, 1, 1), jnp.int32))
    thr = lax.bitcast_convert_type(tbits, F32)
    ulp = lax.bitcast_convert_type(tbits + 1, F32) - thr

    def refine(_, carry):
        t, step = carry
        step = step * 0.5
        cand = t + step
        return jnp.where(count_ge(cand) >= capf, cand, t), step

    thr, _ = lax.fori_loop(0, 29, refine, (thr, ulp))
    gt = jnp.where(aff > thr, 1.0, 0.0)
    eq = jnp.where(aff == thr, 1.0, 0.0)
    need = capf - _count(gt)

    def excl_prefix(m):
        m2 = m.reshape(e_n * r_n, LANES).astype(BF16)
        within = jnp.dot(m2, ut_ref[...], preferred_element_type=F32).reshape(e_n, r_n, LANES)
        tot = jnp.dot(m2, ones_ref[...], preferred_element_type=F32).reshape(e_n, r_n, LANES)
        rows = jnp.stack([jnp.dot(slt_ref[...], tot[e].astype(BF16), preferred_element_type=F32)
                          for e in range(e_n)], axis=0)
        return within + rows, rows

    pe, _ = excl_prefix(eq)
    sel = jnp.maximum(gt, eq * jnp.where(pe < need, 1.0, 0.0))
    pos, rows = excl_prefix(sel)
    slot_ref[...] = jnp.where(sel > 0.5, pos, -1.0).astype(jnp.int32)
    off_ref[...] = rows.astype(jnp.int32)


def _topk(aff3, ut, ones, slt, *, cap):
    shp = aff3.shape
    full = lambda a: pl.BlockSpec(a.shape, lambda i: (0,) * a.ndim)
    return pl.pallas_call(
        functools.partial(_topk_kernel, cap=cap),
        grid=(1,),
        in_specs=[full(aff3), full(ut), full(ones), full(slt)],
        out_specs=[pl.BlockSpec(shp, lambda i: (0, 0, 0))] * 2,
        out_shape=[jax.ShapeDtypeStruct(shp, jnp.int32)] * 2,
        compiler_params=_params(("arbitrary",)),
        name="topk",
    )(aff3, ut, ones, slt)


def _window_start(off):
    return pl.multiple_of((off // BF16_SUBLANES) * BF16_SUBLANES, BF16_SUBLANES)


def _tile_max_count(off_ref, experts, t, nsub):
    cnts = [off_ref[e, (t + 1) * nsub] - off_ref[e, t * nsub] for e in experts]
    return functools.reduce(jnp.maximum, cnts)


def _gather_kernel(off_ref, slot_ref, aff_ref, h_ref, xe_ref, gate_ref, *, tt, fast_max):
    p = pl.program_id(0)
    t = pl.program_id(1)
    eg = slot_ref.shape[0]
    tiles = slot_ref.shape[2] // tt
    nsub = tt // LANES
    experts = [p * eg + j for j in range(eg)]

    @pl.when(t == 0)
    def _():
        xe_ref[...] = jnp.zeros_like(xe_ref)
        gate_ref[...] = jnp.zeros_like(gate_ref)

    def scatter_rows(j, ws, hit, res, aff_row):
        xe_ref[j, pl.ds(ws, GATHER_WIN), :] += res.astype(BF16)
        g = jnp.sum(jnp.where(hit, aff_row, 0.0), axis=1, keepdims=True)
        gate_ref[j, pl.ds(ws, GATHER_WIN), :] += jnp.broadcast_to(g, (GATHER_WIN, LANES))

    def gather_tile(tile, base):
        fast = _tile_max_count(off_ref, experts, tile, nsub) <= fast_max

        @pl.when(fast)
        def _():
            rows = lax.broadcasted_iota(jnp.int32, (GATHER_WIN, tt), 0)
            toks = slice(base, base + tt)
            starts = [_window_start(off_ref[e, tile * nsub]) for e in experts]
            hits = [rows == (slot_ref[j, :, toks] - starts[j]) for j in range(eg)]
            onehot = jnp.concatenate([jnp.where(hit, 1.0, 0.0).astype(BF16) for hit in hits], axis=0)
            res = jnp.dot(onehot, h_ref[toks, :], preferred_element_type=F32)
            for j in range(eg):
                scatter_rows(j, starts[j], hits[j], res[j * GATHER_WIN:(j + 1) * GATHER_WIN],
                             aff_ref[j, :, toks])

        @pl.when(jnp.logical_not(fast))
        def _():
            rows = lax.broadcasted_iota(jnp.int32, (GATHER_WIN, LANES), 0)
            for j in range(eg):
                for b in range(nsub):
                    ws = _window_start(off_ref[experts[j], tile * nsub + b])
                    toks = slice(base + b * LANES, base + (b + 1) * LANES)
                    hit = rows == (slot_ref[j, :, toks] - ws)
                    res = jnp.dot(jnp.where(hit, 1.0, 0.0).astype(BF16), h_ref[toks, :],
                                  preferred_element_type=F32)
                    scatter_rows(j, ws, hit, res, aff_ref[j, :, toks])

    for u in range(tiles):
        gather_tile(t * tiles + u, u * tt)


def _gather(off, slot3, aff3, h, *, cap, tt, eg):
    n = h.shape[0]
    capp = cap + GATHER_WIN
    tb = min(n, GATHER_TILES * tt)
    grid_spec = pltpu.PrefetchScalarGridSpec(
        num_scalar_prefetch=1,
        grid=(N_EXPERTS // eg, n // tb),
        in_specs=[pl.BlockSpec((eg, 1, tb), lambda p, t, off: (p, 0, t)),
                  pl.BlockSpec((eg, 1, tb), lambda p, t, off: (p, 0, t)),
                  pl.BlockSpec((tb, D_MODEL), lambda p, t, off: (t, 0))],
        out_specs=[pl.BlockSpec((eg, capp, D_MODEL), lambda p, t, off: (p, 0, 0),
                                pipeline_mode=pl.Buffered(1)),
                   pl.BlockSpec((eg, capp, LANES), lambda p, t, off: (p, 0, 0),
                                pipeline_mode=pl.Buffered(1))])
    return pl.pallas_call(
        functools.partial(_gather_kernel, tt=tt, fast_max=GATHER_FAST_MAX),
        grid_spec=grid_spec,
        out_shape=[jax.ShapeDtypeStruct((N_EXPERTS, capp, D_MODEL), BF16),
                   jax.ShapeDtypeStruct((N_EXPERTS, capp, LANES), F32)],
        compiler_params=_params(("arbitrary", "arbitrary")),
        name="gather",
    )(off, slot3, aff3, h)


def _ffn_kernel(xe_ref, gate_ref, wg_ref, wu_ref, wd_ref, ye_ref, acc_ref, *, mc):
    f = pl.program_id(1)
    cap = xe_ref.shape[1]
    wg = wg_ref[0].astype(BF16)
    wu = wu_ref[0].astype(BF16)
    wd = wd_ref[0].astype(BF16)

    @pl.when(f == 0)
    def _():
        acc_ref[...] = jnp.zeros_like(acc_ref)

    for m in range(cap // mc):
        rows = slice(m * mc, (m + 1) * mc)
        xm = xe_ref[0, rows, :]
        gate = jnp.dot(xm, wg, preferred_element_type=F32)
        up = jnp.dot(xm, wu, preferred_element_type=F32)
        hid = (jax.nn.silu(gate) * up).astype(BF16)
        acc_ref[rows, :] += jnp.dot(hid, wd, preferred_element_type=F32)

    @pl.when(f == pl.num_programs(1) - 1)
    def _():
        for m in range(cap // mc):
            rows = slice(m * mc, (m + 1) * mc)
            gate = jnp.concatenate([gate_ref[0, rows, :]] * (D_MODEL // LANES), axis=1)
            ye_ref[0, rows, :] = (acc_ref[rows, :] * gate).astype(BF16)


def _ffn(xe, gate, w_gate, w_up, w_down, *, cap, tf, mc):
    e_n = xe.shape[0]
    return pl.pallas_call(
        functools.partial(_ffn_kernel, mc=mc),
        grid=(e_n, D_EXPERT // tf),
        in_specs=[pl.BlockSpec((1, cap, D_MODEL), lambda e, f: (e, 0, 0)),
                  pl.BlockSpec((1, cap, LANES), lambda e, f: (e, 0, 0)),
                  pl.BlockSpec((1, D_MODEL, tf), lambda e, f: (e, 0, f)),
                  pl.BlockSpec((1, D_MODEL, tf), lambda e, f: (e, 0, f)),
                  pl.BlockSpec((1, tf, D_MODEL), lambda e, f: (e, f, 0))],
        out_specs=pl.BlockSpec((1, cap, D_MODEL), lambda e, f: (e, 0, 0)),
        out_shape=jax.ShapeDtypeStruct((e_n, cap, D_MODEL), BF16),
        scratch_shapes=[pltpu.VMEM((cap, D_MODEL), F32)],
        compiler_params=_params(("arbitrary", "arbitrary")),
        name="ffn",
    )(xe, gate, w_gate, w_up, w_down)


def _combine_kernel(off_ref, x1_ref, slot_ref, mod_ref, g_ref, ye_ref, o_ref,
                    wbuf_ref, buf_ref, wsem_ref, sem_ref, *, win, fast_max):
    t = pl.program_id(0)
    tt = x1_ref.shape[0]
    cap = ye_ref.shape[1]
    nsub = tt // LANES
    big = tt + BF16_SUBLANES
    experts = list(range(N_EXPERTS))

    def finish(moe):
        gate2 = mod_ref[0:1, 5 * D_MODEL:6 * D_MODEL]
        x2 = x1_ref[...] + gate2 * moe
        ms = jnp.mean(x2 * x2, axis=-1, keepdims=True)
        o_ref[...] = x2 * lax.rsqrt(ms + NORM_EPS) * g_ref[...]

    def clamped_start(e, rows, tile=t):
        return pl.multiple_of(jnp.minimum(_window_start(off_ref[e, tile * nsub]), cap - rows),
                              BF16_SUBLANES)

    def is_fast(tile):
        return _tile_max_count(off_ref, experts, tile, nsub) <= fast_max

    def window_copies(tile):
        buf = tile % 2
        return [pltpu.make_async_copy(ye_ref.at[e, pl.ds(clamped_start(e, LANES, tile), LANES), :],
                                      wbuf_ref.at[buf, pl.ds(e * LANES, LANES), :], wsem_ref.at[buf])
                for e in experts]

    def start_windows(tile):
        for cp in window_copies(tile):
            cp.start()

    fast = is_fast(t)
    pl.when(jnp.logical_and(t == 0, fast))(lambda: start_windows(t))
    nxt = jnp.minimum(t + 1, pl.num_programs(0) - 1)
    pl.when(jnp.logical_and(t + 1 < pl.num_programs(0), is_fast(nxt)))(lambda: start_windows(nxt))

    @pl.when(fast)
    def _():
        cols = lax.broadcasted_iota(jnp.int32, (tt, LANES), 1)
        onehot = jnp.concatenate(
            [jnp.where(cols == (slot_ref[:, e:e + 1] - clamped_start(e, LANES)), 1.0, 0.0).astype(BF16)
             for e in experts], axis=1)
        for cp in window_copies(t):
            cp.wait()
        finish(jnp.dot(onehot, wbuf_ref[t % 2], preferred_element_type=F32))

    @pl.when(jnp.logical_not(fast))
    def _():
        def copy(e, slot):
            return pltpu.make_async_copy(ye_ref.at[e, pl.ds(clamped_start(e, big), big), :],
                                         buf_ref.at[slot], sem_ref.at[slot])

        copy(0, 0).start()
        cols = lax.broadcasted_iota(jnp.int32, (LANES, win), 1)
        accs = [jnp.zeros((LANES, D_MODEL), F32) for _ in range(nsub)]
        for e in experts:
            slot = e % 2
            if e + 1 < N_EXPERTS:
                copy(e + 1, 1 - slot).start()
            copy(e, slot).wait()
            wb = clamped_start(e, big)
            for j in range(nsub):
                ws = jnp.minimum(_window_start(off_ref[e, t * nsub + j]), wb + (big - win))
                rel = slot_ref[j * LANES:(j + 1) * LANES, e:e + 1] - ws
                onehot = jnp.where(cols == rel, 1.0, 0.0).astype(BF16)
                r0 = pl.multiple_of(ws - wb, BF16_SUBLANES)
                accs[j] = accs[j] + jnp.dot(onehot, buf_ref[slot, pl.ds(r0, win), :],
                                            preferred_element_type=F32)
        finish(jnp.concatenate(accs, axis=0))


def _combine(off, x1, slot_t, mod, g, ye, *, tt):
    n = x1.shape[0]
    win = min(2 * LANES, tt + BF16_SUBLANES)
    grid_spec = pltpu.PrefetchScalarGridSpec(
        num_scalar_prefetch=1,
        grid=(n // tt,),
        in_specs=[pl.BlockSpec((tt, D_MODEL), lambda t, off: (t, 0)),
                  pl.BlockSpec((tt, N_EXPERTS), lambda t, off: (t, 0)),
                  pl.BlockSpec(mod.shape, lambda t, off: (0, 0)),
                  pl.BlockSpec(g.shape, lambda t, off: (0, 0)),
                  pl.BlockSpec(memory_space=pl.ANY)],
        out_specs=pl.BlockSpec((tt, D_MODEL), lambda t, off: (t, 0)),
        scratch_shapes=[pltpu.VMEM((2, N_EXPERTS * LANES, D_MODEL), BF16),
                        pltpu.VMEM((2, tt + BF16_SUBLANES, D_MODEL), BF16),
                        pltpu.SemaphoreType.DMA((2,)),
                        pltpu.SemaphoreType.DMA((2,))])
    return pl.pallas_call(
        functools.partial(_combine_kernel, win=win, fast_max=COMBINE_FAST_MAX),
        grid_spec=grid_spec,
        out_shape=jax.ShapeDtypeStruct((n, D_MODEL), F32),
        compiler_params=_params(("arbitrary",)),
        name="combine",
    )(off, x1, slot_t, mod, g, ye)


def _rope_tables(n):
    rows = n // GRID_W
    half = HEAD_DIM // 4
    freqs = ROPE_BASE ** (-jnp.arange(half, dtype=F32) / half)
    ang_r = jnp.arange(rows, dtype=jnp.int32).astype(F32)[:, None] * freqs[None, :]
    ang_c = jnp.arange(GRID_W, dtype=jnp.int32).astype(F32)[:, None] * freqs[None, :]
    zr = jnp.zeros((rows, 2 * half), F32)
    zc = jnp.zeros((GRID_W, 2 * half), F32)
    head = lambda a, b: jnp.concatenate([a, b] * 2, axis=1)
    rcos = head(jnp.concatenate([jnp.cos(ang_r)] * 2, axis=1), zr)
    rsin = head(jnp.concatenate([-jnp.sin(ang_r), jnp.sin(ang_r)], axis=1), zr)
    ccos = head(zc, jnp.concatenate([jnp.cos(ang_c)] * 2, axis=1))
    csin = head(zc, jnp.concatenate([-jnp.sin(ang_c), jnp.sin(ang_c)], axis=1))
    return rcos, rsin, ccos, csin


def _block_diag(blocks):
    k = len(blocks)
    r, c = blocks[0].shape
    out = jnp.zeros((k * r, k * c), blocks[0].dtype)
    for i, b in enumerate(blocks):
        out = out.at[i * r:(i + 1) * r, i * c:(i + 1) * c].set(b)
    return out


def _dft_tables(n, cb_n2):
    nb = LANES
    na = n // nb
    gd = FOURIER_GROUP_DIM
    scale = 1.0 / math.sqrt(n * gd)

    def cos_sin(i, j, period):
        ang = ((i * j) % period).astype(F32) * (2.0 * math.pi / period)
        return jnp.cos(ang), jnp.sin(ang)

    ar_g = jnp.arange(gd, dtype=jnp.int32)
    c64, s64 = cos_sin(ar_g[:, None], ar_g[None, :], gd)
    w0 = jnp.concatenate([_block_diag([c64 * scale] * N_FOURIER_GROUPS),
                          _block_diag([-s64 * scale] * N_FOURIER_GROUPS)], axis=1)
    ar_a = jnp.arange(na, dtype=jnp.int32)
    ca, sa = cos_sin(ar_a[:, None], ar_a[None, :], na)
    wa = jnp.concatenate([jnp.concatenate([ca, sa], axis=1),
                          jnp.concatenate([-sa, ca], axis=1)], axis=0)
    ar_b = jnp.arange(nb, dtype=jnp.int32)
    ct, st = cos_sin(ar_a[:, None], ar_b[None, :], n)
    group = lambda t: t.reshape(na, nb // cb_n2, cb_n2).transpose(1, 0, 2)
    cc, sc = cos_sin(ar_b[:, None], ar_b[None, :], nb)
    mm = jnp.concatenate([cc, sc], axis=1)
    return w0.astype(BF16), wa.astype(BF16), group(ct), group(st), mm.astype(BF16)


def _prefix_tables(r_n):
    i = np.arange(LANES)
    ut = (i[:, None] < i[None, :]).astype(np.float32)
    ones = np.ones((LANES, LANES), np.float32)
    r = np.arange(r_n)
    slt = (r[None, :] < r[:, None]).astype(np.float32)
    as_bf16 = lambda a: jnp.asarray(a).astype(BF16)
    return as_bf16(ut), as_bf16(ones), as_bf16(slt)


def kernel(x, c, ctx, c_ctx, w_mod, b_mod, norm_mix_g, norm_ffn_g, w_in, w_four, q_norm_g, k_norm_g,
           w_out, w_router, w_gate, w_up, w_down, final_norm_g):
    batch, n, _ = x.shape
    assert batch == 1 and w_mod.shape[0] == 1
    lc = ctx.shape[1]
    cap = EC_CAPACITY_FACTOR * n // N_EXPERTS
    x2d = x[0]
    ctx2d = ctx[0]

    cc = jnp.zeros((8, D_MODEL), F32).at[0].set(c[0]).at[1].set(c_ctx)
    mod = _mod(cc, w_mod[0], b_mod[0][None, :])

    rope_x = _rope_tables(n)
    zero_c = jnp.zeros((GRID_W, LANES), F32)
    rope_c = (jnp.ones((lc // GRID_W, LANES), F32), jnp.zeros((lc // GRID_W, LANES), F32), zero_c, zero_c)
    inv_hd = jnp.full((HEAD_DIM, HEAD_DIM), 1.0 / HEAD_DIM, BF16)
    bq = _block_diag([inv_hd] * N_Q_HEADS)
    bk = _block_diag([inv_hd] * N_KV_HEADS)
    q_scale = (HEAD_DIM ** -0.5) * math.log2(math.e)
    ratio = jnp.max(jnp.abs(k_norm_g[0])) / jnp.maximum(jnp.max(jnp.abs(q_norm_g[0])) * q_scale, 1e-30)
    shift = jnp.clip(jnp.round(0.5 * jnp.log2(jnp.maximum(ratio, 1e-30))), -60, 60).astype(jnp.int32)
    pow2 = lambda e: lax.bitcast_convert_type((e + 127) << 23, F32)
    gq = jnp.tile(q_norm_g[0], N_Q_HEADS)[None, :] * q_scale * pow2(shift)
    gk = jnp.tile(k_norm_g[0], N_KV_HEADS)[None, :] * pow2(-shift)
    nb = LANES
    na = n // nb
    dft_cb = min(4096, nb * FOURIER_WIDTH)
    w0, wa, tw_c, tw_s, mm = _dft_tables(n, dft_cb // FOURIER_WIDTH)
    w_in_b = w_in[0].astype(BF16)
    g_mix = norm_mix_g[0][None, :]
    tm = min(512, n)
    zr, zi, q_t, k_l, v_t = _inproj(x2d, mod, g_mix, w_in_b, rope_x, bq, bk, gq, gk, w0,
                                    mod_row=0, with_q=True, tm=tm)
    k_c, v_c = _inproj(ctx2d, mod, g_mix, w_in_b, rope_c, bq, bk, gq, gk, w0,
                       mod_row=1, with_q=False, tm=lc)

    attn = _attention(q_t, k_l, v_t, k_c, v_c, tq=min(256, n), tk=min(2048, n // 2))

    br, bi = _dft_a(zr.reshape(na, nb * FOURIER_WIDTH), zi.reshape(na, nb * FOURIER_WIDTH), wa,
                    tw_c, tw_s, cb=dft_cb)
    wf = _block_diag([w_four[0, g] for g in range(N_FOURIER_GROUPS)]).astype(BF16)
    four = _dft_c(br.reshape(na, nb, FOURIER_WIDTH), bi.reshape(na, nb, FOURIER_WIDTH), mm, wf,
                  kb=8).reshape(n, FOURIER_WIDTH)

    w_out_b = w_out[0].astype(BF16)
    wo_f = w_out_b[:FOURIER_WIDTH]
    wo_a = w_out_b[FOURIER_WIDTH:].reshape(N_KV_HEADS, Q_PER_KV * HEAD_DIM, D_MODEL)
    wo_a = jnp.pad(wo_a, ((0, 0), (0, GROUP_PAD - Q_PER_KV * HEAD_DIM), (0, 0)))
    wr_t = w_router[0].T.astype(BF16)
    x1, h2, aff = _outproj(four, attn, x2d, mod, norm_ffn_g[0][None, :], wo_f, wo_a, wr_t, tm=tm)

    r_n = n // LANES
    ut, ones, slt = _prefix_tables(r_n)
    slot3, off3 = _topk(aff.reshape(N_EXPERTS, r_n, LANES), ut, ones, slt, cap=cap)
    off = jnp.concatenate([off3[:, :, 0], jnp.full((N_EXPERTS, 1), cap, jnp.int32)], axis=1)
    slot = slot3.reshape(N_EXPERTS, n)

    tt = (min(512, cap - BF16_SUBLANES) // LANES) * LANES
    xe, gate = _gather(off, slot.reshape(N_EXPERTS, 1, n), aff.reshape(N_EXPERTS, 1, n), h2,
                       cap=cap, tt=tt, eg=GATHER_EXPERTS)
    ye = _ffn(xe, gate, w_gate[0], w_up[0], w_down[0], cap=cap, tf=512, mc=min(1024, cap))
    out = _combine(off, x1, slot.T, mod, final_norm_g[None, :], ye, tt=tt)
    return out[None]
```
